```python
import jax, jax.numpy as jnp
from jax import lax
import numpy as np

D_MODEL = 1024
BATCH = 8
SEQ = 2048
DEPTH = 2

HEAD_DIM = 64
NSA_HEADS = 8
NSA_KV_GROUPS = 2
NSA_WIDTH = NSA_HEADS * HEAD_DIM
CMP_STRIDE = 16
CMP_BLOCK = 2 * CMP_STRIDE
CMP_HIDDEN = 128
SLC_BLOCK = 64
SLC_TOPN = 16
NSA_WINDOW = 512
SLC_QUERY_BLOCK = 64
SWA_HEADS = 4
SWA_KV_HEADS = 2
SWA_WIDTH = SWA_HEADS * HEAD_DIM
SWA_WINDOW = 128
RWKV_HEADS = 4
RWKV_WIDTH = RWKV_HEADS * HEAD_DIM
DECAY_LORA = 64
ICLR_LORA = 64
VRES_LORA = 32
RWKV_SHIFT_WIDTH = 3 * RWKV_WIDTH + DECAY_LORA + ICLR_LORA

N_BRANCHES = 3
QUERY_BLOCK = 128
NORM_EPS = 1e-6
GN_EPS = 64e-5
NEG_INF = -1e30
FORCE = 1e9
F32 = jnp.float32

IN_SEGMENTS = (
    ("a_q", NSA_WIDTH),
    ("a_kv_cmp", 2 * NSA_KV_GROUPS * HEAD_DIM),
    ("a_kv_slc", 2 * NSA_KV_GROUPS * HEAD_DIM),
    ("a_kv_win", 2 * NSA_KV_GROUPS * HEAD_DIM),
    ("a_gate", 3 * NSA_HEADS),
    ("a_z", NSA_WIDTH),
    ("b_q", SWA_WIDTH),
    ("b_kv", 2 * SWA_KV_HEADS * HEAD_DIM),
    ("b_z", SWA_WIDTH),
    ("c_shift", RWKV_SHIFT_WIDTH),
    ("c_z", RWKV_WIDTH),
    ("merge", N_BRANCHES * D_MODEL),
)
N_IN = sum(size for _, size in IN_SEGMENTS)

kernel_name = "hybrid_nsa_swa_rwkv7_gated_parallel"


def rms_norm(x, g):
    xf = x.astype(F32)
    y = xf * lax.rsqrt(jnp.mean(xf * xf, axis=-1, keepdims=True) + NORM_EPS)
    return (y * g.astype(F32)).astype(x.dtype)


def split_columns(h):
    out = {}
    off = 0
    for name, size in IN_SEGMENTS:
        out[name] = h[..., off:off + size]
        off += size
    return out


def masked_softmax(s, valid, sink=None):
    s = jnp.where(valid, s, NEG_INF)
    m = jnp.max(s, axis=-1, keepdims=True)
    if sink is not None:
        m = jnp.maximum(m, sink)
    p = jnp.where(valid, jnp.exp(s - m), 0.0)
    denom = jnp.sum(p, axis=-1, keepdims=True)
    if sink is not None:
        denom = denom + jnp.exp(sink - m)
    denom = jnp.where(denom > 0, denom, 1.0)
    return p / denom


def banded_attention(q, k, v, window, sink=None):
    b, t, g, hg, dh = q.shape
    nb = t // QUERY_BLOCK
    span = window + QUERY_BLOCK
    pad = ((0, 0), (window, 0), (0, 0), (0, 0))
    kp = jnp.pad(k, pad)
    vp = jnp.pad(v, pad)
    scale = dh ** -0.5
    sink_b = None if sink is None else sink.astype(F32)[None, :, :, None, None]

    def block(i):
        start = i * QUERY_BLOCK
        qb = lax.dynamic_slice_in_dim(q, start, QUERY_BLOCK, axis=1)
        kb = lax.dynamic_slice_in_dim(kp, start, span, axis=1)
        vb = lax.dynamic_slice_in_dim(vp, start, span, axis=1)
        s = jnp.einsum('bqghd,bkgd->bghqk', qb, kb, preferred_element_type=F32) * scale
        tpos = start + jnp.arange(QUERY_BLOCK)
        kpos = start - window + jnp.arange(span)
        rel = tpos[:, None] - kpos[None, :]
        valid = (rel >= 0) & (rel < window) & (kpos[None, :] >= 0)
        p = masked_softmax(s, valid, sink_b)
        return jnp.einsum('bghqk,bkgd->bqghd', p, vb.astype(F32)).astype(q.dtype)

    out = lax.map(block, jnp.arange(nb))
    return jnp.moveaxis(out, 0, 1).reshape(b, t, g, hg, dh)


def nsa_attention(q, kv_cmp, kv_slc, kv_win, gate_logits, pe_k, w1_k, w2_k, pe_v, w1_v, w2_v):
    b, t, _ = q.shape
    g, hg, dh = NSA_KV_GROUPS, NSA_HEADS // NSA_KV_GROUPS, HEAD_DIM
    q = q.reshape(b, t, g, hg, dh)
    scale = dh ** -0.5

    def kv_split(kv):
        kv = kv.reshape(b, t, 2, g, dh)
        return kv[:, :, 0], kv[:, :, 1]

    k_c, v_c = kv_split(kv_cmp)
    k_s, v_s = kv_split(kv_slc)
    k_w, v_w = kv_split(kv_win)

    n_cmp = t // CMP_STRIDE - 1

    def compress(z, pe, w1, w2):
        c = z.reshape(b, t // CMP_STRIDE, CMP_STRIDE, g, dh)
        blocks = jnp.concatenate([c[:, :-1], c[:, 1:]], axis=2) + pe[None, None, :, None, :]
        flat = jnp.moveaxis(blocks, 3, 2).reshape(b, n_cmp, g, CMP_BLOCK * dh)
        return jax.nn.silu(flat @ w1) @ w2

    kc = compress(k_c, pe_k, w1_k, w2_k)
    vc = compress(v_c, pe_v, w1_v, w2_v)
    s = jnp.einsum('btghd,bngd->bghtn', q, kc, preferred_element_type=F32) * scale
    tpos = jnp.arange(t)
    cmp_end = jnp.arange(n_cmp) * CMP_STRIDE + CMP_BLOCK - 1
    valid_c = cmp_end[None, :] <= tpos[:, None]
    p_cmp = masked_softmax(s, valid_c)
    o_cmp = jnp.einsum('bghtn,bngd->btghd', p_cmp, vc.astype(F32)).astype(q.dtype)

    n_slc = t // SLC_BLOCK
    n_sel = min(SLC_TOPN, n_slc)
    ci = jnp.arange(n_cmp)[:, None] * CMP_STRIDE
    sj = jnp.arange(n_slc)[None, :] * SLC_BLOCK
    overlap = ((ci < sj + SLC_BLOCK) & (ci + CMP_BLOCK > sj)).astype(F32)
    imp = jnp.einsum('bghtn,nj->bgtj', p_cmp, overlap)
    blk = jnp.arange(n_slc)[None, :]
    cur = (tpos // SLC_BLOCK)[:, None]
    forced = (blk == 0) | (blk == cur) | (blk == cur - 1)
    score = jnp.where(forced, FORCE, jnp.where(blk <= cur, imp, -FORCE))
    _, idx = lax.top_k(score, n_sel)

    ks = jnp.moveaxis(k_s.reshape(b, n_slc, SLC_BLOCK, g, dh), 3, 1)
    vs = jnp.moveaxis(v_s.reshape(b, n_slc, SLC_BLOCK, g, dh), 3, 1)
    nqb = t // SLC_QUERY_BLOCK
    q_blocks = jnp.moveaxis(q.reshape(b, nqb, SLC_QUERY_BLOCK, g, hg, dh), 1, 0)
    idx_blocks = jnp.moveaxis(idx.reshape(b, g, nqb, SLC_QUERY_BLOCK, n_sel), 2, 0)
    starts = jnp.arange(nqb) * SLC_QUERY_BLOCK
    bi = jnp.arange(b)[:, None, None, None]
    gi = jnp.arange(g)[None, :, None, None]

    def sel_block(args):
        qb, ib, start = args
        kg = ks[bi, gi, ib]
        vg = vs[bi, gi, ib]
        sc = jnp.einsum('bqghd,bgqnkd->bghqnk', qb, kg, preferred_element_type=F32) * scale
        kpos = ib[..., None] * SLC_BLOCK + jnp.arange(SLC_BLOCK)
        qpos = start + jnp.arange(SLC_QUERY_BLOCK)
        valid = (kpos <= qpos[None, None, :, None, None])[:, :, None]
        sc = sc.reshape(b, g, hg, SLC_QUERY_BLOCK, n_sel * SLC_BLOCK)
        valid = valid.reshape(b, g, 1, SLC_QUERY_BLOCK, n_sel * SLC_BLOCK)
        p = masked_softmax(sc, valid).reshape(b, g, hg, SLC_QUERY_BLOCK, n_sel, SLC_BLOCK)
        return jnp.einsum('bghqnk,bgqnkd->bqghd', p, vg.astype(F32)).astype(qb.dtype)

    o_slc = lax.map(sel_block, (q_blocks, idx_blocks, starts))
    o_slc = jnp.moveaxis(o_slc, 0, 1).reshape(b, t, g, hg, dh)

    o_win = banded_attention(q, k_w, v_w, NSA_WINDOW)

    gates = jax.nn.sigmoid(gate_logits.reshape(b, t, 3, g, hg))[..., None]
    o = gates[:, :, 0] * o_cmp + gates[:, :, 1] * o_slc + gates[:, :, 2] * o_win
    return o.reshape(b, t, NSA_WIDTH)


def swa_sink_attention(q, kv, sinks):
    b, t, _ = q.shape
    g, hg = SWA_KV_HEADS, SWA_HEADS // SWA_KV_HEADS
    q = q.reshape(b, t, g, hg, HEAD_DIM)
    kv = kv.reshape(b, t, 2, g, HEAD_DIM)
    o = banded_attention(q, kv[:, :, 0], kv[:, :, 1], SWA_WINDOW, sinks.reshape(g, hg))
    return o.reshape(b, t, SWA_WIDTH)


def token_shift(z, mu):
    prev = jnp.pad(z, ((0, 0), (1, 0), (0, 0)))[:, :-1]
    return z + (prev - z) * mu


def rwkv7_step(state, inp):
    r, w, k, v, a, bb = inp
    sa = jnp.einsum('bhvk,bhk->bhv', state, a)
    state = state * w[:, :, None, :] + sa[..., None] * bb[:, :, None, :] + v[..., None] * k[:, :, None, :]
    return state, jnp.einsum('bhvk,bhk->bhv', state, r)


def rwkv7_time_mix(feat, v_first, mu, w0, w2, a0, a2, k_k, k_a, r_k, ln_w, ln_b, v_res):
    b, t, _ = feat.shape
    h, n, c = RWKV_HEADS, HEAD_DIM, RWKV_WIDTH
    xs = token_shift(feat, mu)
    r = xs[..., :c]
    k = xs[..., c:2 * c]
    v = xs[..., 2 * c:3 * c]
    wd = xs[..., 3 * c:3 * c + DECAY_LORA]
    ad = xs[..., 3 * c + DECAY_LORA:]
    w = -jax.nn.softplus(-(w0 + jnp.tanh(wd) @ w2)) - 0.5
    decay = jnp.exp(-jnp.exp(w.astype(F32)))
    if v_res is None:
        v_first = v
    else:
        v0, v1, v2 = v_res
        v = v + (v_first - v) * jax.nn.sigmoid(v0 + (v @ v1) @ v2)
    a = jax.nn.sigmoid(a0 + ad @ a2)
    kk = (k * k_k).reshape(b, t, h, n).astype(F32)
    kk = kk / jnp.maximum(jnp.sqrt(jnp.sum(kk * kk, axis=-1, keepdims=True)), 1e-12)
    k = k * (1.0 + (a - 1.0) * k_a)

    def heads(z):
        return z.reshape(b, t, h, n).astype(F32)

    r_h, k_h, v_h, a_h, w_h = heads(r), heads(k), heads(v), heads(a), heads(decay)
    seq_in = tuple(jnp.moveaxis(z, 1, 0) for z in (r_h, w_h, k_h, v_h, -kk, kk * a_h))
    state0 = jnp.zeros((b, h, n, n), F32)
    _, ys = lax.scan(rwkv7_step, state0, seq_in)
    y = jnp.moveaxis(ys, 0, 1)
    mean = jnp.mean(y, axis=-1, keepdims=True)
    var = jnp.mean(jnp.square(y - mean), axis=-1, keepdims=True)
    y = (y - mean) * lax.rsqrt(var + GN_EPS) * ln_w.astype(F32).reshape(h, n) + ln_b.astype(F32).reshape(h, n)
    y = y + jnp.sum(r_h * k_h * r_k.astype(F32), axis=-1, keepdims=True) * v_h
    return y.reshape(b, t, c).astype(feat.dtype), v_first


def setup_inputs(seed: int = 0) -> dict:
    key = jax.random.key(seed)
    keys = iter(jax.random.split(key, 40))
    L = DEPTH
    C = RWKV_WIDTH

    def nrm(shape, scale):
        return jax.random.normal(next(keys), shape, F32) * scale

    return {
        "x": nrm((BATCH, SEQ, D_MODEL), 1.0),
        "norm_g": 1.0 + nrm((L, D_MODEL), 0.02),
        "w_in": nrm((L, D_MODEL, N_IN), D_MODEL ** -0.5),
        "b_merge": nrm((L, N_BRANCHES, D_MODEL), 0.02),
        "cmp_pe_k": nrm((L, CMP_BLOCK, HEAD_DIM), 0.1),
        "cmp_w1_k": nrm((L, CMP_BLOCK * HEAD_DIM, CMP_HIDDEN), (CMP_BLOCK * HEAD_DIM) ** -0.5),
        "cmp_w2_k": nrm((L, CMP_HIDDEN, HEAD_DIM), CMP_HIDDEN ** -0.5),
        "cmp_pe_v": nrm((L, CMP_BLOCK, HEAD_DIM), 0.1),
        "cmp_w1_v": nrm((L, CMP_BLOCK * HEAD_DIM, CMP_HIDDEN), (CMP_BLOCK * HEAD_DIM) ** -0.5),
        "cmp_w2_v": nrm((L, CMP_HIDDEN, HEAD_DIM), CMP_HIDDEN ** -0.5),
        "swa_sinks": nrm((L, SWA_HEADS), 0.5),
        "rwkv_mu": jax.random.uniform(next(keys), (L, RWKV_SHIFT_WIDTH), F32),
        "rwkv_w0": jax.random.uniform(next(keys), (L, C), F32, -6.0, -1.0),
        "rwkv_w2": nrm((L, DECAY_LORA, C), 0.5 * DECAY_LORA ** -0.5),
        "rwkv_a0": nrm((L, C), 0.5),
        "rwkv_a2": nrm((L, ICLR_LORA, C), 0.5 * ICLR_LORA ** -0.5),
        "rwkv_k_k": 0.85 + nrm((L, C), 0.02),
        "rwkv_k_a": 1.0 + nrm((L, C), 0.02),
        "rwkv_r_k": nrm((L, RWKV_HEADS, HEAD_DIM), 0.1),
        "rwkv_ln_w": 1.0 + nrm((L, C), 0.02),
        "rwkv_ln_b": nrm((L, C), 0.02),
        "rwkv_v0": nrm((L - 1, C), 0.5),
        "rwkv_v1": nrm((L - 1, C, VRES_LORA), C ** -0.5),
        "rwkv_v2": nrm((L - 1, VRES_LORA, C), 0.5 * VRES_LORA ** -0.5),
        "proj_a": nrm((L, NSA_WIDTH, D_MODEL), NSA_WIDTH ** -0.5),
        "proj_b": nrm((L, SWA_WIDTH, D_MODEL), SWA_WIDTH ** -0.5),
        "proj_c": nrm((L, RWKV_WIDTH, D_MODEL), RWKV_WIDTH ** -0.5),
        "w_out": nrm((L, D_MODEL, D_MODEL), D_MODEL ** -0.5),
        "final_g": 1.0 + nrm((D_MODEL,), 0.02),
    }


def reference(x, norm_g, w_in, b_merge, cmp_pe_k, cmp_w1_k, cmp_w2_k, cmp_pe_v, cmp_w1_v, cmp_w2_v,
              swa_sinks, rwkv_mu, rwkv_w0, rwkv_w2, rwkv_a0, rwkv_a2, rwkv_k_k, rwkv_k_a, rwkv_r_k,
              rwkv_ln_w, rwkv_ln_b, rwkv_v0, rwkv_v1, rwkv_v2, proj_a, proj_b, proj_c, w_out, final_g):
    b, t, d = x.shape
    v_first = None
    for l in range(DEPTH):
        xn = rms_norm(x, norm_g[l])
        cols = split_columns(xn @ w_in[l])
        y_a = nsa_attention(cols["a_q"], cols["a_kv_cmp"], cols["a_kv_slc"], cols["a_kv_win"], cols["a_gate"],
                            cmp_pe_k[l], cmp_w1_k[l], cmp_w2_k[l], cmp_pe_v[l], cmp_w1_v[l], cmp_w2_v[l])
        y_a = y_a * jax.nn.silu(cols["a_z"])
        y_b = swa_sink_attention(cols["b_q"], cols["b_kv"], swa_sinks[l]) * jax.nn.silu(cols["b_z"])
        v_res = None if l == 0 else (rwkv_v0[l - 1], rwkv_v1[l - 1], rwkv_v2[l - 1])
        y_c, v_first = rwkv7_time_mix(cols["c_shift"], v_first, rwkv_mu[l], rwkv_w0[l], rwkv_w2[l],
                                      rwkv_a0[l], rwkv_a2[l], rwkv_k_k[l], rwkv_k_a[l], rwkv_r_k[l],
                                      rwkv_ln_w[l], rwkv_ln_b[l], v_res)
        y_c = y_c * jax.nn.silu(cols["c_z"])
        gates = jax.nn.sigmoid(cols["merge"].reshape(b, t, N_BRANCHES, d) + b_merge[l])
        mixed = (gates[:, :, 0] * (y_a @ proj_a[l])
                 + gates[:, :, 1] * (y_b @ proj_b[l])
                 + gates[:, :, 2] * (y_c @ proj_c[l]))
        x = x + mixed @ w_out[l]
    return rms_norm(x, final_g)
```

```python
import functools

import numpy as np
import jax
import jax.numpy as jnp
from jax import lax
from jax.experimental import pallas as pl
from jax.experimental.pallas import tpu as pltpu

F32 = jnp.float32
BF16 = jnp.bfloat16
HIGHEST = lax.Precision.HIGHEST

D_MODEL = 1024
HEAD_DIM = 64
NSA_HEADS = 8
NSA_GROUPS = 2
NSA_HG = NSA_HEADS // NSA_GROUPS
CMP_STRIDE = 16
CMP_BLOCK = 32
CMP_HIDDEN = 128
SLC_BLOCK = 64
SLC_TOPN = 16
NSA_WINDOW = 512
SWA_HEADS = 4
SWA_GROUPS = 2
SWA_HG = SWA_HEADS // SWA_GROUPS
SWA_WINDOW = 128
RWKV_HEADS = 4
RWKV_WIDTH = RWKV_HEADS * HEAD_DIM
DECAY_LORA = 64
ICLR_LORA = 64
VRES_LORA = 32
SHIFT_WIDTH = 3 * RWKV_WIDTH + DECAY_LORA + ICLR_LORA
NORM_EPS = 1e-6
GN_EPS = 64e-5
NEG_INF = -1e30
FORCE = 1e9

_REF_SEGMENTS = (
    ("a_q", 512), ("a_kv_cmp", 256), ("a_kv_slc", 256), ("a_kv_win", 256), ("a_gate", 24),
    ("a_z", 512), ("b_q", 256), ("b_kv", 256), ("b_z", 256), ("c_shift", SHIFT_WIDTH),
    ("c_z", 256), ("merge", 3 * D_MODEL),
)
N_IN = sum(w for _, w in _REF_SEGMENTS)

VMEM_LIMIT = 56 * 1024 * 1024
ROW_TILE = 256
ATT_TILE = 256
RWKV_CHUNK = 64
RWKV_SUB = 16


def _ref_offsets():
    out, off = {}, 0
    for name, width in _REF_SEGMENTS:
        out[name] = off
        off += width
    return out


def _kv_interleave(base):
    idx = []
    for g in range(2):
        idx += list(range(base + g * 64, base + (g + 1) * 64))
        idx += list(range(base + 128 + g * 64, base + 128 + (g + 1) * 64))
    return idx


def _projection_layout():
    o = _ref_offsets()
    segs, idx = [], []

    def add(name, cols, dtype):
        segs.append((name, len(cols), dtype))
        idx.extend(cols)

    add("a_q", list(range(o["a_q"], o["a_q"] + 512)), BF16)
    add("a_z", list(range(o["a_z"], o["a_z"] + 512)), F32)
    add("a_kv_cmp", list(range(o["a_kv_cmp"], o["a_kv_cmp"] + 256)), BF16)
    add("a_kv_slc", _kv_interleave(o["a_kv_slc"]), BF16)
    add("a_kv_win", _kv_interleave(o["a_kv_win"]), BF16)
    gate = []
    for g in range(NSA_GROUPS):
        cols = [N_IN] * 128
        for br in range(3):
            for h in range(NSA_HG):
                cols[br * NSA_HG + h] = o["a_gate"] + br * NSA_HEADS + g * NSA_HG + h
        gate += cols
    add("a_gate", gate, F32)
    add("b_q", list(range(o["b_q"], o["b_q"] + 256)), BF16)
    add("b_kv", _kv_interleave(o["b_kv"]), BF16)
    add("b_z", list(range(o["b_z"], o["b_z"] + 256)), F32)
    add("c_shift", list(range(o["c_shift"], o["c_shift"] + SHIFT_WIDTH)), F32)
    add("c_z", list(range(o["c_z"], o["c_z"] + 256)), F32)
    add("merge", list(range(o["merge"], o["merge"] + 3 * D_MODEL)), F32)
    return tuple(segs), np.asarray(idx, np.int32)


_SEGS, _COL_IDX = _projection_layout()
_N_PROJ = int(_COL_IDX.shape[0])


def _params(sem):
    return pltpu.CompilerParams(dimension_semantics=sem, vmem_limit_bytes=VMEM_LIMIT)


def _in_proj_kernel(x_ref, g_ref, w_ref, *out_refs):
    x = x_ref[...]
    ms = jnp.mean(x * x, axis=-1, keepdims=True)
    xn = (x * lax.rsqrt(ms + NORM_EPS) * g_ref[...]).astype(BF16)
    off = 0
    for o_ref, (_, width, _) in zip(out_refs, _SEGS):
        for c0 in range(0, width, 512):
            cw = min(512, width - c0)
            o_ref[:, c0:c0 + cw] = jnp.dot(
                xn, w_ref[:, off + c0:off + c0 + cw], preferred_element_type=F32).astype(o_ref.dtype)
        off += width


def _in_proj(x2d, g, w):
    m = x2d.shape[0]
    tm = ROW_TILE
    out_shape = [jax.ShapeDtypeStruct((m, width), dt) for _, width, dt in _SEGS]
    out_specs = [pl.BlockSpec((tm, width), lambda i: (i, 0)) for _, width, _ in _SEGS]
    return pl.pallas_call(
        _in_proj_kernel,
        grid=(m // tm,),
        in_specs=[
            pl.BlockSpec((tm, D_MODEL), lambda i: (i, 0)),
            pl.BlockSpec((1, D_MODEL), lambda i: (0, 0)),
            pl.BlockSpec((D_MODEL, _N_PROJ), lambda i: (0, 0), pipeline_mode=pl.Buffered(1)),
        ],
        out_specs=out_specs,
        out_shape=out_shape,
        compiler_params=_params(("parallel",)),
    )(x2d, g, w)


def _compress_kernel(z_ref, pe_ref, w1_ref, w2_ref, o_ref):
    z = z_ref[...].astype(F32)
    half = CMP_STRIDE * HEAD_DIM
    w1 = w1_ref[...]
    first = jnp.dot(z, w1[:half], precision=HIGHEST, preferred_element_type=F32)
    second = jnp.dot(z, w1[half:], precision=HIGHEST, preferred_element_type=F32)
    n = z.shape[0]
    hid = first + pltpu.roll(second, n - 1, 0)
    hid = hid + jnp.dot(pe_ref[...], w1, precision=HIGHEST, preferred_element_type=F32)
    act = hid * jax.nn.sigmoid(hid)
    o_ref[...] = jnp.dot(act, w2_ref[...], precision=HIGHEST, preferred_element_type=F32)


def _compress(kv_cmp, pe, w1, w2):
    b, t, _ = kv_cmp.shape
    nch = t // CMP_STRIDE
    z = kv_cmp.reshape(b, nch, CMP_STRIDE, 4, HEAD_DIM)
    z = jnp.transpose(z, (0, 3, 1, 2, 4)).reshape(b, 4, nch, CMP_STRIDE * HEAD_DIM)
    return pl.pallas_call(
        _compress_kernel,
        grid=(b, 4),
        in_specs=[
            pl.BlockSpec((None, None, nch, CMP_STRIDE * HEAD_DIM), lambda i, j: (i, j, 0, 0)),
            pl.BlockSpec((None, 1, CMP_BLOCK * HEAD_DIM), lambda i, j: (j // 2, 0, 0)),
            pl.BlockSpec((None, CMP_BLOCK * HEAD_DIM, CMP_HIDDEN), lambda i, j: (j // 2, 0, 0)),
            pl.BlockSpec((None, CMP_HIDDEN, HEAD_DIM), lambda i, j: (j // 2, 0, 0)),
        ],
        out_specs=pl.BlockSpec((None, None, nch, HEAD_DIM), lambda i, j: (i, j, 0, 0)),
        out_shape=jax.ShapeDtypeStruct((b, 4, nch, HEAD_DIM), F32),
        compiler_params=_params(("parallel", "parallel")),
    )(z, pe, w1, w2)


def _stack_heads(q, n_heads):
    return jnp.concatenate([q[:, h * HEAD_DIM:(h + 1) * HEAD_DIM] for h in range(n_heads)], axis=0)


def _flash_loop(q_st, n_heads, kv_ref, lo, hi, mask_fn, m_sc, l_sc, acc_sc):
    tk = ATT_TILE
    m_sc[...] = jnp.full(m_sc.shape, NEG_INF, F32)
    l_sc[...] = jnp.zeros(l_sc.shape, F32)
    acc_sc[...] = jnp.zeros(acc_sc.shape, F32)

    def body(kj, carry):
        start = pl.multiple_of(kj * tk, tk)
        kv = kv_ref[pl.ds(start, tk), :]
        k = kv[:, :HEAD_DIM]
        v = kv[:, HEAD_DIM:]
        s = lax.dot_general(q_st, k, (((1,), (1,)), ((), ())), preferred_element_type=F32)
        valid = mask_fn(kj)
        valid = jnp.concatenate([valid] * n_heads, axis=0)
        s = jnp.where(valid, s, NEG_INF)
        m_prev = m_sc[...]
        m_new = jnp.maximum(m_prev, jnp.max(s, axis=-1, keepdims=True))
        p = jnp.where(valid, jnp.exp(s - m_new), 0.0)
        alpha = jnp.exp(m_prev - m_new)
        l_sc[...] = alpha * l_sc[...] + jnp.sum(p, axis=-1, keepdims=True)
        acc_sc[...] = alpha * acc_sc[...] + jnp.dot(p.astype(BF16), v, preferred_element_type=F32)
        m_sc[...] = m_new
        return carry

    lax.fori_loop(lo, hi, body, 0)


def _tile_positions(i, kj):
    tq = tk = ATT_TILE
    qpos = i * tq + lax.broadcasted_iota(jnp.int32, (tq, tk), 0)
    kpos = kj * tk + lax.broadcasted_iota(jnp.int32, (tq, tk), 1)
    return qpos, kpos


def _nsa_kernel(q_ref, z_ref, gate_ref, kvs_ref, kvw_ref, kc_ref, vc_ref, expand_ref, ovl_ref,
                o_ref, m_sc, l_sc, acc_sc):
    i = pl.program_id(2)
    tq = ATT_TILE
    n_cmp_pad = kc_ref.shape[0]
    n_slc = ovl_ref.shape[0]
    q = q_ref[...] * jnp.asarray(HEAD_DIM ** -0.5, BF16)

    kc = kc_ref[...]
    vc = vc_ref[...].astype(BF16)
    tpos = i * tq + lax.broadcasted_iota(jnp.int32, (tq, n_cmp_pad), 0)
    cmp_end = lax.broadcasted_iota(jnp.int32, (tq, n_cmp_pad), 1) * CMP_STRIDE + (CMP_BLOCK - 1)
    valid_c = cmp_end <= tpos
    o_cmp = []
    p_sum = jnp.zeros((tq, n_cmp_pad), F32)
    for h in range(NSA_HG):
        qh = q[:, h * HEAD_DIM:(h + 1) * HEAD_DIM].astype(F32)
        s = lax.dot_general(qh, kc, (((1,), (1,)), ((), ())), precision=HIGHEST,
                            preferred_element_type=F32)
        s = jnp.where(valid_c, s, NEG_INF)
        m = jnp.max(s, axis=-1, keepdims=True)
        p = jnp.where(valid_c, jnp.exp(s - m), 0.0)
        denom = jnp.sum(p, axis=-1, keepdims=True)
        p = p / jnp.where(denom > 0, denom, 1.0)
        p_sum = p_sum + p
        o_cmp.append(jnp.dot(p.astype(BF16), vc, preferred_element_type=F32))

    imp = lax.dot_general(ovl_ref[...], p_sum, (((1,), (1,)), ((), ())), precision=HIGHEST,
                          preferred_element_type=F32)
    blk = lax.broadcasted_iota(jnp.int32, (n_slc, tq), 0)
    cur = (i * tq + lax.broadcasted_iota(jnp.int32, (n_slc, tq), 1)) // SLC_BLOCK
    forced = (blk == 0) | (blk == cur) | (blk == cur - 1)
    score = jnp.where(forced, FORCE, jnp.where(blk <= cur, imp, -FORCE))
    rank = jnp.zeros((n_slc, tq), F32)
    for r in range(n_slc):
        row = score[r:r + 1, :]
        ahead = (row > score) | ((row == score) & (r < blk))
        rank = rank + jnp.where(ahead, 1.0, 0.0)
    n_sel = min(SLC_TOPN, n_slc)
    sel_t = jnp.where((rank < n_sel) & (blk <= cur), 1.0, 0.0)
    sel_t = jnp.concatenate([sel_t, jnp.zeros((128 - n_slc, tq), F32)], axis=0)
    sel = jnp.transpose(sel_t).astype(BF16)

    q_st = _stack_heads(q, NSA_HG)

    def slc_mask(kj):
        start = pl.multiple_of(kj * ATT_TILE, ATT_TILE)
        hit = jnp.dot(sel, expand_ref[:, pl.ds(start, ATT_TILE)], preferred_element_type=F32)
        qpos, kpos = _tile_positions(i, kj)
        return (hit > 0.5) & (kpos <= qpos)

    _flash_loop(q_st, NSA_HG, kvs_ref, 0, i + 1, slc_mask, m_sc, l_sc, acc_sc)
    l = l_sc[...]
    o_slc = acc_sc[...] / jnp.where(l > 0, l, 1.0)

    def win_mask(kj):
        qpos, kpos = _tile_positions(i, kj)
        rel = qpos - kpos
        return (rel >= 0) & (rel < NSA_WINDOW)

    lo = jnp.maximum(i - NSA_WINDOW // ATT_TILE, 0)
    _flash_loop(q_st, NSA_HG, kvw_ref, lo, i + 1, win_mask, m_sc, l_sc, acc_sc)
    l = l_sc[...]
    o_win = acc_sc[...] / jnp.where(l > 0, l, 1.0)

    sg = jax.nn.sigmoid(gate_ref[...])
    heads = []
    for h in range(NSA_HG):
        rows = slice(h * tq, (h + 1) * tq)
        heads.append(sg[:, h:h + 1] * o_cmp[h]
                     + sg[:, NSA_HG + h:NSA_HG + h + 1] * o_slc[rows]
                     + sg[:, 2 * NSA_HG + h:2 * NSA_HG + h + 1] * o_win[rows])
    y = jnp.concatenate(heads, axis=-1)
    z = z_ref[...]
    o_ref[...] = (y * (z * jax.nn.sigmoid(z))).astype(o_ref.dtype)


def _nsa(q, z, gate, kv_slc, kv_win, kcvc):
    b, t, _ = q.shape
    tq = ATT_TILE
    n_slc = t // SLC_BLOCK
    nch = t // CMP_STRIDE
    expand = (np.arange(128)[:, None] == (np.arange(t)[None, :] // SLC_BLOCK)).astype(np.float32)
    ci = np.arange(nch)[None, :] * CMP_STRIDE
    sj = np.arange(n_slc)[:, None] * SLC_BLOCK
    overlap_t = ((ci < sj + SLC_BLOCK) & (ci + CMP_BLOCK > sj)).astype(np.float32)
    overlap_t[:, nch - 1] = 0.0
    rows = NSA_HG * tq
    return pl.pallas_call(
        _nsa_kernel,
        grid=(b, NSA_GROUPS, t // tq),
        in_specs=[
            pl.BlockSpec((None, tq, 256), lambda bi, g, i: (bi, i, g)),
            pl.BlockSpec((None, tq, 256), lambda bi, g, i: (bi, i, g)),
            pl.BlockSpec((None, tq, 128), lambda bi, g, i: (bi, i, g)),
            pl.BlockSpec((None, t, 128), lambda bi, g, i: (bi, 0, g)),
            pl.BlockSpec((None, t, 128), lambda bi, g, i: (bi, 0, g)),
            pl.BlockSpec((None, None, nch, HEAD_DIM), lambda bi, g, i: (bi, g, 0, 0)),
            pl.BlockSpec((None, None, nch, HEAD_DIM), lambda bi, g, i: (bi, 2 + g, 0, 0)),
            pl.BlockSpec((128, t), lambda bi, g, i: (0, 0)),
            pl.BlockSpec((n_slc, nch), lambda bi, g, i: (0, 0)),
        ],
        out_specs=pl.BlockSpec((None, tq, 256), lambda bi, g, i: (bi, i, g)),
        out_shape=jax.ShapeDtypeStruct((b, t, 512), BF16),
        scratch_shapes=[
            pltpu.VMEM((rows, 1), F32),
            pltpu.VMEM((rows, 1), F32),
            pltpu.VMEM((rows, HEAD_DIM), F32),
        ],
        compiler_params=_params(("parallel", "parallel", "parallel")),
    )(q, z, gate, kv_slc, kv_win, kcvc, kcvc, jnp.asarray(expand, BF16), jnp.asarray(overlap_t))


def _swa_kernel(sink_ref, q_ref, z_ref, kv_ref, o_ref, m_sc, l_sc, acc_sc):
    g = pl.program_id(1)
    i = pl.program_id(2)
    tq = ATT_TILE
    q = q_ref[...] * jnp.asarray(HEAD_DIM ** -0.5, BF16)
    q_st = _stack_heads(q, SWA_HG)

    def mask(kj):
        qpos, kpos = _tile_positions(i, kj)
        rel = qpos - kpos
        return (rel >= 0) & (rel < SWA_WINDOW)

    lo = jnp.maximum(i - (SWA_WINDOW + ATT_TILE - 1) // ATT_TILE, 0)
    _flash_loop(q_st, SWA_HG, kv_ref, lo, i + 1, mask, m_sc, l_sc, acc_sc)
    m = m_sc[...]
    l = l_sc[...]
    acc = acc_sc[...]
    heads = []
    for h in range(SWA_HG):
        rows = slice(h * tq, (h + 1) * tq)
        sink = sink_ref[g * SWA_HG + h]
        m_f = jnp.maximum(m[rows], sink)
        w = jnp.exp(m[rows] - m_f)
        denom = l[rows] * w + jnp.exp(sink - m_f)
        denom = jnp.where(denom > 0, denom, 1.0)
        heads.append(acc[rows] * w / denom)
    y = jnp.concatenate(heads, axis=-1)
    z = z_ref[...]
    o_ref[...] = (y * (z * jax.nn.sigmoid(z))).astype(o_ref.dtype)


def _swa(q, z, kv, sinks):
    b, t, _ = q.shape
    tq = ATT_TILE
    rows = SWA_HG * tq
    return pl.pallas_call(
        _swa_kernel,
        grid=(b, SWA_GROUPS, t // tq),
        in_specs=[
            pl.BlockSpec(memory_space=pltpu.SMEM),
            pl.BlockSpec((None, tq, 128), lambda bi, g, i: (bi, i, g)),
            pl.BlockSpec((None, tq, 128), lambda bi, g, i: (bi, i, g)),
            pl.BlockSpec((None, t, 128), lambda bi, g, i: (bi, 0, g)),
        ],
        out_specs=pl.BlockSpec((None, tq, 128), lambda bi, g, i: (bi, i, g)),
        out_shape=jax.ShapeDtypeStruct((b, t, 256), BF16),
        scratch_shapes=[
            pltpu.VMEM((rows, 1), F32),
            pltpu.VMEM((rows, 1), F32),
            pltpu.VMEM((rows, HEAD_DIM), F32),
        ],
        compiler_params=_params(("parallel", "parallel", "parallel")),
    )(sinks, q, z, kv)


def _mm(a, b):
    return jnp.dot(a, b, precision=HIGHEST, preferred_element_type=F32)


def _mm_nt(a, b):
    return lax.dot_general(a, b, (((1,), (1,)), ((), ())), precision=HIGHEST, preferred_element_type=F32)


def _mm_tn(a, b):
    return lax.dot_general(a, b, (((0,), (0,)), ((), ())), precision=HIGHEST, preferred_element_type=F32)


def _unit_lower_inverse(low, eye, same_sub):
    diag = jnp.where(same_sub, low, 0.0)
    off = low - diag
    inv = eye + diag
    power = diag
    span = 2
    while span < RWKV_SUB:
        power = _mm(power, power)
        inv = _mm(inv, eye + power)
        span *= 2
    n_sub = RWKV_CHUNK // RWKV_SUB
    cross = _mm(inv, off)
    total = eye + cross
    power = cross
    span = 2
    while span < n_sub:
        power = _mm(power, power)
        total = _mm(total, eye + power)
        span *= 2
    return _mm(total, inv)


def _rwkv_kernel(*refs, has_vres):
    if has_vres:
        (feat_ref, prev_ref, z_ref, vfirst_ref, mu_ref, w0_ref, w2_ref, a0_ref, a2_ref, kk_ref, ka_ref,
         rk_ref, lnw_ref, lnb_ref, hsum_ref, v0_ref, v1_ref, v2_ref, y_ref, state) = refs
    else:
        (feat_ref, prev_ref, z_ref, mu_ref, w0_ref, w2_ref, a0_ref, a2_ref, kk_ref, ka_ref,
         rk_ref, lnw_ref, lnb_ref, hsum_ref, y_ref, vout_ref, state) = refs
    c = pl.program_id(1)
    C = RWKV_CHUNK
    W = RWKV_WIDTH

    @pl.when(c == 0)
    def _():
        state[...] = jnp.zeros(state.shape, F32)

    feat = feat_ref[...]
    row = lax.broadcasted_iota(jnp.int32, feat.shape, 0)
    last_prev = jnp.where(c == 0, 0.0, prev_ref[7:8, :])
    prev = jnp.where(row == 0, last_prev, pltpu.roll(feat, 1, 0))
    xs = feat + (prev - feat) * mu_ref[...]
    r = xs[:, :W]
    k = xs[:, W:2 * W]
    v = xs[:, 2 * W:3 * W]
    wd = xs[:, 3 * W:3 * W + DECAY_LORA]
    ad = xs[:, 3 * W + DECAY_LORA:]

    pre = -(w0_ref[...] + _mm(jnp.tanh(wd), w2_ref[...]))
    softplus = jnp.maximum(pre, 0.0) + jnp.log(1.0 + jnp.exp(-jnp.abs(pre)))
    logw = -jnp.exp(-softplus - 0.5)
    if has_vres:
        mix = jax.nn.sigmoid(v0_ref[...] + _mm(_mm(v, v1_ref[...]), v2_ref[...]))
        v = v + (vfirst_ref[...] - v) * mix
    else:
        vout_ref[...] = v
    alpha = jax.nn.sigmoid(a0_ref[...] + _mm(ad, a2_ref[...]))
    hsum = hsum_ref[...]
    kk = k * kk_ref[...]
    kk = kk / jnp.maximum(jnp.sqrt(_mm(kk * kk, hsum)), 1e-12)
    k = k * (1.0 + (alpha - 1.0) * ka_ref[...])

    ri = lax.broadcasted_iota(jnp.int32, (C, C), 0)
    ci = lax.broadcasted_iota(jnp.int32, (C, C), 1)
    incl = ri >= ci
    strict = ri > ci
    same_sub = (ri // RWKV_SUB) == (ci // RWKV_SUB)
    eye = jnp.where(ri == ci, 1.0, 0.0)

    cum = _mm(jnp.where(incl, 1.0, 0.0), logw)
    cum_end = cum[C - 1:C, :]
    a_t = -kk * jnp.exp(cum - logw)
    r_t = r * jnp.exp(cum)
    inv_decay = jnp.exp(-cum)
    b_raw = kk * alpha
    b_t = b_raw * inv_decay
    k_t = k * inv_decay
    to_end = jnp.exp(cum_end - cum)
    b_e = b_raw * to_end
    k_e = k * to_end
    p_end = jnp.exp(cum_end)

    ys = []
    for h in range(RWKV_HEADS):
        cols = slice(h * HEAD_DIM, (h + 1) * HEAD_DIM)
        ah, rh, bh, kh, vh = a_t[:, cols], r_t[:, cols], b_t[:, cols], k_t[:, cols], v[:, cols]
        a_ab = jnp.where(strict, _mm_nt(ah, bh), 0.0)
        a_ak = jnp.where(strict, _mm_nt(ah, kh), 0.0)
        a_rb = jnp.where(incl, _mm_nt(rh, bh), 0.0)
        a_rk = jnp.where(incl, _mm_nt(rh, kh), 0.0)
        t_inv = _unit_lower_inverse(a_ab, eye, same_sub)
        w_m = _mm(t_inv, ah)
        u0 = _mm(t_inv, _mm(a_ak, vh))
        r_m = rh + _mm(a_rb, w_m)
        y0 = _mm(a_rb, u0) + _mm(a_rk, vh)
        m_m = jnp.where(ri == ci, p_end[:, cols], 0.0) + _mm_tn(b_e[:, cols], w_m)
        n_m = _mm_tn(b_e[:, cols], u0) + _mm_tn(k_e[:, cols], vh)
        s0 = state[h]
        ys.append(_mm(r_m, s0) + y0)
        state[h] = _mm(m_m, s0) + n_m
    y = jnp.concatenate(ys, axis=-1)

    inv_n = 1.0 / HEAD_DIM
    mean = _mm(y, hsum) * inv_n
    cen = y - mean
    var = _mm(cen * cen, hsum) * inv_n
    y = cen * lax.rsqrt(var + GN_EPS) * lnw_ref[...] + lnb_ref[...]
    y = y + _mm(r * k * rk_ref[...], hsum) * v
    z = z_ref[...]
    y_ref[...] = (y * (z * jax.nn.sigmoid(z))).astype(y_ref.dtype)


def _rwkv(feat, z, v_first, p):
    b, t, _ = feat.shape
    C = RWKV_CHUNK
    W = RWKV_WIDTH
    has_vres = v_first is not None
    hsum = (np.arange(W)[:, None] // HEAD_DIM == np.arange(W)[None, :] // HEAD_DIM).astype(np.float32)

    def tok(width):
        return pl.BlockSpec((None, C, width), lambda bi, c: (bi, c, 0))

    def full(arr):
        return pl.BlockSpec(arr.shape, lambda bi, c: (0,) * arr.ndim)

    args = [feat, feat, z]
    specs = [tok(SHIFT_WIDTH),
             pl.BlockSpec((None, 8, SHIFT_WIDTH), lambda bi, c: (bi, jnp.maximum(c * (C // 8) - 1, 0), 0)),
             tok(W)]
    if has_vres:
        args.append(v_first)
        specs.append(tok(W))
    names = ["mu", "w0", "w2", "a0", "a2", "k_k", "k_a", "r_k", "ln_w", "ln_b"]
    consts = [p[n] for n in names] + [jnp.asarray(hsum)]
    if has_vres:
        consts += [p["v0"], p["v1"], p["v2"]]
    args += consts
    specs += [full(a) for a in consts]
    y_shape = jax.ShapeDtypeStruct((b, t, W), BF16)
    if has_vres:
        out_shape, out_specs = y_shape, tok(W)
    else:
        out_shape = (y_shape, jax.ShapeDtypeStruct((b, t, W), F32))
        out_specs = (tok(W), tok(W))
    res = pl.pallas_call(
        functools.partial(_rwkv_kernel, has_vres=has_vres),
        grid=(b, t // C),
        in_specs=specs,
        out_specs=out_specs,
        out_shape=out_shape,
        scratch_shapes=[pltpu.VMEM((RWKV_HEADS, HEAD_DIM, HEAD_DIM), F32)],
        compiler_params=_params(("parallel", "arbitrary")),
    )(*args)
    if has_vres:
        return res, v_first
    return res[0], res[1]


def _merge_kernel(x_ref, ya_ref, yb_ref, yc_ref, mg_ref, bm_ref, pa_ref, pb_ref, pc_ref, wo_ref, fg_ref,
                  o_ref, *, final_norm):
    d = D_MODEL
    mixed = None
    for j, (y_ref, p_ref) in enumerate(((ya_ref, pa_ref), (yb_ref, pb_ref), (yc_ref, pc_ref))):
        gate = jax.nn.sigmoid(mg_ref[:, j * d:(j + 1) * d] + bm_ref[j:j + 1, :])
        term = gate * jnp.dot(y_ref[...], p_ref[...], preferred_element_type=F32)
        mixed = term if mixed is None else mixed + term
    x = x_ref[...] + jnp.dot(mixed.astype(BF16), wo_ref[...], preferred_element_type=F32)
    if final_norm:
        ms = jnp.mean(x * x, axis=-1, keepdims=True)
        x = x * lax.rsqrt(ms + NORM_EPS) * fg_ref[...]
    o_ref[...] = x


def _merge(x2d, ya, yb, yc, mg, bm, pa, pb, pc, wo, fg, final_norm):
    m = x2d.shape[0]
    tm = ROW_TILE

    def rows(width):
        return pl.BlockSpec((tm, width), lambda i: (i, 0))

    def full(arr):
        return pl.BlockSpec(arr.shape, lambda i: (0, 0))

    return pl.pallas_call(
        functools.partial(_merge_kernel, final_norm=final_norm),
        grid=(m // tm,),
        in_specs=[rows(D_MODEL), rows(512), rows(256), rows(256), rows(3 * D_MODEL),
                  full(bm), full(pa), full(pb), full(pc), full(wo), full(fg)],
        out_specs=rows(D_MODEL),
        out_shape=jax.ShapeDtypeStruct((m, D_MODEL), F32),
        compiler_params=_params(("parallel",)),
    )(x2d, ya, yb, yc, mg, bm, pa, pb, pc, wo, fg)


def kernel(x, norm_g, w_in, b_merge, cmp_pe_k, cmp_w1_k, cmp_w2_k, cmp_pe_v, cmp_w1_v, cmp_w2_v, swa_sinks,
           rwkv_mu, rwkv_w0, rwkv_w2, rwkv_a0, rwkv_a2, rwkv_k_k, rwkv_k_a, rwkv_r_k, rwkv_ln_w, rwkv_ln_b,
           rwkv_v0, rwkv_v1, rwkv_v2, proj_a, proj_b, proj_c, w_out, final_g):
    b, t, d = x.shape
    depth = w_in.shape[0]
    m = b * t
    x2d = x.reshape(m, d)
    col_idx = jnp.asarray(_COL_IDX)
    v_first = None
    for l in range(depth):
        w_pad = jnp.concatenate([w_in[l], jnp.zeros((d, 1), w_in.dtype)], axis=1)
        w = jnp.take(w_pad, col_idx, axis=1).astype(BF16)
        outs = _in_proj(x2d, norm_g[l].reshape(1, d), w)
        seg = {name: o.reshape(b, t, o.shape[-1]) for (name, _, _), o in zip(_SEGS, outs)}

        pe = jnp.stack([cmp_pe_k[l], cmp_pe_v[l]]).reshape(2, 1, CMP_BLOCK * HEAD_DIM)
        kcvc = _compress(seg["a_kv_cmp"], pe, jnp.stack([cmp_w1_k[l], cmp_w1_v[l]]),
                         jnp.stack([cmp_w2_k[l], cmp_w2_v[l]]))
        y_a = _nsa(seg["a_q"], seg["a_z"], seg["a_gate"], seg["a_kv_slc"], seg["a_kv_win"], kcvc)
        y_b = _swa(seg["b_q"], seg["b_z"], seg["b_kv"], swa_sinks[l])

        row = lambda a: a.reshape(1, -1)
        p = {"mu": row(rwkv_mu[l]), "w0": row(rwkv_w0[l]), "w2": rwkv_w2[l], "a0": row(rwkv_a0[l]),
             "a2": rwkv_a2[l], "k_k": row(rwkv_k_k[l]), "k_a": row(rwkv_k_a[l]), "r_k": row(rwkv_r_k[l]),
             "ln_w": row(rwkv_ln_w[l]), "ln_b": row(rwkv_ln_b[l])}
        if l > 0:
            p.update(v0=row(rwkv_v0[l - 1]), v1=rwkv_v1[l - 1], v2=rwkv_v2[l - 1])
        y_c, v_first = _rwkv(seg["c_shift"], seg["c_z"], v_first, p)

        x2d = _merge(x2d, y_a.reshape(m, -1), y_b.reshape(m, -1), y_c.reshape(m, -1),
                     seg["merge"].reshape(m, -1), b_merge[l], proj_a[l].astype(BF16), proj_b[l].astype(BF16),
                     proj_c[l].astype(BF16), w_out[l].astype(BF16), final_g.reshape(1, d),
                     final_norm=(l == depth - 1))
    return x2d.reshape(b, t, d)
```

```python
import functools

import numpy as np
import jax
import jax.numpy as jnp
from jax import lax
from jax.experimental import pallas as pl
from jax.experimental.pallas import tpu as pltpu

F32 = jnp.float32
BF16 = jnp.bfloat16
HIGHEST = lax.Precision.HIGHEST

D_MODEL = 1024
HEAD_DIM = 64
NSA_HEADS = 8
NSA_GROUPS = 2
NSA_HG = NSA_HEADS // NSA_GROUPS
CMP_STRIDE = 16
CMP_BLOCK = 32
CMP_HIDDEN = 128
SLC_BLOCK = 64
SLC_TOPN = 16
NSA_WINDOW = 512
SWA_HEADS = 4
SWA_GROUPS = 2
SWA_HG = SWA_HEADS // SWA_GROUPS
SWA_WINDOW = 128
RWKV_HEADS = 4
RWKV_WIDTH = RWKV_HEADS * HEAD_DIM
DECAY_LORA = 64
ICLR_LORA = 64
VRES_LORA = 32
SHIFT_WIDTH = 3 * RWKV_WIDTH + DECAY_LORA + ICLR_LORA
NORM_EPS = 1e-6
GN_EPS = 64e-5
NEG_INF = -1e30
FORCE = 1e9

_REF_SEGMENTS = (
    ("a_q", 512), ("a_kv_cmp", 256), ("a_kv_slc", 256), ("a_kv_win", 256), ("a_gate", 24),
    ("a_z", 512), ("b_q", 256), ("b_kv", 256), ("b_z", 256), ("c_shift", SHIFT_WIDTH),
    ("c_z", 256), ("merge", 3 * D_MODEL),
)
N_IN = sum(w for _, w in _REF_SEGMENTS)

VMEM_LIMIT = 56 * 1024 * 1024
ROW_TILE = 256
ATT_TILE = 256
RWKV_CHUNK = 64
RWKV_STEP_CHUNKS = 2
RWKV_SUB = 16


def _ref_offsets():
    out, off = {}, 0
    for name, width in _REF_SEGMENTS:
        out[name] = off
        off += width
    return out


def _kv_interleave(base):
    idx = []
    for g in range(2):
        idx += list(range(base + 128 + g * 64, base + 128 + (g + 1) * 64))
        idx += list(range(base + g * 64, base + (g + 1) * 64))
    return idx


def _projection_layout():
    o = _ref_offsets()
    segs, idx = [], []

    def add(name, cols, dtype):
        segs.append((name, len(cols), dtype))
        idx.extend(cols)

    add("a_q", list(range(o["a_q"], o["a_q"] + 512)), BF16)
    add("a_z", list(range(o["a_z"], o["a_z"] + 512)), F32)
    add("a_kv_cmp", list(range(o["a_kv_cmp"], o["a_kv_cmp"] + 256)), BF16)
    add("a_kv_slc", _kv_interleave(o["a_kv_slc"]), BF16)
    add("a_kv_win", _kv_interleave(o["a_kv_win"]), BF16)
    gate = []
    for g in range(NSA_GROUPS):
        cols = [N_IN] * 128
        for br in range(3):
            for h in range(NSA_HG):
                cols[br * NSA_HG + h] = o["a_gate"] + br * NSA_HEADS + g * NSA_HG + h
        gate += cols
    add("a_gate", gate, F32)
    add("b_q", list(range(o["b_q"], o["b_q"] + 256)), BF16)
    add("b_kv", _kv_interleave(o["b_kv"]), BF16)
    add("b_z", list(range(o["b_z"], o["b_z"] + 256)), F32)
    add("c_shift", list(range(o["c_shift"], o["c_shift"] + SHIFT_WIDTH)), F32)
    add("c_z", list(range(o["c_z"], o["c_z"] + 256)), F32)
    add("merge", list(range(o["merge"], o["merge"] + 3 * D_MODEL)), F32)
    return tuple(segs), np.asarray(idx, np.int32)


_SEGS, _COL_IDX = _projection_layout()
_N_PROJ = int(_COL_IDX.shape[0])


def _params(sem):
    return pltpu.CompilerParams(dimension_semantics=sem, vmem_limit_bytes=VMEM_LIMIT)


def _in_proj_kernel(x_ref, g_ref, w_ref, *out_refs):
    x = x_ref[...]
    ms = jnp.mean(x * x, axis=-1, keepdims=True)
    xn = (x * lax.rsqrt(ms + NORM_EPS) * g_ref[...]).astype(BF16)
    off = 0
    for o_ref, (_, width, _) in zip(out_refs, _SEGS):
        for c0 in range(0, width, 512):
            cw = min(512, width - c0)
            o_ref[:, c0:c0 + cw] = jnp.dot(
                xn, w_ref[:, off + c0:off + c0 + cw], preferred_element_type=F32).astype(o_ref.dtype)
        off += width


def _in_proj(x2d, g, w):
    m = x2d.shape[0]
    tm = ROW_TILE
    out_shape = [jax.ShapeDtypeStruct((m, width), dt) for _, width, dt in _SEGS]
    out_specs = [pl.BlockSpec((tm, width), lambda i: (i, 0)) for _, width, _ in _SEGS]
    return pl.pallas_call(
        _in_proj_kernel,
        grid=(m // tm,),
        in_specs=[
            pl.BlockSpec((tm, D_MODEL), lambda i: (i, 0)),
            pl.BlockSpec((1, D_MODEL), lambda i: (0, 0)),
            pl.BlockSpec((D_MODEL, _N_PROJ), lambda i: (0, 0), pipeline_mode=pl.Buffered(1)),
        ],
        out_specs=out_specs,
        out_shape=out_shape,
        compiler_params=_params(("parallel",)),
    )(x2d, g, w)


def _compress_kernel(z_ref, pe_ref, w1_ref, w2_ref, o_ref):
    z = z_ref[...].astype(F32)
    half = CMP_STRIDE * HEAD_DIM
    w1 = w1_ref[...]
    first = jnp.dot(z, w1[:half], precision=HIGHEST, preferred_element_type=F32)
    second = jnp.dot(z, w1[half:], precision=HIGHEST, preferred_element_type=F32)
    n = z.shape[0]
    hid = first + pltpu.roll(second, n - 1, 0)
    hid = hid + jnp.dot(pe_ref[...], w1, precision=HIGHEST, preferred_element_type=F32)
    act = hid * jax.nn.sigmoid(hid)
    o_ref[...] = jnp.dot(act, w2_ref[...], precision=HIGHEST, preferred_element_type=F32)


def _compress(kv_cmp, pe, w1, w2):
    b, t, _ = kv_cmp.shape
    nch = t // CMP_STRIDE
    z = kv_cmp.reshape(b, nch, CMP_STRIDE, 4, HEAD_DIM)
    z = jnp.transpose(z, (0, 3, 1, 2, 4)).reshape(b, 4, nch, CMP_STRIDE * HEAD_DIM)
    return pl.pallas_call(
        _compress_kernel,
        grid=(b, 4),
        in_specs=[
            pl.BlockSpec((None, None, nch, CMP_STRIDE * HEAD_DIM), lambda i, j: (i, j, 0, 0)),
            pl.BlockSpec((None, 1, CMP_BLOCK * HEAD_DIM), lambda i, j: (j // 2, 0, 0)),
            pl.BlockSpec((None, CMP_BLOCK * HEAD_DIM, CMP_HIDDEN), lambda i, j: (j // 2, 0, 0)),
            pl.BlockSpec((None, CMP_HIDDEN, HEAD_DIM), lambda i, j: (j // 2, 0, 0)),
        ],
        out_specs=pl.BlockSpec((None, None, nch, HEAD_DIM), lambda i, j: (i, j, 0, 0)),
        out_shape=jax.ShapeDtypeStruct((b, 4, nch, HEAD_DIM), F32),
        compiler_params=_params(("parallel", "parallel")),
    )(z, pe, w1, w2)


def _widen_queries(q, n_heads):
    zeros = jnp.zeros((q.shape[0], HEAD_DIM), q.dtype)
    return [jnp.concatenate([zeros, q[:, h * HEAD_DIM:(h + 1) * HEAD_DIM]], axis=1) for h in range(n_heads)]


def _values_with_ones(kv):
    lane = lax.broadcasted_iota(jnp.int32, kv.shape, 1)
    return jnp.where(lane < HEAD_DIM, kv, jnp.ones_like(kv))


def _scores(qa, kv, bias):
    return lax.dot_general(qa, kv, (((1,), (1,)), ((), ())), preferred_element_type=F32) + bias


def _band_bias(q0, k0, rows, width, window):
    qpos = q0 + lax.broadcasted_iota(jnp.int32, (rows, width), 0)
    kpos = k0 + lax.broadcasted_iota(jnp.int32, (rows, width), 1)
    rel = qpos - kpos
    return jnp.where((rel >= 0) & (rel < window), 0.0, NEG_INF)


def _band_attention(qas, kv, bias, sinks=None):
    v1 = _values_with_ones(kv)
    outs = []
    for h, qa in enumerate(qas):
        s = _scores(qa, kv, bias)
        m = jnp.max(s, axis=-1, keepdims=True)
        if sinks is not None:
            m = jnp.maximum(m, sinks[h])
        p = jnp.exp(s - m)
        acc = jnp.dot(p.astype(BF16), v1, preferred_element_type=F32)
        denom = acc[:, HEAD_DIM:]
        if sinks is not None:
            denom = denom + jnp.exp(sinks[h] - m)
        outs.append(acc[:, :HEAD_DIM] / denom)
    return outs


def _online_update(qa, kv, v1, bias, m_prev, acc_prev):
    s = _scores(qa, kv, bias)
    m_next = jnp.maximum(m_prev, jnp.max(s, axis=-1, keepdims=True))
    p = jnp.exp(s - jnp.concatenate([m_next] * (s.shape[1] // 128), axis=1))
    alpha = jnp.exp(m_prev - m_next)
    return m_next, alpha * acc_prev + jnp.dot(p.astype(BF16), v1, preferred_element_type=F32)


def _nsa_kernel(q_ref, z_ref, gate_ref, kvs_ref, kvw_ref, kc_ref, vc_ref, expand_ref, ovl_ref,
                o_ref, m_sc, acc_sc):
    i = pl.program_id(2)
    tq = tk = ATT_TILE
    n_cmp_pad = kc_ref.shape[0]
    n_slc = ovl_ref.shape[0]
    q = q_ref[...] * jnp.asarray(HEAD_DIM ** -0.5, BF16)

    kc = kc_ref[...]
    vc = vc_ref[...].astype(BF16)
    tpos = i * tq + lax.broadcasted_iota(jnp.int32, (tq, n_cmp_pad), 0)
    cmp_end = lax.broadcasted_iota(jnp.int32, (tq, n_cmp_pad), 1) * CMP_STRIDE + (CMP_BLOCK - 1)
    valid_c = cmp_end <= tpos
    o_cmp = []
    p_sum = jnp.zeros((tq, n_cmp_pad), F32)
    for h in range(NSA_HG):
        qh = q[:, h * HEAD_DIM:(h + 1) * HEAD_DIM].astype(F32)
        s = lax.dot_general(qh, kc, (((1,), (1,)), ((), ())), precision=HIGHEST,
                            preferred_element_type=F32)
        s = jnp.where(valid_c, s, NEG_INF)
        m = jnp.max(s, axis=-1, keepdims=True)
        p = jnp.where(valid_c, jnp.exp(s - m), 0.0)
        denom = jnp.sum(p, axis=-1, keepdims=True)
        p = p * (1.0 / jnp.where(denom > 0, denom, 1.0))
        p_sum = p_sum + p
        o_cmp.append(jnp.dot(p.astype(BF16), vc, preferred_element_type=F32))

    imp = lax.dot_general(ovl_ref[...], p_sum, (((1,), (1,)), ((), ())), precision=HIGHEST,
                          preferred_element_type=F32)
    blk = lax.broadcasted_iota(jnp.int32, (n_slc, tq), 0)
    cur = (i * tq + lax.broadcasted_iota(jnp.int32, (n_slc, tq), 1)) // SLC_BLOCK
    forced = (blk == 0) | (blk == cur) | (blk == cur - 1)
    score = jnp.where(forced, FORCE, jnp.where(blk <= cur, imp, -FORCE))
    rank = jnp.zeros((n_slc, tq), F32)
    for r in range(n_slc):
        row = score[r:r + 1, :]
        ahead = (row > score) | ((row == score) & (r < blk))
        rank = rank + jnp.where(ahead, 1.0, 0.0)
    n_sel = min(SLC_TOPN, n_slc)
    sel_t = jnp.where((rank < n_sel) & (blk <= cur), 1.0, 0.0)
    sel_t = jnp.concatenate([sel_t, jnp.zeros((128 - n_slc, tq), F32)], axis=0)
    sel = jnp.transpose(sel_t).astype(BF16)

    qas = _widen_queries(q, NSA_HG)

    m_sc[...] = jnp.full(m_sc.shape, NEG_INF, F32)
    acc_sc[...] = jnp.zeros(acc_sc.shape, F32)

    def slc_tile(kj, carry):
        start = pl.multiple_of(kj * tk, tk)
        kv = kvs_ref[pl.ds(start, tk), :]
        v1 = _values_with_ones(kv)
        hit = jnp.dot(sel, expand_ref[:, pl.ds(start, tk)], preferred_element_type=F32)
        qpos = i * tq + lax.broadcasted_iota(jnp.int32, (tq, tk), 0)
        kpos = start + lax.broadcasted_iota(jnp.int32, (tq, tk), 1)
        bias = jnp.where((hit > 0.5) & (kpos <= qpos), 0.0, NEG_INF)
        for h in range(NSA_HG):
            m_next, acc = _online_update(qas[h], kv, v1, bias, m_sc[h], acc_sc[h])
            m_sc[h] = m_next
            acc_sc[h] = acc
        return carry

    lax.fori_loop(0, i + 1, slc_tile, 0)
    o_slc = []
    for h in range(NSA_HG):
        acc = acc_sc[h]
        o_slc.append(acc[:, :HEAD_DIM] / acc[:, HEAD_DIM:])

    width = NSA_WINDOW + tq
    k0 = pl.multiple_of(jnp.maximum(i * tq - NSA_WINDOW, 0), tq)
    o_win = _band_attention(qas, kvw_ref[pl.ds(k0, width), :],
                            _band_bias(i * tq, k0, tq, width, NSA_WINDOW))

    sg = jax.nn.sigmoid(gate_ref[...])
    heads = []
    for h in range(NSA_HG):
        heads.append(sg[:, h:h + 1] * o_cmp[h]
                     + sg[:, NSA_HG + h:NSA_HG + h + 1] * o_slc[h]
                     + sg[:, 2 * NSA_HG + h:2 * NSA_HG + h + 1] * o_win[h])
    y = jnp.concatenate(heads, axis=-1)
    z = z_ref[...]
    o_ref[...] = (y * (z * jax.nn.sigmoid(z))).astype(o_ref.dtype)


def _nsa(q, z, gate, kv_slc, kv_win, kcvc):
    b, t, _ = q.shape
    tq = ATT_TILE
    assert t % tq == 0 and t >= NSA_WINDOW + tq
    n_slc = t // SLC_BLOCK
    nch = t // CMP_STRIDE
    expand = (np.arange(128)[:, None] == (np.arange(t)[None, :] // SLC_BLOCK)).astype(np.float32)
    ci = np.arange(nch)[None, :] * CMP_STRIDE
    sj = np.arange(n_slc)[:, None] * SLC_BLOCK
    overlap_t = ((ci < sj + SLC_BLOCK) & (ci + CMP_BLOCK > sj)).astype(np.float32)
    return pl.pallas_call(
        _nsa_kernel,
        grid=(b, NSA_GROUPS, t // tq),
        in_specs=[
            pl.BlockSpec((None, tq, 256), lambda bi, g, i: (bi, i, g)),
            pl.BlockSpec((None, tq, 256), lambda bi, g, i: (bi, i, g)),
            pl.BlockSpec((None, tq, 128), lambda bi, g, i: (bi, i, g)),
            pl.BlockSpec((None, t, 128), lambda bi, g, i: (bi, 0, g)),
            pl.BlockSpec((None, t, 128), lambda bi, g, i: (bi, 0, g)),
            pl.BlockSpec((None, None, nch, HEAD_DIM), lambda bi, g, i: (bi, g, 0, 0)),
            pl.BlockSpec((None, None, nch, HEAD_DIM), lambda bi, g, i: (bi, 2 + g, 0, 0)),
            pl.BlockSpec((128, t), lambda bi, g, i: (0, 0)),
            pl.BlockSpec((n_slc, nch), lambda bi, g, i: (0, 0)),
        ],
        out_specs=pl.BlockSpec((None, tq, 256), lambda bi, g, i: (bi, i, g)),
        out_shape=jax.ShapeDtypeStruct((b, t, 512), BF16),
        scratch_shapes=[
            pltpu.VMEM((NSA_HG, tq, 128), F32),
            pltpu.VMEM((NSA_HG, tq, 128), F32),
        ],
        compiler_params=_params(("parallel", "parallel", "parallel")),
    )(q, z, gate, kv_slc, kv_win, kcvc, kcvc, jnp.asarray(expand, BF16), jnp.asarray(overlap_t))


def _swa_kernel(sink_ref, q_ref, z_ref, kv_ref, o_ref):
    g = pl.program_id(1)
    i = pl.program_id(2)
    tq = ATT_TILE
    q = q_ref[...] * jnp.asarray(HEAD_DIM ** -0.5, BF16)
    qas = _widen_queries(q, SWA_HG)
    width = SWA_WINDOW + tq
    k0 = pl.multiple_of(jnp.maximum(i * tq - SWA_WINDOW, 0), SWA_WINDOW)
    sinks = [sink_ref[g * SWA_HG + h] for h in range(SWA_HG)]
    heads = _band_attention(qas, kv_ref[pl.ds(k0, width), :],
                            _band_bias(i * tq, k0, tq, width, SWA_WINDOW), sinks)
    y = jnp.concatenate(heads, axis=-1)
    z = z_ref[...]
    o_ref[...] = (y * (z * jax.nn.sigmoid(z))).astype(o_ref.dtype)


def _swa(q, z, kv, sinks):
    b, t, _ = q.shape
    tq = ATT_TILE
    assert t % tq == 0 and t >= SWA_WINDOW + tq
    return pl.pallas_call(
        _swa_kernel,
        grid=(b, SWA_GROUPS, t // tq),
        in_specs=[
            pl.BlockSpec(memory_space=pltpu.SMEM),
            pl.BlockSpec((None, tq, 128), lambda bi, g, i: (bi, i, g)),
            pl.BlockSpec((None, tq, 128), lambda bi, g, i: (bi, i, g)),
            pl.BlockSpec((None, t, 128), lambda bi, g, i: (bi, 0, g)),
        ],
        out_specs=pl.BlockSpec((None, tq, 128), lambda bi, g, i: (bi, i, g)),
        out_shape=jax.ShapeDtypeStruct((b, t, 256), BF16),
        compiler_params=_params(("parallel", "parallel", "parallel")),
    )(sinks, q, z, kv)


_NN = (((1,), (0,)), ((), ()))
_NT = (((1,), (1,)), ((), ()))


def _split(a):
    hi = a.astype(BF16)
    lo = (a - hi.astype(F32)).astype(BF16)
    return hi, lo


def _dot(a, b, dims=_NN):
    return lax.dot_general(a, b, dims, preferred_element_type=F32)


def _dot3(a, b, dims=_NN):
    return _dot(a[0], b[0], dims) + (_dot(a[0], b[1], dims) + _dot(a[1], b[0], dims))


def _dot_exact_lhs(a, b, n_terms):
    out = None
    for _ in range(n_terms):
        term = b.astype(BF16)
        b = b - term.astype(F32)
        part = _dot(a, term)
        out = part if out is None else out + part
    return out


def _dot_exact_rhs(x, ones):
    hi, lo = _split(x)
    return _dot(hi, ones) + _dot(lo, ones)


def _block_diag(pair, head_masks):
    return tuple(jnp.concatenate([x * m for m in head_masks], axis=0) for x in pair)


def _unit_lower_inverse(low, eye, same_sub, bd):
    diag = jnp.where(same_sub, low, 0.0)
    off = low - diag
    inv = eye + diag
    power = diag
    span = 2
    while span < RWKV_SUB:
        ps = _split(power)
        power = _dot3(ps, bd(ps))
        inv = _dot3(_split(inv), bd(_split(eye + power)))
        span *= 2
    inv_s = _split(inv)
    cross = _dot3(inv_s, bd(_split(off)))
    total = eye + cross
    power = cross
    span = 2
    while span < RWKV_CHUNK // RWKV_SUB:
        ps = _split(power)
        power = _dot3(ps, bd(ps))
        total = _dot3(_split(total), bd(_split(eye + power)))
        span *= 2
    return _dot3(_split(total), bd(inv_s))


def _rwkv_chunk(a_t, r_t, b_t, k_t, b_e, k_e, v, p_end, s_bd, cst):
    C = RWKV_CHUNK
    eye, strict, incl, same_sub, head_masks, same_head, eye_bd = cst
    bd = lambda pair: _block_diag(pair, head_masks)

    ar = _split(jnp.concatenate([a_t, r_t], axis=0))
    g_b = _dot3(ar, bd(_split(b_t)), _NT)
    g_k = _dot3(ar, bd(_split(k_t)), _NT)
    a_ab = jnp.where(strict, g_b[:C], 0.0)
    a_rb = jnp.where(incl, g_b[C:], 0.0)
    a_ak = jnp.where(strict, g_k[:C], 0.0)
    a_rk = jnp.where(incl, g_k[C:], 0.0)

    t_inv = _split(_unit_lower_inverse(a_ab, eye, same_sub, bd))
    v_bd = bd(_split(v))
    akv = _dot3(_split(a_ak), v_bd)
    w_m = _dot3(t_inv, bd(_split(a_t)))
    u0 = _dot3(t_inv, bd(_split(akv)))
    a_rb = _split(a_rb)
    r_m = r_t + _dot3(a_rb, bd(_split(w_m)))
    y0 = _dot3(a_rb, bd(_split(u0))) + _dot3(_split(a_rk), v_bd)

    bk = _split(jnp.transpose(jnp.concatenate([b_e, k_e], axis=0)))
    b_only = (bk[0][:, :C], bk[1][:, :C])
    m_add = _dot3(b_only, _split(w_m))
    n_all = _dot3(bk, _split(jnp.concatenate([u0, v], axis=0)))
    m_bd = jnp.where(same_head, m_add, 0.0) + jnp.where(eye_bd, p_end, 0.0)
    n_bd = jnp.where(same_head, n_all, 0.0)

    s_pair = _split(s_bd)
    y = _dot3(_split(r_m), s_pair) + y0
    s_new = _dot3(_split(m_bd), s_pair) + n_bd
    return y, s_new


def _rwkv_kernel(*refs, has_vres):
    if has_vres:
        (feat_ref, prev_ref, z_ref, vfirst_ref, mu_ref, w0_ref, w2_ref, a0_ref, a2_ref, kk_ref, ka_ref,
         rk_ref, lnw_ref, lnb_ref, hsum_ref, v0_ref, v1_ref, v2_ref, y_ref, state) = refs
    else:
        (feat_ref, prev_ref, z_ref, mu_ref, w0_ref, w2_ref, a0_ref, a2_ref, kk_ref, ka_ref,
         rk_ref, lnw_ref, lnb_ref, hsum_ref, y_ref, vout_ref, state) = refs
    c = pl.program_id(1)
    C = RWKV_CHUNK
    W = RWKV_WIDTH
    n_tok = feat_ref.shape[0]

    @pl.when(c == 0)
    def _():
        state[...] = jnp.zeros(state.shape, F32)

    def lora(x, w_ref):
        return _dot3(_split(x), _split(w_ref[...]))

    hsum = hsum_ref[...]

    def head_sum(x):
        return _dot_exact_rhs(x, hsum)

    feat = feat_ref[...]
    row = lax.broadcasted_iota(jnp.int32, feat.shape, 0)
    last_prev = jnp.where(c == 0, 0.0, prev_ref[7:8, :])
    prev = jnp.where(row == 0, last_prev, pltpu.roll(feat, 1, 0))
    xs = feat + (prev - feat) * mu_ref[...]
    r = xs[:, :W]
    k = xs[:, W:2 * W]
    v = xs[:, 2 * W:3 * W]
    wd = xs[:, 3 * W:3 * W + DECAY_LORA]
    ad = xs[:, 3 * W + DECAY_LORA:]

    pre = -(w0_ref[...] + lora(jnp.tanh(wd), w2_ref))
    softplus = jnp.maximum(pre, 0.0) + jnp.log(1.0 + jnp.exp(-jnp.abs(pre)))
    logw = -jnp.exp(-softplus - 0.5)
    if has_vres:
        mix = jax.nn.sigmoid(v0_ref[...] + lora(lora(v, v1_ref), v2_ref))
        v = v + (vfirst_ref[...] - v) * mix
    else:
        vout_ref[...] = v
    alpha = jax.nn.sigmoid(a0_ref[...] + lora(ad, a2_ref))
    kk = k * kk_ref[...]
    kk = kk / jnp.maximum(jnp.sqrt(head_sum(kk * kk)), 1e-12)
    k = k * (1.0 + (alpha - 1.0) * ka_ref[...])

    ti = lax.broadcasted_iota(jnp.int32, (n_tok, n_tok), 0)
    tj = lax.broadcasted_iota(jnp.int32, (n_tok, n_tok), 1)
    tri = jnp.where((ti >= tj) & (ti // C == tj // C), 1.0, 0.0).astype(BF16)
    cum = _dot_exact_lhs(tri, logw, 3)

    t_idx = lax.broadcasted_iota(jnp.int32, (C, W), 0)
    i_idx = lax.broadcasted_iota(jnp.int32, (C, W), 1) % HEAD_DIM
    lane_head = lax.broadcasted_iota(jnp.int32, (C, W), 1) // HEAD_DIM
    bi = lax.broadcasted_iota(jnp.int32, (W, W), 0)
    bj = lax.broadcasted_iota(jnp.int32, (W, W), 1)
    cst = (
        jnp.where(t_idx == i_idx, 1.0, 0.0),
        t_idx > i_idx,
        t_idx >= i_idx,
        (t_idx // RWKV_SUB) == (i_idx // RWKV_SUB),
        [jnp.where(lane_head == h, 1.0, 0.0).astype(BF16) for h in range(RWKV_HEADS)],
        (bi // HEAD_DIM) == (bj // HEAD_DIM),
        bi == bj,
    )

    a_t = -kk * jnp.exp(cum - logw)
    r_t = r * jnp.exp(cum)
    inv_decay = jnp.exp(-cum)
    b_raw = kk * alpha
    b_t = b_raw * inv_decay
    k_t = k * inv_decay

    s_bd = state[...]
    ys = []
    for j in range(n_tok // C):
        rows = slice(j * C, (j + 1) * C)
        cum_end = cum[(j + 1) * C - 1:(j + 1) * C, :]
        to_end = jnp.exp(cum_end - cum[rows])
        y_j, s_bd = _rwkv_chunk(a_t[rows], r_t[rows], b_t[rows], k_t[rows], b_raw[rows] * to_end,
                                k[rows] * to_end, v[rows], jnp.exp(cum_end), s_bd, cst)
        ys.append(y_j)
    state[...] = s_bd
    y = jnp.concatenate(ys, axis=0)

    inv_n = 1.0 / HEAD_DIM
    mean = head_sum(y) * inv_n
    cen = y - mean
    var = head_sum(cen * cen) * inv_n
    y = cen * lax.rsqrt(var + GN_EPS) * lnw_ref[...] + lnb_ref[...]
    y = y + head_sum(r * k * rk_ref[...]) * v
    z = z_ref[...]
    y_ref[...] = (y * (z * jax.nn.sigmoid(z))).astype(y_ref.dtype)


def _rwkv(feat, z, v_first, p):
    b, t, _ = feat.shape
    n_tok = RWKV_STEP_CHUNKS * RWKV_CHUNK
    assert t % n_tok == 0
    W = RWKV_WIDTH
    has_vres = v_first is not None
    hsum = (np.arange(W)[:, None] // HEAD_DIM == np.arange(W)[None, :] // HEAD_DIM).astype(np.float32)

    def tok(width):
        return pl.BlockSpec((None, n_tok, width), lambda bi, c: (bi, c, 0))

    def full(arr):
        return pl.BlockSpec(arr.shape, lambda bi, c: (0,) * arr.ndim)

    args = [feat, feat, z]
    specs = [tok(SHIFT_WIDTH),
             pl.BlockSpec((None, 8, SHIFT_WIDTH), lambda bi, c: (bi, jnp.maximum(c * (n_tok // 8) - 1, 0), 0)),
             tok(W)]
    if has_vres:
        args.append(v_first)
        specs.append(tok(W))
    names = ["mu", "w0", "w2", "a0", "a2", "k_k", "k_a", "r_k", "ln_w", "ln_b"]
    consts = [p[n] for n in names] + [jnp.asarray(hsum, BF16)]
    if has_vres:
        consts += [p["v0"], p["v1"], p["v2"]]
    args += consts
    specs += [full(a) for a in consts]
    y_shape = jax.ShapeDtypeStruct((b, t, W), BF16)
    if has_vres:
        out_shape, out_specs = y_shape, tok(W)
    else:
        out_shape = (y_shape, jax.ShapeDtypeStruct((b, t, W), F32))
        out_specs = (tok(W), tok(W))
    res = pl.pallas_call(
        functools.partial(_rwkv_kernel, has_vres=has_vres),
        grid=(b, t // n_tok),
        in_specs=specs,
        out_specs=out_specs,
        out_shape=out_shape,
        scratch_shapes=[pltpu.VMEM((W, W), F32)],
        compiler_params=_params(("parallel", "arbitrary")),
    )(*args)
    if has_vres:
        return res, v_first
    return res[0], res[1]


def _merge_kernel(x_ref, ya_ref, yb_ref, yc_ref, mg_ref, bm_ref, pa_ref, pb_ref, pc_ref, wo_ref, fg_ref,
                  o_ref, *, final_norm):
    d = D_MODEL
    mixed = None
    for j, (y_ref, p_ref) in enumerate(((ya_ref, pa_ref), (yb_ref, pb_ref), (yc_ref, pc_ref))):
        gate = jax.nn.sigmoid(mg_ref[:, j * d:(j + 1) * d] + bm_ref[j:j + 1, :])
        term = gate * jnp.dot(y_ref[...], p_ref[...], preferred_element_type=F32)
        mixed = term if mixed is None else mixed + term
    x = x_ref[...] + jnp.dot(mixed.astype(BF16), wo_ref[...], preferred_element_type=F32)
    if final_norm:
        ms = jnp.mean(x * x, axis=-1, keepdims=True)
        x = x * lax.rsqrt(ms + NORM_EPS) * fg_ref[...]
    o_ref[...] = x


def _merge(x2d, ya, yb, yc, mg, bm, pa, pb, pc, wo, fg, final_norm):
    m = x2d.shape[0]
    tm = ROW_TILE

    def rows(width):
        return pl.BlockSpec((tm, width), lambda i: (i, 0))

    def full(arr):
        return pl.BlockSpec(arr.shape, lambda i: (0, 0))

    return pl.pallas_call(
        functools.partial(_merge_kernel, final_norm=final_norm),
        grid=(m // tm,),
        in_specs=[rows(D_MODEL), rows(512), rows(256), rows(256), rows(3 * D_MODEL),
                  full(bm), full(pa), full(pb), full(pc), full(wo), full(fg)],
        out_specs=rows(D_MODEL),
        out_shape=jax.ShapeDtypeStruct((m, D_MODEL), F32),
        compiler_params=_params(("parallel",)),
    )(x2d, ya, yb, yc, mg, bm, pa, pb, pc, wo, fg)


def kernel(x, norm_g, w_in, b_merge, cmp_pe_k, cmp_w1_k, cmp_w2_k, cmp_pe_v, cmp_w1_v, cmp_w2_v, swa_sinks,
           rwkv_mu, rwkv_w0, rwkv_w2, rwkv_a0, rwkv_a2, rwkv_k_k, rwkv_k_a, rwkv_r_k, rwkv_ln_w, rwkv_ln_b,
           rwkv_v0, rwkv_v1, rwkv_v2, proj_a, proj_b, proj_c, w_out, final_g):
    b, t, d = x.shape
    depth = w_in.shape[0]
    m = b * t
    x2d = x.reshape(m, d)
    col_idx = jnp.asarray(_COL_IDX)
    v_first = None
    for l in range(depth):
        w_pad = jnp.concatenate([w_in[l], jnp.zeros((d, 1), w_in.dtype)], axis=1)
        w = jnp.take(w_pad, col_idx, axis=1).astype(BF16)
        outs = _in_proj(x2d, norm_g[l].reshape(1, d), w)
        seg = {name: o.reshape(b, t, o.shape[-1]) for (name, _, _), o in zip(_SEGS, outs)}

        pe = jnp.stack([cmp_pe_k[l], cmp_pe_v[l]]).reshape(2, 1, CMP_BLOCK * HEAD_DIM)
        kcvc = _compress(seg["a_kv_cmp"], pe, jnp.stack([cmp_w1_k[l], cmp_w1_v[l]]),
                         jnp.stack([cmp_w2_k[l], cmp_w2_v[l]]))
        y_a = _nsa(seg["a_q"], seg["a_z"], seg["a_gate"], seg["a_kv_slc"], seg["a_kv_win"], kcvc)
        y_b = _swa(seg["b_q"], seg["b_z"], seg["b_kv"], swa_sinks[l])

        row = lambda a: a.reshape(1, -1)
        p = {"mu": row(rwkv_mu[l]), "w0": row(rwkv_w0[l]), "w2": rwkv_w2[l], "a0": row(rwkv_a0[l]),
             "a2": rwkv_a2[l], "k_k": row(rwkv_k_k[l]), "k_a": row(rwkv_k_a[l]), "r_k": row(rwkv_r_k[l]),
             "ln_w": row(rwkv_ln_w[l]), "ln_b": row(rwkv_ln_b[l])}
        if l > 0:
            p.update(v0=row(rwkv_v0[l - 1]), v1=rwkv_v1[l - 1], v2=rwkv_v2[l - 1])
        y_c, v_first = _rwkv(seg["c_shift"], seg["c_z"], v_first, p)

        x2d = _merge(x2d, y_a.reshape(m, -1), y_b.reshape(m, -1), y_c.reshape(m, -1),
                     seg["merge"].reshape(m, -1), b_merge[l], proj_a[l].astype(BF16), proj_b[l].astype(BF16),
                     proj_c[l].astype(BF16), w_out[l].astype(BF16), final_g.reshape(1, d),
                     final_norm=(l == depth - 1))
    return x2d.reshape(b, t, d)
```

```python
import functools

import numpy as np
import jax
import jax.numpy as jnp
from jax import lax
from jax.experimental import pallas as pl
from jax.experimental.pallas import tpu as pltpu

F32 = jnp.float32
BF16 = jnp.bfloat16
HIGHEST = lax.Precision.HIGHEST

D_MODEL = 1024
HEAD_DIM = 64
NSA_HEADS = 8
NSA_GROUPS = 2
NSA_HG = NSA_HEADS // NSA_GROUPS
CMP_STRIDE = 16
CMP_BLOCK = 32
CMP_HIDDEN = 128
SLC_BLOCK = 64
SLC_TOPN = 16
NSA_WINDOW = 512
SWA_HEADS = 4
SWA_GROUPS = 2
SWA_HG = SWA_HEADS // SWA_GROUPS
SWA_WINDOW = 128
RWKV_HEADS = 4
RWKV_WIDTH = RWKV_HEADS * HEAD_DIM
DECAY_LORA = 64
ICLR_LORA = 64
VRES_LORA = 32
SHIFT_WIDTH = 3 * RWKV_WIDTH + DECAY_LORA + ICLR_LORA
NORM_EPS = 1e-6
GN_EPS = 64e-5
NEG_INF = -1e30
FORCE = 1e9

_REF_SEGMENTS = (
    ("a_q", 512), ("a_kv_cmp", 256), ("a_kv_slc", 256), ("a_kv_win", 256), ("a_gate", 24),
    ("a_z", 512), ("b_q", 256), ("b_kv", 256), ("b_z", 256), ("c_shift", SHIFT_WIDTH),
    ("c_z", 256), ("merge", 3 * D_MODEL),
)
N_IN = sum(w for _, w in _REF_SEGMENTS)

VMEM_LIMIT = 56 * 1024 * 1024
ROW_TILE = 256
ATT_TILE = 256
ATT_ROWS = 128
RWKV_CHUNK = 64
RWKV_STEP_CHUNKS = 4
RWKV_SUB = 16


def _ref_offsets():
    out, off = {}, 0
    for name, width in _REF_SEGMENTS:
        out[name] = off
        off += width
    return out


def _kv_interleave(base):
    idx = []
    for g in range(2):
        idx += list(range(base + 128 + g * 64, base + 128 + (g + 1) * 64))
        idx += list(range(base + g * 64, base + (g + 1) * 64))
    return idx


def _projection_layout():
    o = _ref_offsets()
    segs, idx = [], []

    def add(name, cols, dtype):
        segs.append((name, len(cols), dtype))
        idx.extend(cols)

    add("a_q", list(range(o["a_q"], o["a_q"] + 512)), BF16)
    add("a_z", list(range(o["a_z"], o["a_z"] + 512)), F32)
    add("a_kv_cmp", list(range(o["a_kv_cmp"], o["a_kv_cmp"] + 256)), BF16)
    add("a_kv_slc", _kv_interleave(o["a_kv_slc"]), BF16)
    add("a_kv_win", _kv_interleave(o["a_kv_win"]), BF16)
    gate = []
    for g in range(NSA_GROUPS):
        cols = [N_IN] * 128
        for br in range(3):
            for h in range(NSA_HG):
                cols[br * NSA_HG + h] = o["a_gate"] + br * NSA_HEADS + g * NSA_HG + h
        gate += cols
    add("a_gate", gate, F32)
    add("b_q", list(range(o["b_q"], o["b_q"] + 256)), BF16)
    add("b_kv", _kv_interleave(o["b_kv"]), BF16)
    add("b_z", list(range(o["b_z"], o["b_z"] + 256)), F32)
    add("c_shift", list(range(o["c_shift"], o["c_shift"] + SHIFT_WIDTH)), F32)
    add("c_z", list(range(o["c_z"], o["c_z"] + 256)), F32)
    add("merge", list(range(o["merge"], o["merge"] + 3 * D_MODEL)), F32)
    return tuple(segs), np.asarray(idx, np.int32)


_SEGS, _COL_IDX = _projection_layout()
_N_PROJ = int(_COL_IDX.shape[0])


def _params(sem):
    return pltpu.CompilerParams(dimension_semantics=sem, vmem_limit_bytes=VMEM_LIMIT)


def _in_proj_kernel(x_ref, g_ref, w_ref, *out_refs):
    x = x_ref[...]
    ms = jnp.mean(x * x, axis=-1, keepdims=True)
    xn = (x * lax.rsqrt(ms + NORM_EPS) * g_ref[...]).astype(BF16)
    off = 0
    for o_ref, (_, width, _) in zip(out_refs, _SEGS):
        for c0 in range(0, width, 512):
            cw = min(512, width - c0)
            o_ref[:, c0:c0 + cw] = jnp.dot(
                xn, w_ref[:, off + c0:off + c0 + cw], preferred_element_type=F32).astype(o_ref.dtype)
        off += width


def _in_proj(x2d, g, w):
    m = x2d.shape[0]
    tm = ROW_TILE
    out_shape = [jax.ShapeDtypeStruct((m, width), dt) for _, width, dt in _SEGS]
    out_specs = [pl.BlockSpec((tm, width), lambda i: (i, 0)) for _, width, _ in _SEGS]
    return pl.pallas_call(
        _in_proj_kernel,
        grid=(m // tm,),
        in_specs=[
            pl.BlockSpec((tm, D_MODEL), lambda i: (i, 0)),
            pl.BlockSpec((1, D_MODEL), lambda i: (0, 0)),
            pl.BlockSpec((D_MODEL, _N_PROJ), lambda i: (0, 0), pipeline_mode=pl.Buffered(1)),
        ],
        out_specs=out_specs,
        out_shape=out_shape,
        compiler_params=_params(("parallel",)),
    )(x2d, g, w)


def _compress_kernel(z_ref, pe_ref, w1_ref, w2_ref, o_ref):
    z = z_ref[...].astype(F32)
    half = CMP_STRIDE * HEAD_DIM
    w1 = w1_ref[...]
    first = jnp.dot(z, w1[:half], precision=HIGHEST, preferred_element_type=F32)
    second = jnp.dot(z, w1[half:], precision=HIGHEST, preferred_element_type=F32)
    n = z.shape[0]
    hid = first + pltpu.roll(second, n - 1, 0)
    hid = hid + jnp.dot(pe_ref[...], w1, precision=HIGHEST, preferred_element_type=F32)
    act = hid * jax.nn.sigmoid(hid)
    o_ref[...] = jnp.dot(act, w2_ref[...], precision=HIGHEST, preferred_element_type=F32)


def _compress(kv_cmp, pe, w1, w2):
    b, t, _ = kv_cmp.shape
    nch = t // CMP_STRIDE
    z = kv_cmp.reshape(b, nch, CMP_STRIDE, 4, HEAD_DIM)
    z = jnp.transpose(z, (0, 3, 1, 2, 4)).reshape(b, 4, nch, CMP_STRIDE * HEAD_DIM)
    return pl.pallas_call(
        _compress_kernel,
        grid=(b, 4),
        in_specs=[
            pl.BlockSpec((None, None, nch, CMP_STRIDE * HEAD_DIM), lambda i, j: (i, j, 0, 0)),
            pl.BlockSpec((None, 1, CMP_BLOCK * HEAD_DIM), lambda i, j: (j // 2, 0, 0)),
            pl.BlockSpec((None, CMP_BLOCK * HEAD_DIM, CMP_HIDDEN), lambda i, j: (j // 2, 0, 0)),
            pl.BlockSpec((None, CMP_HIDDEN, HEAD_DIM), lambda i, j: (j // 2, 0, 0)),
        ],
        out_specs=pl.BlockSpec((None, None, nch, HEAD_DIM), lambda i, j: (i, j, 0, 0)),
        out_shape=jax.ShapeDtypeStruct((b, 4, nch, HEAD_DIM), F32),
        compiler_params=_params(("parallel", "parallel")),
    )(z, pe, w1, w2)


def _widen_queries(q, n_heads):
    zeros = jnp.zeros((q.shape[0], HEAD_DIM), q.dtype)
    return [jnp.concatenate([zeros, q[:, h * HEAD_DIM:(h + 1) * HEAD_DIM]], axis=1) for h in range(n_heads)]


def _pair_values(kv):
    swapped = jnp.concatenate([kv[:, HEAD_DIM:], kv[:, :HEAD_DIM]], axis=1)
    lane = lax.broadcasted_iota(jnp.int32, kv.shape, 1)
    low = jnp.where(lane < HEAD_DIM, 1.0, 0.0).astype(kv.dtype)
    high = jnp.where(lane < HEAD_DIM, 0.0, 1.0).astype(kv.dtype)
    even = jnp.concatenate([kv * low, low], axis=1)
    odd = jnp.concatenate([swapped * high, high], axis=1)
    return even, odd


def _pair_lanes(even, odd):
    low = lax.broadcasted_iota(jnp.int32, even.shape, 1) < HEAD_DIM
    return jnp.where(low, even, odd)


def _nt(a, b):
    return lax.dot_general(a, b, (((1,), (1,)), ((), ())), preferred_element_type=F32)


def _band_pairs(blocks, keys, val_even, val_odd, sinks=None):
    s = [[_nt(qa, keys) + bias for qa in (qa_even, qa_odd)] for qa_even, qa_odd, bias in blocks]
    m = [[jnp.max(x, axis=-1, keepdims=True) for x in pair] for pair in s]
    if sinks is not None:
        m = [[jnp.maximum(x, sinks[e]) for e, x in enumerate(pair)] for pair in m]
    p = [[jnp.exp(x - mx).astype(BF16) for x, mx in zip(xs, ms)] for xs, ms in zip(s, m)]
    outs = []
    for ps, ms in zip(p, m):
        acc = (jnp.dot(ps[0], val_even, preferred_element_type=F32)
               + jnp.dot(ps[1], val_odd, preferred_element_type=F32))
        denom = acc[:, 2 * HEAD_DIM:]
        if sinks is not None:
            shape = (acc.shape[0], 2 * HEAD_DIM)
            denom = denom + _pair_lanes(jnp.broadcast_to(jnp.exp(sinks[0] - ms[0]), shape),
                                        jnp.broadcast_to(jnp.exp(sinks[1] - ms[1]), shape))
        outs.append(acc[:, :2 * HEAD_DIM] / denom)
    return outs


def _band_bias_table(deltas, rows, width, window):
    r = np.arange(rows)[:, None]
    c = np.arange(width)[None, :]
    out = [np.where((c <= r + d) & (c > r + d - window), 0.0, NEG_INF) for d in deltas]
    return np.stack(out).astype(np.float32)


def _nsa_kernel(q_ref, z_ref, gate_ref, kvs_ref, kvw_ref, kc_ref, vc_ref, negexp_ref, ovl_ref, wbias_ref,
                cbias_ref, gexp_ref, o_ref, srhs_sc, spe_sc, spo_sc, wpe_sc, wpo_sc, m_sc, acc_sc):
    i = pl.program_id(2)
    tq = tk = ATT_TILE
    rb = ATT_ROWS
    n_cmp_pad = kc_ref.shape[0]
    n_slc = ovl_ref.shape[0]
    n_pairs = NSA_HG // 2

    @pl.when(i == 0)
    def _():
        kvs = kvs_ref[...]
        srhs_sc[...] = jnp.concatenate([kvs, negexp_ref[...]], axis=1)
        spe_sc[...], spo_sc[...] = _pair_values(kvs)
        wpe_sc[...], wpo_sc[...] = _pair_values(kvw_ref[...])

    q = q_ref[...] * jnp.asarray(HEAD_DIM ** -0.5, BF16)
    qas = _widen_queries(q, NSA_HG)

    width = NSA_WINDOW + tq
    k0 = pl.multiple_of(jnp.maximum(i * tq - NSA_WINDOW, 0), tq)
    keys = kvw_ref[pl.ds(k0, width), :]
    val_even = wpe_sc[pl.ds(k0, width), :]
    val_odd = wpo_sc[pl.ds(k0, width), :]
    variant = jnp.minimum(i, wbias_ref.shape[0] - 1)
    n_rb = tq // rb
    blocks = [(qas[2 * pair][r0:r0 + rb], qas[2 * pair + 1][r0:r0 + rb], wbias_ref[variant, r0:r0 + rb, :])
              for pair in range(n_pairs) for r0 in range(0, tq, rb)]
    outs = _band_pairs(blocks, keys, val_even, val_odd)
    o_win = jnp.concatenate([jnp.concatenate(outs[pair * n_rb:(pair + 1) * n_rb], axis=0)
                             for pair in range(n_pairs)], axis=1)

    zeros_k =jnp.zeros((n_cmp_pad, HEAD_DIM), BF16)
    kc_terms = [jnp.concatenate([zeros_k, t], axis=1) for t in _split(kc_ref[...])]
    vc = vc_ref[...].astype(BF16)
    vc_even = jnp.concatenate([vc, zeros_k], axis=1)
    vc_odd = jnp.concatenate([zeros_k, vc], axis=1)
    tpos = i * tq + lax.broadcasted_iota(jnp.int32, (tq, n_cmp_pad), 0)
    cmp_end = lax.broadcasted_iota(jnp.int32, (tq, n_cmp_pad), 1) * CMP_STRIDE + (CMP_BLOCK - 1)
    valid_c = cmp_end <= tpos
    o_cmp = []
    p_sum = jnp.zeros((tq, n_cmp_pad), F32)
    for pair in range(n_pairs):
        ps = []
        for e in range(2):
            qa = qas[2 * pair + e]
            s = _nt(qa, kc_terms[0]) + _nt(qa, kc_terms[1])
            s = jnp.where(valid_c, s, NEG_INF)
            m = jnp.max(s, axis=-1, keepdims=True)
            p = jnp.where(valid_c, jnp.exp(s - m), 0.0)
            denom = jnp.sum(p, axis=-1, keepdims=True)
            p = p * (1.0 / jnp.where(denom > 0, denom, 1.0))
            p_sum = p_sum + p
            ps.append(p.astype(BF16))
        o_cmp.append(jnp.dot(ps[0], vc_even, preferred_element_type=F32)
                     + jnp.dot(ps[1], vc_odd, preferred_element_type=F32))
    o_cmp = jnp.concatenate(o_cmp, axis=1)

    imp = None
    rest = p_sum
    for _ in range(3):
        term = rest.astype(BF16)
        rest = rest - term.astype(F32)
        part = _nt(ovl_ref[...], term)
        imp = part if imp is None else imp + part
    blk = lax.broadcasted_iota(jnp.int32, (n_slc, tq), 0)
    cur = (i * tq + lax.broadcasted_iota(jnp.int32, (n_slc, tq), 1)) // SLC_BLOCK
    forced = (blk == 0) | (blk == cur) | (blk == cur - 1)
    score = jnp.where(forced, FORCE, jnp.where(blk <= cur, imp, -FORCE))
    rank = jnp.zeros((n_slc, tq), F32)
    for r in range(n_slc):
        row = score[r:r + 1, :]
        ahead = (row > score) | ((row == score) & (r < blk))
        rank = rank + jnp.where(ahead, 1.0, 0.0)
    n_sel = min(SLC_TOPN, n_slc)
    unsel_t = jnp.where((rank < n_sel) & (blk <= cur), 0.0, 1.0)
    unsel_t = jnp.concatenate([unsel_t, jnp.zeros((128 - n_slc, tq), F32)], axis=0)
    unsel = jnp.transpose(unsel_t).astype(BF16)

    lhs = [jnp.concatenate([qa, unsel], axis=1) for qa in qas]
    m_sc[...] = jnp.full(m_sc.shape, NEG_INF, F32)
    acc_sc[...] = jnp.zeros(acc_sc.shape, F32)

    def slc_tile(kj, causal):
        start = pl.multiple_of(kj * tk, tk)
        rhs = srhs_sc[pl.ds(start, tk), :]
        val_even = spe_sc[pl.ds(start, tk), :]
        val_odd = spo_sc[pl.ds(start, tk), :]
        subs = [(h, r0) for h in range(NSA_HG) for r0 in range(0, tq, rb)]
        s = {}
        for h, r0 in subs:
            s[h, r0] = _nt(lhs[h][r0:r0 + rb], rhs)
            if causal:
                s[h, r0] = s[h, r0] + cbias_ref[r0:r0 + rb, :]
        m_prev = {(h, r0): m_sc[h, r0:r0 + rb] for h, r0 in subs}
        m_next = {key: jnp.maximum(m_prev[key], jnp.max(s[key], axis=-1, keepdims=True)) for key in subs}
        p = {key: jnp.exp(s[key] - jnp.concatenate([m_next[key]] * (tk // 128), axis=1)).astype(BF16)
             for key in subs}
        alpha = {key: jnp.exp(m_prev[key] - m_next[key]) for key in subs}
        for h, r0 in subs:
            m_sc[h, r0:r0 + rb] = m_next[h, r0]
        for pair in range(n_pairs):
            for r0 in range(0, tq, rb):
                rows = slice(r0, r0 + rb)
                scale = _pair_lanes(alpha[2 * pair, r0], alpha[2 * pair + 1, r0])
                scale = jnp.concatenate([scale, scale], axis=1)
                acc_sc[pair, rows] = (scale * acc_sc[pair, rows]
                                      + jnp.dot(p[2 * pair, r0], val_even, preferred_element_type=F32)
                                      + jnp.dot(p[2 * pair + 1, r0], val_odd, preferred_element_type=F32))

    def off_diagonal(kj, carry):
        slc_tile(kj, False)
        return carry

    lax.fori_loop(0, i, off_diagonal, 0)
    slc_tile(i, True)
    o_slc = []
    for pair in range(n_pairs):
        acc = acc_sc[pair]
        o_slc.append(acc[:, :2 * HEAD_DIM] / acc[:, 2 * HEAD_DIM:])
    o_slc = jnp.concatenate(o_slc, axis=1)

    sg = _split(jax.nn.sigmoid(gate_ref[...]))
    gates = (jnp.dot(sg[0], gexp_ref[...], preferred_element_type=F32)
             + jnp.dot(sg[1], gexp_ref[...], preferred_element_type=F32))
    wd = NSA_HG * HEAD_DIM
    y = gates[:, :wd] * o_cmp + gates[:, wd:2 * wd] * o_slc + gates[:, 2 * wd:] * o_win
    z = z_ref[...]
    o_ref[...] = (y * (z * jax.nn.sigmoid(z))).astype(o_ref.dtype)


def _nsa(q, z, gate, kv_slc, kv_win, kcvc):
    b, t, _ = q.shape
    tq = ATT_TILE
    assert t % tq == 0 and t >= NSA_WINDOW + tq and NSA_WINDOW % tq == 0
    n_slc = t // SLC_BLOCK
    nch = t // CMP_STRIDE
    wd = NSA_HG * HEAD_DIM
    negexp = np.where(np.arange(t)[:, None] // SLC_BLOCK == np.arange(128)[None, :], -2.0 ** 100, 0.0)
    ci = np.arange(nch)[None, :] * CMP_STRIDE
    sj = np.arange(n_slc)[:, None] * SLC_BLOCK
    overlap_t = ((ci < sj + SLC_BLOCK) & (ci + CMP_BLOCK > sj)).astype(np.float32)
    wbias = _band_bias_table(range(0, NSA_WINDOW + 1, tq), tq, NSA_WINDOW + tq, NSA_WINDOW)
    cbias = _band_bias_table([0], tq, tq, t)[0]
    gexp = np.zeros((128, 3 * wd), np.float32)
    for br in range(3):
        for h in range(NSA_HG):
            gexp[br * NSA_HG + h, br * wd + h * HEAD_DIM:br * wd + (h + 1) * HEAD_DIM] = 1.0

    def const(arr):
        return pl.BlockSpec(arr.shape, lambda bi, g, i: (0,) * arr.ndim)

    consts = [jnp.asarray(negexp, BF16), jnp.asarray(overlap_t, BF16), jnp.asarray(wbias),
              jnp.asarray(cbias), jnp.asarray(gexp, BF16)]
    return pl.pallas_call(
        _nsa_kernel,
        grid=(b, NSA_GROUPS, t // tq),
        in_specs=[
            pl.BlockSpec((None, tq, 256), lambda bi, g, i: (bi, i, g)),
            pl.BlockSpec((None, tq, 256), lambda bi, g, i: (bi, i, g)),
            pl.BlockSpec((None, tq, 128), lambda bi, g, i: (bi, i, g)),
            pl.BlockSpec((None, t, 128), lambda bi, g, i: (bi, 0, g)),
            pl.BlockSpec((None, t, 128), lambda bi, g, i: (bi, 0, g)),
            pl.BlockSpec((None, None, nch, HEAD_DIM), lambda bi, g, i: (bi, g, 0, 0)),
            pl.BlockSpec((None, None, nch, HEAD_DIM), lambda bi, g, i: (bi, 2 + g, 0, 0)),
        ] + [const(a) for a in consts],
        out_specs=pl.BlockSpec((None, tq, 256), lambda bi, g, i: (bi, i, g)),
        out_shape=jax.ShapeDtypeStruct((b, t, 512), BF16),
        scratch_shapes=[
            pltpu.VMEM((t, 256), BF16),
            pltpu.VMEM((t, 256), BF16),
            pltpu.VMEM((t, 256), BF16),
            pltpu.VMEM((t, 256), BF16),
            pltpu.VMEM((t, 256), BF16),
            pltpu.VMEM((NSA_HG, tq, 128), F32),
            pltpu.VMEM((NSA_HG // 2, tq, 256), F32),
        ],
        compiler_params=_params(("parallel", "parallel", "arbitrary")),
    )(q, z, gate, kv_slc, kv_win, kcvc, kcvc, *consts)


def _swa_kernel(sink_ref, q_ref, z_ref, kv_ref, bias_ref, o_ref, pe_sc, po_sc):
    g = pl.program_id(1)
    i = pl.program_id(2)
    tq = ATT_TILE
    rb = ATT_ROWS

    @pl.when(i == 0)
    def _():
        pe_sc[...], po_sc[...] = _pair_values(kv_ref[...])

    q = q_ref[...] * jnp.asarray(HEAD_DIM ** -0.5, BF16)
    qas = _widen_queries(q, SWA_HG)
    width = SWA_WINDOW + tq
    k0 = pl.multiple_of(jnp.maximum(i * tq - SWA_WINDOW, 0), SWA_WINDOW)
    keys = kv_ref[pl.ds(k0, width), :]
    val_even = pe_sc[pl.ds(k0, width), :]
    val_odd = po_sc[pl.ds(k0, width), :]
    variant = jnp.minimum(i, bias_ref.shape[0] - 1)
    sinks = (sink_ref[g * SWA_HG], sink_ref[g * SWA_HG + 1])
    blocks = [(qas[0][r0:r0 + rb], qas[1][r0:r0 + rb], bias_ref[variant, r0:r0 + rb, :])
              for r0 in range(0, tq, rb)]
    y = jnp.concatenate(_band_pairs(blocks, keys, val_even, val_odd, sinks), axis=0)
    z = z_ref[...]
    o_ref[...] = (y * (z * jax.nn.sigmoid(z))).astype(o_ref.dtype)


def _swa(q, z, kv, sinks):
    b, t, _ = q.shape
    tq = ATT_TILE
    assert SWA_HG == 2 and t % tq == 0 and t >= SWA_WINDOW + tq
    bias = jnp.asarray(_band_bias_table([0, SWA_WINDOW], tq, SWA_WINDOW + tq, SWA_WINDOW))
    return pl.pallas_call(
        _swa_kernel,
        grid=(b, SWA_GROUPS, t // tq),
        in_specs=[
            pl.BlockSpec(memory_space=pltpu.SMEM),
            pl.BlockSpec((None, tq, 128), lambda bi, g, i: (bi, i, g)),
            pl.BlockSpec((None, tq, 128), lambda bi, g, i: (bi, i, g)),
            pl.BlockSpec((None, t, 128), lambda bi, g, i: (bi, 0, g)),
            pl.BlockSpec(bias.shape, lambda bi, g, i: (0, 0, 0)),
        ],
        out_specs=pl.BlockSpec((None, tq, 128), lambda bi, g, i: (bi, i, g)),
        out_shape=jax.ShapeDtypeStruct((b, t, 256), BF16),
        scratch_shapes=[pltpu.VMEM((t, 256), BF16), pltpu.VMEM((t, 256), BF16)],
        compiler_params=_params(("parallel", "parallel", "arbitrary")),
    )(sinks, q, z, kv, bias)


_NN = (((1,), (0,)), ((), ()))
_NT = (((1,), (1,)), ((), ()))


def _split(a):
    hi = a.astype(BF16)
    lo = (a - hi.astype(F32)).astype(BF16)
    return hi, lo


def _dot(a, b, dims=_NN):
    return lax.dot_general(a, b, dims, preferred_element_type=F32)


def _dot3(a, b, dims=_NN):
    return _dot(a[0], b[0], dims) + (_dot(a[0], b[1], dims) + _dot(a[1], b[0], dims))


def _dot_exact_lhs(a, b, n_terms):
    out = None
    for _ in range(n_terms):
        term = b.astype(BF16)
        b = b - term.astype(F32)
        part = _dot(a, term)
        out = part if out is None else out + part
    return out


def _dot_exact_rhs(x, ones):
    hi, lo = _split(x)
    return _dot(hi, ones) + _dot(lo, ones)


def _block_diag(pair, head_masks):
    return tuple(jnp.concatenate([x * m for m in head_masks], axis=0) for x in pair)


def _each(fn, *lists):
    return [fn(*items) for items in zip(*lists)]


def _unit_lower_inverse(lows, eye, same_sub, bd):
    diag = _each(lambda low: jnp.where(same_sub, low, 0.0), lows)
    off = _each(lambda low, d: low - d, lows, diag)
    inv = _each(lambda d: eye + d, diag)
    power = diag
    span = 2
    while span < RWKV_SUB:
        ps = _each(_split, power)
        power = _each(lambda p: _dot3(p, bd(p)), ps)
        inv = _each(lambda iv, pw: _dot3(_split(iv), bd(_split(eye + pw))), inv, power)
        span *= 2
    inv_s = _each(_split, inv)
    cross = _each(lambda iv, o: _dot3(iv, bd(_split(o))), inv_s, off)
    total = _each(lambda cr: eye + cr, cross)
    power = cross
    span = 2
    while span < RWKV_CHUNK // RWKV_SUB:
        ps = _each(_split, power)
        power = _each(lambda p: _dot3(p, bd(p)), ps)
        total = _each(lambda tt, pw: _dot3(_split(tt), bd(_split(eye + pw))), total, power)
        span *= 2
    return _each(lambda tt, iv: _dot3(_split(tt), bd(iv)), total, inv_s)


def _rwkv_chunks(a_t, r_t, b_t, k_t, b_e, k_e, v, p_end, s_bd, cst):
    C = RWKV_CHUNK
    eye, strict, incl, same_sub, head_masks, same_head, eye_bd = cst
    bd = lambda pair: _block_diag(pair, head_masks)

    ar = _each(lambda a, r: _split(jnp.concatenate([a, r], axis=0)), a_t, r_t)
    g_b = _each(lambda x, b: _dot3(x, bd(_split(b)), _NT), ar, b_t)
    g_k = _each(lambda x, k: _dot3(x, bd(_split(k)), _NT), ar, k_t)
    a_ab = _each(lambda g: jnp.where(strict, g[:C], 0.0), g_b)
    a_rb = _each(lambda g: _split(jnp.where(incl, g[C:], 0.0)), g_b)
    a_ak = _each(lambda g: _split(jnp.where(strict, g[:C], 0.0)), g_k)
    a_rk = _each(lambda g: _split(jnp.where(incl, g[C:], 0.0)), g_k)

    v_bd = _each(lambda x: bd(_split(x)), v)
    akv = _each(_dot3, a_ak, v_bd)
    t_inv = _each(_split, _unit_lower_inverse(a_ab, eye, same_sub, bd))
    w_m = _each(lambda t, a: _dot3(t, bd(_split(a))), t_inv, a_t)
    u0 = _each(lambda t, x: _dot3(t, bd(_split(x))), t_inv, akv)
    r_m = _each(lambda r, g, w: r + _dot3(g, bd(_split(w))), r_t, a_rb, w_m)
    y0 = _each(lambda g, u, gk, vb: _dot3(g, bd(_split(u))) + _dot3(gk, vb), a_rb, u0, a_rk, v_bd)

    bk = _each(lambda b, k: _split(jnp.transpose(jnp.concatenate([b, k], axis=0))), b_e, k_e)
    m_add = _each(lambda x, w: _dot3((x[0][:, :C], x[1][:, :C]), _split(w)), bk, w_m)
    n_all = _each(lambda x, u, vv: _dot3(x, _split(jnp.concatenate([u, vv], axis=0))), bk, u0, v)
    m_bd = _each(lambda m, p: _split(jnp.where(same_head, m, 0.0) + jnp.where(eye_bd, p, 0.0)), m_add, p_end)
    n_bd = _each(lambda n: jnp.where(same_head, n, 0.0), n_all)
    r_s = _each(_split, r_m)

    ys = []
    for j in range(len(v)):
        s_pair = _split(s_bd)
        ys.append(_dot3(r_s[j], s_pair) + y0[j])
        s_bd = _dot3(m_bd[j], s_pair) + n_bd[j]
    return ys, s_bd


def _rwkv_kernel(*refs, has_vres):
    if has_vres:
        (feat_ref, prev_ref, z_ref, vfirst_ref, mu_ref, w0_ref, w2_ref, a0_ref, a2_ref, kk_ref, ka_ref,
         rk_ref, lnw_ref, lnb_ref, hsum_ref, v0_ref, v1_ref, v2_ref, y_ref, state) = refs
    else:
        (feat_ref, prev_ref, z_ref, mu_ref, w0_ref, w2_ref, a0_ref, a2_ref, kk_ref, ka_ref,
         rk_ref, lnw_ref, lnb_ref, hsum_ref, y_ref, vout_ref, state) = refs
    c = pl.program_id(1)
    C = RWKV_CHUNK
    W = RWKV_WIDTH
    n_tok = feat_ref.shape[0]

    @pl.when(c == 0)
    def _():
        state[...] = jnp.zeros(state.shape, F32)

    def lora(x, w_ref):
        return _dot3(_split(x), _split(w_ref[...]))

    hsum = hsum_ref[...]

    def head_sum(x):
        return _dot_exact_rhs(x, hsum)

    feat = feat_ref[...]
    row = lax.broadcasted_iota(jnp.int32, feat.shape, 0)
    last_prev = jnp.where(c == 0, 0.0, prev_ref[7:8, :])
    prev = jnp.where(row == 0, last_prev, pltpu.roll(feat, 1, 0))
    xs = feat + (prev - feat) * mu_ref[...]
    r = xs[:, :W]
    k = xs[:, W:2 * W]
    v = xs[:, 2 * W:3 * W]
    wd = xs[:, 3 * W:3 * W + DECAY_LORA]
    ad = xs[:, 3 * W + DECAY_LORA:]

    pre = -(w0_ref[...] + lora(jnp.tanh(wd), w2_ref))
    softplus = jnp.maximum(pre, 0.0) + jnp.log(1.0 + jnp.exp(-jnp.abs(pre)))
    logw = -jnp.exp(-softplus - 0.5)
    if has_vres:
        mix = jax.nn.sigmoid(v0_ref[...] + lora(lora(v, v1_ref), v2_ref))
        v = v + (vfirst_ref[...] - v) * mix
    else:
        vout_ref[...] = v
    alpha = jax.nn.sigmoid(a0_ref[...] + lora(ad, a2_ref))
    kk = k * kk_ref[...]
    kk = kk / jnp.maximum(jnp.sqrt(head_sum(kk * kk)), 1e-12)
    k = k * (1.0 + (alpha - 1.0) * ka_ref[...])

    ti = lax.broadcasted_iota(jnp.int32, (n_tok, n_tok), 0)
    tj = lax.broadcasted_iota(jnp.int32, (n_tok, n_tok), 1)
    tri = jnp.where((ti >= tj) & (ti // C == tj // C), 1.0, 0.0).astype(BF16)
    cum = _dot_exact_lhs(tri, logw, 3)

    t_idx = lax.broadcasted_iota(jnp.int32, (C, W), 0)
    i_idx = lax.broadcasted_iota(jnp.int32, (C, W), 1) % HEAD_DIM
    lane_head = lax.broadcasted_iota(jnp.int32, (C, W), 1) // HEAD_DIM
    bi = lax.broadcasted_iota(jnp.int32, (W, W), 0)
    bj = lax.broadcasted_iota(jnp.int32, (W, W), 1)
    cst = (
        jnp.where(t_idx == i_idx, 1.0, 0.0),
        t_idx > i_idx,
        t_idx >= i_idx,
        (t_idx // RWKV_SUB) == (i_idx // RWKV_SUB),
        [jnp.where(lane_head == h, 1.0, 0.0).astype(BF16) for h in range(RWKV_HEADS)],
        (bi // HEAD_DIM) == (bj // HEAD_DIM),
        bi == bj,
    )

    a_t = -kk * jnp.exp(cum - logw)
    r_t = r * jnp.exp(cum)
    inv_decay = jnp.exp(-cum)
    b_raw = kk * alpha
    b_t = b_raw * inv_decay
    k_t = k * inv_decay

    chunks = [slice(j * C, (j + 1) * C) for j in range(n_tok // C)]
    cum_end = [cum[rows.stop - 1:rows.stop, :] for rows in chunks]
    to_end = [jnp.exp(ce - cum[rows]) for ce, rows in zip(cum_end, chunks)]
    ys, s_bd = _rwkv_chunks(
        [a_t[rows] for rows in chunks], [r_t[rows] for rows in chunks], [b_t[rows] for rows in chunks],
        [k_t[rows] for rows in chunks], [b_raw[rows] * te for rows, te in zip(chunks, to_end)],
        [k[rows] * te for rows, te in zip(chunks, to_end)], [v[rows] for rows in chunks],
        [jnp.exp(ce) for ce in cum_end], state[...], cst)
    state[...] = s_bd
    y = jnp.concatenate(ys, axis=0)

    inv_n = 1.0 / HEAD_DIM
    mean = head_sum(y) * inv_n
    cen = y - mean
    var = head_sum(cen * cen) * inv_n
    y = cen * lax.rsqrt(var + GN_EPS) * lnw_ref[...] + lnb_ref[...]
    y = y + head_sum(r * k * rk_ref[...]) * v
    z = z_ref[...]
    y_ref[...] = (y * (z * jax.nn.sigmoid(z))).astype(y_ref.dtype)


def _rwkv(feat, z, v_first, p):
    b, t, _ = feat.shape
    n_tok = RWKV_STEP_CHUNKS * RWKV_CHUNK
    assert t % n_tok == 0
    W = RWKV_WIDTH
    has_vres = v_first is not None
    hsum = (np.arange(W)[:, None] // HEAD_DIM == np.arange(W)[None, :] // HEAD_DIM).astype(np.float32)

    def tok(width):
        return pl.BlockSpec((None, n_tok, width), lambda bi, c: (bi, c, 0))

    def full(arr):
        return pl.BlockSpec(arr.shape, lambda bi, c: (0,) * arr.ndim)

    args = [feat, feat, z]
    specs = [tok(SHIFT_WIDTH),
             pl.BlockSpec((None, 8, SHIFT_WIDTH), lambda bi, c: (bi, jnp.maximum(c * (n_tok // 8) - 1, 0), 0)),
             tok(W)]
    if has_vres:
        args.append(v_first)
        specs.append(tok(W))
    names = ["mu", "w0", "w2", "a0", "a2", "k_k", "k_a", "r_k", "ln_w", "ln_b"]
    consts = [p[n] for n in names] + [jnp.asarray(hsum, BF16)]
    if has_vres:
        consts += [p["v0"], p["v1"], p["v2"]]
    args += consts
    specs += [full(a) for a in consts]
    y_shape = jax.ShapeDtypeStruct((b, t, W), BF16)
    if has_vres:
        out_shape, out_specs = y_shape, tok(W)
    else:
        out_shape = (y_shape, jax.ShapeDtypeStruct((b, t, W), F32))
        out_specs = (tok(W), tok(W))
    res = pl.pallas_call(
        functools.partial(_rwkv_kernel, has_vres=has_vres),
        grid=(b, t // n_tok),
        in_specs=specs,
        out_specs=out_specs,
        out_shape=out_shape,
        scratch_shapes=[pltpu.VMEM((W, W), F32)],
        compiler_params=_params(("parallel", "arbitrary")),
    )(*args)
    if has_vres:
        return res, v_first
    return res[0], res[1]


def _merge_kernel(x_ref, ya_ref, yb_ref, yc_ref, mg_ref, bm_ref, pa_ref, pb_ref, pc_ref, wo_ref, fg_ref,
                  o_ref, *, final_norm):
    d = D_MODEL
    mixed = None
    for j, (y_ref, p_ref) in enumerate(((ya_ref, pa_ref), (yb_ref, pb_ref), (yc_ref, pc_ref))):
        gate = jax.nn.sigmoid(mg_ref[:, j * d:(j + 1) * d] + bm_ref[j:j + 1, :])
        term = gate * jnp.dot(y_ref[...], p_ref[...], preferred_element_type=F32)
        mixed = term if mixed is None else mixed + term
    x = x_ref[...] + jnp.dot(mixed.astype(BF16), wo_ref[...], preferred_element_type=F32)
    if final_norm:
        ms = jnp.mean(x * x, axis=-1, keepdims=True)
        x = x * lax.rsqrt(ms + NORM_EPS) * fg_ref[...]
    o_ref[...] = x


def _merge(x2d, ya, yb, yc, mg, bm, pa, pb, pc, wo, fg, final_norm):
    m = x2d.shape[0]
    tm = ROW_TILE

    def rows(width):
        return pl.BlockSpec((tm, width), lambda i: (i, 0))

    def full(arr):
        return pl.BlockSpec(arr.shape, lambda i: (0, 0))

    return pl.pallas_call(
        functools.partial(_merge_kernel, final_norm=final_norm),
        grid=(m // tm,),
        in_specs=[rows(D_MODEL), rows(512), rows(256), rows(256), rows(3 * D_MODEL),
                  full(bm), full(pa), full(pb), full(pc), full(wo), full(fg)],
        out_specs=rows(D_MODEL),
        out_shape=jax.ShapeDtypeStruct((m, D_MODEL), F32),
        compiler_params=_params(("parallel",)),
    )(x2d, ya, yb, yc, mg, bm, pa, pb, pc, wo, fg)


def kernel(x, norm_g, w_in, b_merge, cmp_pe_k, cmp_w1_k, cmp_w2_k, cmp_pe_v, cmp_w1_v, cmp_w2_v, swa_sinks,
           rwkv_mu, rwkv_w0, rwkv_w2, rwkv_a0, rwkv_a2, rwkv_k_k, rwkv_k_a, rwkv_r_k, rwkv_ln_w, rwkv_ln_b,
           rwkv_v0, rwkv_v1, rwkv_v2, proj_a, proj_b, proj_c, w_out, final_g):
    b, t, d = x.shape
    depth = w_in.shape[0]
    m = b * t
    x2d = x.reshape(m, d)
    col_idx = jnp.asarray(_COL_IDX)
    v_first = None
    for l in range(depth):
        w_pad = jnp.concatenate([w_in[l], jnp.zeros((d, 1), w_in.dtype)], axis=1)
        w = jnp.take(w_pad, col_idx, axis=1).astype(BF16)
        outs = _in_proj(x2d, norm_g[l].reshape(1, d), w)
        seg = {name: o.reshape(b, t, o.shape[-1]) for (name, _, _), o in zip(_SEGS, outs)}

        pe = jnp.stack([cmp_pe_k[l], cmp_pe_v[l]]).reshape(2, 1, CMP_BLOCK * HEAD_DIM)
        kcvc = _compress(seg["a_kv_cmp"], pe, jnp.stack([cmp_w1_k[l], cmp_w1_v[l]]),
                         jnp.stack([cmp_w2_k[l], cmp_w2_v[l]]))
        y_a = _nsa(seg["a_q"], seg["a_z"], seg["a_gate"], seg["a_kv_slc"], seg["a_kv_win"], kcvc)
        y_b = _swa(seg["b_q"], seg["b_z"], seg["b_kv"], swa_sinks[l])

        row = lambda a: a.reshape(1, -1)
        p = {"mu": row(rwkv_mu[l]), "w0": row(rwkv_w0[l]), "w2": rwkv_w2[l], "a0": row(rwkv_a0[l]),
             "a2": rwkv_a2[l], "k_k": row(rwkv_k_k[l]), "k_a": row(rwkv_k_a[l]), "r_k": row(rwkv_r_k[l]),
             "ln_w": row(rwkv_ln_w[l]), "ln_b": row(rwkv_ln_b[l])}
        if l > 0:
            p.update(v0=row(rwkv_v0[l - 1]), v1=rwkv_v1[l - 1], v2=rwkv_v2[l - 1])
        y_c, v_first = _rwkv(seg["c_shift"], seg["c_z"], v_first, p)

        x2d = _merge(x2d, y_a.reshape(m, -1), y_b.reshape(m, -1), y_c.reshape(m, -1),
                     seg["merge"].reshape(m, -1), b_merge[l], proj_a[l].astype(BF16), proj_b[l].astype(BF16),
                     proj_c[l].astype(BF16), w_out[l].astype(BF16), final_g.reshape(1, d),
                     final_norm=(l == depth - 1))
    return x2d.reshape(b, t, d)
```

```python
import functools

import numpy as np
import jax
import jax.numpy as jnp
from jax import lax
from jax.experimental import pallas as pl
from jax.experimental.pallas import tpu as pltpu

F32 = jnp.float32
BF16 = jnp.bfloat16
HIGHEST = lax.Precision.HIGHEST

D_MODEL = 1024
HEAD_DIM = 64
NSA_HEADS = 8
NSA_GROUPS = 2
NSA_HG = NSA_HEADS // NSA_GROUPS
CMP_STRIDE = 16
CMP_BLOCK = 32
CMP_HIDDEN = 128
SLC_BLOCK = 64
SLC_TOPN = 16
NSA_WINDOW = 512
SWA_HEADS = 4
SWA_GROUPS = 2
SWA_HG = SWA_HEADS // SWA_GROUPS
SWA_WINDOW = 128
RWKV_HEADS = 4
RWKV_WIDTH = RWKV_HEADS * HEAD_DIM
DECAY_LORA = 64
ICLR_LORA = 64
VRES_LORA = 32
SHIFT_WIDTH = 3 * RWKV_WIDTH + DECAY_LORA + ICLR_LORA
NORM_EPS = 1e-6
GN_EPS = 64e-5
NEG_INF = -1e30
FORCE = 1e9

_REF_SEGMENTS = (
    ("a_q", 512), ("a_kv_cmp", 256), ("a_kv_slc", 256), ("a_kv_win", 256), ("a_gate", 24),
    ("a_z", 512), ("b_q", 256), ("b_kv", 256), ("b_z", 256), ("c_shift", SHIFT_WIDTH),
    ("c_z", 256), ("merge", 3 * D_MODEL),
)
N_IN = sum(w for _, w in _REF_SEGMENTS)

VMEM_LIMIT = 56 * 1024 * 1024
ROW_TILE = 256
ATT_TILE = 256
ATT_ROWS = 128
RWKV_CHUNK = 64
RWKV_STEP_CHUNKS = 4
RWKV_SUB = 16


def _ref_offsets():
    out, off = {}, 0
    for name, width in _REF_SEGMENTS:
        out[name] = off
        off += width
    return out


def _kv_interleave(base):
    idx = []
    for g in range(2):
        idx += list(range(base + 128 + g * 64, base + 128 + (g + 1) * 64))
        idx += list(range(base + g * 64, base + (g + 1) * 64))
    return idx


def _projection_layout():
    o = _ref_offsets()
    segs, idx = [], []

    def add(name, cols, dtype):
        segs.append((name, len(cols), dtype))
        idx.extend(cols)

    add("a_q", list(range(o["a_q"], o["a_q"] + 512)), BF16)
    add("a_z", list(range(o["a_z"], o["a_z"] + 512)), F32)
    add("a_kv_cmp", list(range(o["a_kv_cmp"], o["a_kv_cmp"] + 256)), BF16)
    add("a_kv_slc", _kv_interleave(o["a_kv_slc"]), BF16)
    add("a_kv_win", _kv_interleave(o["a_kv_win"]), BF16)
    gate = []
    for g in range(NSA_GROUPS):
        cols = [N_IN] * 128
        for br in range(3):
            for h in range(NSA_HG):
                cols[br * NSA_HG + h] = o["a_gate"] + br * NSA_HEADS + g * NSA_HG + h
        gate += cols
    add("a_gate", gate, F32)
    add("b_q", list(range(o["b_q"], o["b_q"] + 256)), BF16)
    add("b_kv", _kv_interleave(o["b_kv"]), BF16)
    add("b_z", list(range(o["b_z"], o["b_z"] + 256)), F32)
    add("c_shift", list(range(o["c_shift"], o["c_shift"] + SHIFT_WIDTH)), F32)
    add("c_z", list(range(o["c_z"], o["c_z"] + 256)), F32)
    add("merge", list(range(o["merge"], o["merge"] + 3 * D_MODEL)), F32)
    return tuple(segs), np.asarray(idx, np.int32)


_SEGS, _COL_IDX = _projection_layout()
_N_PROJ = int(_COL_IDX.shape[0])


def _params(sem):
    return pltpu.CompilerParams(dimension_semantics=sem, vmem_limit_bytes=VMEM_LIMIT)


def _in_proj_kernel(x_ref, g_ref, w_ref, *out_refs):
    x = x_ref[...]
    ms = jnp.mean(x * x, axis=-1, keepdims=True)
    xn = (x * lax.rsqrt(ms + NORM_EPS) * g_ref[...]).astype(BF16)
    off = 0
    for o_ref, (_, width, _) in zip(out_refs, _SEGS):
        for c0 in range(0, width, 512):
            cw = min(512, width - c0)
            o_ref[:, c0:c0 + cw] = jnp.dot(
                xn, w_ref[:, off + c0:off + c0 + cw], preferred_element_type=F32).astype(o_ref.dtype)
        off += width


def _in_proj(x2d, g, w):
    m = x2d.shape[0]
    tm = ROW_TILE
    out_shape = [jax.ShapeDtypeStruct((m, width), dt) for _, width, dt in _SEGS]
    out_specs = [pl.BlockSpec((tm, width), lambda i: (i, 0)) for _, width, _ in _SEGS]
    return pl.pallas_call(
        _in_proj_kernel,
        grid=(m // tm,),
        in_specs=[
            pl.BlockSpec((tm, D_MODEL), lambda i: (i, 0)),
            pl.BlockSpec((1, D_MODEL), lambda i: (0, 0)),
            pl.BlockSpec((D_MODEL, _N_PROJ), lambda i: (0, 0), pipeline_mode=pl.Buffered(1)),
        ],
        out_specs=out_specs,
        out_shape=out_shape,
        compiler_params=_params(("parallel",)),
    )(x2d, g, w)


def _compress_kernel(z_ref, pe_ref, w1_ref, w2_ref, o_ref):
    z = z_ref[...].astype(F32)
    half = CMP_STRIDE * HEAD_DIM
    w1 = w1_ref[...]
    first = jnp.dot(z, w1[:half], precision=HIGHEST, preferred_element_type=F32)
    second = jnp.dot(z, w1[half:], precision=HIGHEST, preferred_element_type=F32)
    n = z.shape[0]
    hid = first + pltpu.roll(second, n - 1, 0)
    hid = hid + jnp.dot(pe_ref[...], w1, precision=HIGHEST, preferred_element_type=F32)
    act = hid * jax.nn.sigmoid(hid)
    o_ref[...] = jnp.dot(act, w2_ref[...], precision=HIGHEST, preferred_element_type=F32)


def _compress(kv_cmp, pe, w1, w2):
    b, t, _ = kv_cmp.shape
    nch = t // CMP_STRIDE
    z = kv_cmp.reshape(b, nch, CMP_STRIDE, 4, HEAD_DIM)
    z = jnp.transpose(z, (0, 3, 1, 2, 4)).reshape(b, 4, nch, CMP_STRIDE * HEAD_DIM)
    return pl.pallas_call(
        _compress_kernel,
        grid=(b, 4),
        in_specs=[
            pl.BlockSpec((None, None, nch, CMP_STRIDE * HEAD_DIM), lambda i, j: (i, j, 0, 0)),
            pl.BlockSpec((None, 1, CMP_BLOCK * HEAD_DIM), lambda i, j: (j // 2, 0, 0)),
            pl.BlockSpec((None, CMP_BLOCK * HEAD_DIM, CMP_HIDDEN), lambda i, j: (j // 2, 0, 0)),
            pl.BlockSpec((None, CMP_HIDDEN, HEAD_DIM), lambda i, j: (j // 2, 0, 0)),
        ],
        out_specs=pl.BlockSpec((None, None, nch, HEAD_DIM), lambda i, j: (i, j, 0, 0)),
        out_shape=jax.ShapeDtypeStruct((b, 4, nch, HEAD_DIM), F32),
        compiler_params=_params(("parallel", "parallel")),
    )(z, pe, w1, w2)


def _widen_queries(q, n_heads):
    zeros = jnp.zeros((q.shape[0], HEAD_DIM), q.dtype)
    return [jnp.concatenate([zeros, q[:, h * HEAD_DIM:(h + 1) * HEAD_DIM]], axis=1) for h in range(n_heads)]


def _pair_values(kv):
    swapped = jnp.concatenate([kv[:, HEAD_DIM:], kv[:, :HEAD_DIM]], axis=1)
    lane = lax.broadcasted_iota(jnp.int32, kv.shape, 1)
    low = jnp.where(lane < HEAD_DIM, 1.0, 0.0).astype(kv.dtype)
    high = jnp.where(lane < HEAD_DIM, 0.0, 1.0).astype(kv.dtype)
    even = jnp.concatenate([kv * low, low], axis=1)
    odd = jnp.concatenate([swapped * high, high], axis=1)
    return even, odd


def _pair_lanes(even, odd):
    low = lax.broadcasted_iota(jnp.int32, even.shape, 1) < HEAD_DIM
    return jnp.where(low, even, odd)


def _nt(a, b):
    return lax.dot_general(a, b, (((1,), (1,)), ((), ())), preferred_element_type=F32)


def _interleave(*streams):
    results = [None] * len(streams)
    live = dict(enumerate(streams))
    while live:
        for idx in list(live):
            try:
                next(live[idx])
            except StopIteration as stop:
                results[idx] = stop.value
                del live[idx]
    return results


def _band_pairs(blocks, keys, val_even, val_odd, sinks=None):
    s = [[_nt(qa, keys) + bias for qa in (qa_even, qa_odd)] for qa_even, qa_odd, bias in blocks]
    yield
    m = [[jnp.max(x, axis=-1, keepdims=True) for x in pair] for pair in s]
    if sinks is not None:
        m = [[jnp.maximum(x, sinks[e]) for e, x in enumerate(pair)] for pair in m]
    yield
    p = [[jnp.exp(x - mx).astype(BF16) for x, mx in zip(xs, ms)] for xs, ms in zip(s, m)]
    yield
    outs = []
    for ps, ms in zip(p, m):
        acc = (jnp.dot(ps[0], val_even, preferred_element_type=F32)
               + jnp.dot(ps[1], val_odd, preferred_element_type=F32))
        denom = acc[:, 2 * HEAD_DIM:]
        if sinks is not None:
            shape = (acc.shape[0], 2 * HEAD_DIM)
            denom = denom + _pair_lanes(jnp.broadcast_to(jnp.exp(sinks[0] - ms[0]), shape),
                                        jnp.broadcast_to(jnp.exp(sinks[1] - ms[1]), shape))
        outs.append(acc[:, :2 * HEAD_DIM] / denom)
        yield
    return outs


def _band_bias_table(deltas, rows, width, window):
    r = np.arange(rows)[:, None]
    c = np.arange(width)[None, :]
    out = [np.where((c <= r + d) & (c > r + d - window), 0.0, NEG_INF) for d in deltas]
    return np.stack(out).astype(np.float32)


def _attention_kernel(sink_ref, q_ref, z_ref, gate_ref, kvs_ref, kvw_ref, kc_ref, vc_ref, bq_ref, bz_ref,
                      bkv_ref, negexp_ref, ovl_ref, wbias_ref, cbias_ref, gexp_ref, bbias_ref,
                      o_ref, ob_ref, srhs_sc, spe_sc, spo_sc, wpe_sc, wpo_sc, bpe_sc, bpo_sc, m_sc, acc_sc):
    g = pl.program_id(1)
    i = pl.program_id(2)
    tq = ATT_TILE
    rb = ATT_ROWS
    n_rb = tq // rb
    n_cmp_pad = kc_ref.shape[0]
    n_slc = ovl_ref.shape[0]
    n_pairs = NSA_HG // 2

    @pl.when(i == 0)
    def _():
        kvs = kvs_ref[...]
        srhs_sc[...] = jnp.concatenate([kvs, negexp_ref[...]], axis=1)
        spe_sc[...], spo_sc[...] = _pair_values(kvs)
        wpe_sc[...], wpo_sc[...] = _pair_values(kvw_ref[...])
        bpe_sc[...], bpo_sc[...] = _pair_values(bkv_ref[...])

    scale = jnp.asarray(HEAD_DIM ** -0.5, BF16)
    qas = _widen_queries(q_ref[...] * scale, NSA_HG)

    def window_stream():
        width = NSA_WINDOW + tq
        k0 = pl.multiple_of(jnp.maximum(i * tq - NSA_WINDOW, 0), tq)
        variant = jnp.minimum(i, wbias_ref.shape[0] - 1)
        blocks = [(qas[2 * pair][r0:r0 + rb], qas[2 * pair + 1][r0:r0 + rb],
                   wbias_ref[variant, r0:r0 + rb, :]) for pair in range(n_pairs) for r0 in range(0, tq, rb)]
        outs = yield from _band_pairs(blocks, kvw_ref[pl.ds(k0, width), :], wpe_sc[pl.ds(k0, width), :],
                                      wpo_sc[pl.ds(k0, width), :])
        return jnp.concatenate([jnp.concatenate(outs[pair * n_rb:(pair + 1) * n_rb], axis=0)
                                for pair in range(n_pairs)], axis=1)

    def sink_window_stream():
        qbs = _widen_queries(bq_ref[...] * scale, SWA_HG)
        width = SWA_WINDOW + tq
        k0 = pl.multiple_of(jnp.maximum(i * tq - SWA_WINDOW, 0), SWA_WINDOW)
        variant = jnp.minimum(i, bbias_ref.shape[0] - 1)
        sinks = (sink_ref[g * SWA_HG], sink_ref[g * SWA_HG + 1])
        blocks = [(qbs[0][r0:r0 + rb], qbs[1][r0:r0 + rb], bbias_ref[variant, r0:r0 + rb, :])
                  for r0 in range(0, tq, rb)]
        outs = yield from _band_pairs(blocks, bkv_ref[pl.ds(k0, width), :], bpe_sc[pl.ds(k0, width), :],
                                      bpo_sc[pl.ds(k0, width), :], sinks)
        y = jnp.concatenate(outs, axis=0)
        zb = bz_ref[...]
        ob_ref[...] = (y * (zb * jax.nn.sigmoid(zb))).astype(ob_ref.dtype)

    def compressed_stream():
        zeros_k = jnp.zeros((n_cmp_pad, HEAD_DIM), BF16)
        kc_terms = [jnp.concatenate([zeros_k, t], axis=1) for t in _split(kc_ref[...])]
        vc = vc_ref[...].astype(BF16)
        vc_even = jnp.concatenate([vc, zeros_k], axis=1)
        vc_odd = jnp.concatenate([zeros_k, vc], axis=1)
        tpos = i * tq + lax.broadcasted_iota(jnp.int32, (tq, n_cmp_pad), 0)
        cmp_end = lax.broadcasted_iota(jnp.int32, (tq, n_cmp_pad), 1) * CMP_STRIDE + (CMP_BLOCK - 1)
        valid_c = cmp_end <= tpos
        s = [_nt(qa, kc_terms[0]) + _nt(qa, kc_terms[1]) for qa in qas]
        yield
        s = [jnp.where(valid_c, x, NEG_INF) for x in s]
        m = [jnp.max(x, axis=-1, keepdims=True) for x in s]
        yield
        p = [jnp.where(valid_c, jnp.exp(x - mx), 0.0) for x, mx in zip(s, m)]
        denom = [jnp.sum(x, axis=-1, keepdims=True) for x in p]
        yield
        p = [x * (1.0 / jnp.where(d > 0, d, 1.0)) for x, d in zip(p, denom)]
        p_sum = (p[0] + p[1]) + (p[2] + p[3])
        pb = [x.astype(BF16) for x in p]
        o_cmp = jnp.concatenate(
            [jnp.dot(pb[2 * pair], vc_even, preferred_element_type=F32)
             + jnp.dot(pb[2 * pair + 1], vc_odd, preferred_element_type=F32) for pair in range(n_pairs)], axis=1)
        yield
        imp = None
        rest = p_sum
        for _ in range(3):
            term = rest.astype(BF16)
            rest = rest - term.astype(F32)
            part = _nt(ovl_ref[...], term)
            imp = part if imp is None else imp + part
        yield
        blk = lax.broadcasted_iota(jnp.int32, (n_slc, tq), 0)
        cur = (i * tq + lax.broadcasted_iota(jnp.int32, (n_slc, tq), 1)) // SLC_BLOCK
        forced = (blk == 0) | (blk == cur) | (blk == cur - 1)
        score = jnp.where(forced, FORCE, jnp.where(blk <= cur, imp, -FORCE))
        ranks = [jnp.zeros((n_slc, tq), F32) for _ in range(4)]
        for r in range(n_slc):
            row = score[r:r + 1, :]
            ahead = (row > score) | ((row == score) & (r < blk))
            ranks[r % 4] = ranks[r % 4] + jnp.where(ahead, 1.0, 0.0)
            if r % 4 == 3:
                yield
        rank = (ranks[0] + ranks[1]) + (ranks[2] + ranks[3])
        n_sel = min(SLC_TOPN, n_slc)
        unsel_t = jnp.where((rank < n_sel) & (blk <= cur), 0.0, 1.0)
        unsel_t = jnp.concatenate([unsel_t, jnp.zeros((128 - n_slc, tq), F32)], axis=0)
        return o_cmp, jnp.transpose(unsel_t).astype(BF16)

    o_win, _, (o_cmp, unsel) = _interleave(window_stream(), sink_window_stream(), compressed_stream())

    lhs = [jnp.concatenate([qa, unsel], axis=1) for qa in qas]
    m_sc[...] = jnp.full(m_sc.shape, NEG_INF, F32)
    acc_sc[...] = jnp.zeros(acc_sc.shape, F32)
    subs = [(h, r0) for h in range(NSA_HG) for r0 in range(0, tq, rb)]

    def slc_keys(start, width, bias_lanes):
        rhs = srhs_sc[pl.ds(start, width), :]
        val_even = spe_sc[pl.ds(start, width), :]
        val_odd = spo_sc[pl.ds(start, width), :]
        s = {}
        for h, r0 in subs:
            s[h, r0] = _nt(lhs[h][r0:r0 + rb], rhs)
            if bias_lanes is not None:
                s[h, r0] = s[h, r0] + cbias_ref[r0:r0 + rb, bias_lanes:bias_lanes + width]
        m_prev = {(h, r0): m_sc[h, r0:r0 + rb] for h, r0 in subs}
        m_next = {key: jnp.maximum(m_prev[key], jnp.max(s[key], axis=-1, keepdims=True)) for key in subs}
        p = {key: jnp.exp(s[key] - jnp.concatenate([m_next[key]] * (width // 128), axis=1)).astype(BF16)
             for key in subs}
        alpha = {key: jnp.exp(m_prev[key] - m_next[key]) for key in subs}
        for h, r0 in subs:
            m_sc[h, r0:r0 + rb] = m_next[h, r0]
        for pair in range(n_pairs):
            for r0 in range(0, tq, rb):
                rows = slice(r0, r0 + rb)
                decay = _pair_lanes(alpha[2 * pair, r0], alpha[2 * pair + 1, r0])
                decay = jnp.concatenate([decay, decay], axis=1)
                acc_sc[pair, rows] = (decay * acc_sc[pair, rows]
                                      + jnp.dot(p[2 * pair, r0], val_even, preferred_element_type=F32)
                                      + jnp.dot(p[2 * pair + 1, r0], val_odd, preferred_element_type=F32))

    def wide(j, carry):
        slc_keys(pl.multiple_of(j * (2 * tq), 2 * tq), 2 * tq, None)
        return carry

    lax.fori_loop(0, i // 2, wide, 0)

    @pl.when(i % 2 == 0)
    def _():
        slc_keys(pl.multiple_of(i * tq, tq), tq, tq)

    @pl.when(i % 2 == 1)
    def _():
        slc_keys(pl.multiple_of((i - 1) * tq, 2 * tq), 2 * tq, 0)

    o_slc = []
    for pair in range(n_pairs):
        acc = acc_sc[pair]
        o_slc.append(acc[:, :2 * HEAD_DIM] / acc[:, 2 * HEAD_DIM:])
    o_slc = jnp.concatenate(o_slc, axis=1)

    sg = _split(jax.nn.sigmoid(gate_ref[...]))
    gates = (jnp.dot(sg[0], gexp_ref[...], preferred_element_type=F32)
             + jnp.dot(sg[1], gexp_ref[...], preferred_element_type=F32))
    wd = NSA_HG * HEAD_DIM
    y = gates[:, :wd] * o_cmp + gates[:, wd:2 * wd] * o_slc + gates[:, 2 * wd:] * o_win
    z = z_ref[...]
    o_ref[...] = (y * (z * jax.nn.sigmoid(z))).astype(o_ref.dtype)


def _attention(q, z, gate, kv_slc, kv_win, kcvc, bq, bz, bkv, sinks):
    b, t, _ = q.shape
    tq = ATT_TILE
    assert t % (2 * tq) == 0 and t >= NSA_WINDOW + tq and NSA_WINDOW % tq == 0 and SWA_HG == 2
    assert NSA_GROUPS == SWA_GROUPS
    n_slc = t // SLC_BLOCK
    nch = t // CMP_STRIDE
    wd = NSA_HG * HEAD_DIM
    negexp = np.where(np.arange(t)[:, None] // SLC_BLOCK == np.arange(128)[None, :], -2.0 ** 100, 0.0)
    ci = np.arange(nch)[None, :] * CMP_STRIDE
    sj = np.arange(n_slc)[:, None] * SLC_BLOCK
    overlap_t = ((ci < sj + SLC_BLOCK) & (ci + CMP_BLOCK > sj)).astype(np.float32)
    wbias = _band_bias_table(range(0, NSA_WINDOW + 1, tq), tq, NSA_WINDOW + tq, NSA_WINDOW)
    cbias = np.concatenate([np.zeros((tq, tq), np.float32), _band_bias_table([0], tq, tq, t)[0]], axis=1)
    bbias = _band_bias_table([0, SWA_WINDOW], tq, SWA_WINDOW + tq, SWA_WINDOW)
    gexp = np.zeros((128, 3 * wd), np.float32)
    for br in range(3):
        for h in range(NSA_HG):
            gexp[br * NSA_HG + h, br * wd + h * HEAD_DIM:br * wd + (h + 1) * HEAD_DIM] = 1.0

    def const(arr):
        return pl.BlockSpec(arr.shape, lambda bi, g, i: (0,) * arr.ndim)

    def tile(width):
        return pl.BlockSpec((None, tq, width), lambda bi, g, i: (bi, i, g))

    def keys():
        return pl.BlockSpec((None, t, 128), lambda bi, g, i: (bi, 0, g))

    consts = [jnp.asarray(negexp, BF16), jnp.asarray(overlap_t, BF16), jnp.asarray(wbias),
              jnp.asarray(cbias), jnp.asarray(gexp, BF16), jnp.asarray(bbias)]
    return pl.pallas_call(
        _attention_kernel,
        grid=(b, NSA_GROUPS, t // tq),
        in_specs=[
            pl.BlockSpec(memory_space=pltpu.SMEM),
            tile(256), tile(256), tile(128), keys(), keys(),
            pl.BlockSpec((None, None, nch, HEAD_DIM), lambda bi, g, i: (bi, g, 0, 0)),
            pl.BlockSpec((None, None, nch, HEAD_DIM), lambda bi, g, i: (bi, 2 + g, 0, 0)),
            tile(128), tile(128), keys(),
        ] + [const(a) for a in consts],
        out_specs=(tile(256), tile(128)),
        out_shape=(jax.ShapeDtypeStruct((b, t, 512), BF16), jax.ShapeDtypeStruct((b, t, 256), BF16)),
        scratch_shapes=[pltpu.VMEM((t, 256), BF16)] * 7 + [
            pltpu.VMEM((NSA_HG, tq, 128), F32),
            pltpu.VMEM((NSA_HG // 2, tq, 256), F32),
        ],
        compiler_params=_params(("parallel", "parallel", "arbitrary")),
    )(sinks, q, z, gate, kv_slc, kv_win, kcvc, kcvc, bq, bz, bkv, *consts)


_NN = (((1,), (0,)), ((), ()))
_NT = (((1,), (1,)), ((), ()))


def _split(a):
    hi = a.astype(BF16)
    lo = (a - hi.astype(F32)).astype(BF16)
    return hi, lo


def _dot(a, b, dims=_NN):
    return lax.dot_general(a, b, dims, preferred_element_type=F32)


def _dot3(a, b, dims=_NN):
    return _dot(a[0], b[0], dims) + (_dot(a[0], b[1], dims) + _dot(a[1], b[0], dims))


def _dot_exact_lhs(a, b, n_terms):
    out = None
    for _ in range(n_terms):
        term = b.astype(BF16)
        b = b - term.astype(F32)
        part = _dot(a, term)
        out = part if out is None else out + part
    return out


def _dot_exact_rhs(x, ones):
    hi, lo = _split(x)
    return _dot(hi, ones) + _dot(lo, ones)


def _block_diag(pair, head_masks):
    return tuple(jnp.concatenate([x * m for m in head_masks], axis=0) for x in pair)


def _each(fn, *lists):
    return [fn(*items) for items in zip(*lists)]


def _unit_lower_inverse(lows, eye, same_sub, bd):
    diag = _each(lambda low: jnp.where(same_sub, low, 0.0), lows)
    off = _each(lambda low, d: low - d, lows, diag)
    inv = _each(lambda d: eye + d, diag)
    power = diag
    span = 2
    while span < RWKV_SUB:
        ps = _each(_split, power)
        power = _each(lambda p: _dot3(p, bd(p)), ps)
        inv = _each(lambda iv, pw: _dot3(_split(iv), bd(_split(eye + pw))), inv, power)
        span *= 2
    inv_s = _each(_split, inv)
    cross = _each(lambda iv, o: _dot3(iv, bd(_split(o))), inv_s, off)
    total = _each(lambda cr: eye + cr, cross)
    power = cross
    span = 2
    while span < RWKV_CHUNK // RWKV_SUB:
        ps = _each(_split, power)
        power = _each(lambda p: _dot3(p, bd(p)), ps)
        total = _each(lambda tt, pw: _dot3(_split(tt), bd(_split(eye + pw))), total, power)
        span *= 2
    return _each(lambda tt, iv: _dot3(_split(tt), bd(iv)), total, inv_s)


def _rwkv_chunks(a_t, r_t, b_t, k_t, b_e, k_e, v, p_end, s_bd, cst):
    C = RWKV_CHUNK
    eye, strict, incl, same_sub, head_masks, same_head, eye_bd = cst
    bd = lambda pair: _block_diag(pair, head_masks)

    ar = _each(lambda a, r: _split(jnp.concatenate([a, r], axis=0)), a_t, r_t)
    g_b = _each(lambda x, b: _dot3(x, bd(_split(b)), _NT), ar, b_t)
    g_k = _each(lambda x, k: _dot3(x, bd(_split(k)), _NT), ar, k_t)
    a_ab = _each(lambda g: jnp.where(strict, g[:C], 0.0), g_b)
    a_rb = _each(lambda g: _split(jnp.where(incl, g[C:], 0.0)), g_b)
    a_ak = _each(lambda g: _split(jnp.where(strict, g[:C], 0.0)), g_k)
    a_rk = _each(lambda g: _split(jnp.where(incl, g[C:], 0.0)), g_k)

    v_bd = _each(lambda x: bd(_split(x)), v)
    akv = _each(_dot3, a_ak, v_bd)
    t_inv = _each(_split, _unit_lower_inverse(a_ab, eye, same_sub, bd))
    w_m = _each(lambda t, a: _dot3(t, bd(_split(a))), t_inv, a_t)
    u0 = _each(lambda t, x: _dot3(t, bd(_split(x))), t_inv, akv)
    r_m = _each(lambda r, g, w: r + _dot3(g, bd(_split(w))), r_t, a_rb, w_m)
    y0 = _each(lambda g, u, gk, vb: _dot3(g, bd(_split(u))) + _dot3(gk, vb), a_rb, u0, a_rk, v_bd)

    bk = _each(lambda b, k: _split(jnp.transpose(jnp.concatenate([b, k], axis=0))), b_e, k_e)
    m_add = _each(lambda x, w: _dot3((x[0][:, :C], x[1][:, :C]), _split(w)), bk, w_m)
    n_all = _each(lambda x, u, vv: _dot3(x, _split(jnp.concatenate([u, vv], axis=0))), bk, u0, v)
    m_bd = _each(lambda m, p: _split(jnp.where(same_head, m, 0.0) + jnp.where(eye_bd, p, 0.0)), m_add, p_end)
    n_bd = _each(lambda n: jnp.where(same_head, n, 0.0), n_all)
    r_s = _each(_split, r_m)

    ys = []
    for j in range(len(v)):
        s_pair = _split(s_bd)
        ys.append(_dot3(r_s[j], s_pair) + y0[j])
        s_bd = _dot3(m_bd[j], s_pair) + n_bd[j]
    return ys, s_bd


def _rwkv_kernel(*refs, has_vres):
    if has_vres:
        (feat_ref, prev_ref, z_ref, vfirst_ref, mu_ref, w0_ref, w2_ref, a0_ref, a2_ref, kk_ref, ka_ref,
         rk_ref, lnw_ref, lnb_ref, hsum_ref, v0_ref, v1_ref, v2_ref, y_ref, state) = refs
    else:
        (feat_ref, prev_ref, z_ref, mu_ref, w0_ref, w2_ref, a0_ref, a2_ref, kk_ref, ka_ref,
         rk_ref, lnw_ref, lnb_ref, hsum_ref, y_ref, vout_ref, state) = refs
    c = pl.program_id(1)
    C = RWKV_CHUNK
    W = RWKV_WIDTH
    n_tok = feat_ref.shape[0]

    @pl.when(c == 0)
    def _():
        state[...] = jnp.zeros(state.shape, F32)

    def lora(x, w_ref):
        return _dot3(_split(x), _split(w_ref[...]))

    hsum = hsum_ref[...]

    def head_sum(x):
        return _dot_exact_rhs(x, hsum)

    feat = feat_ref[...]
    row = lax.broadcasted_iota(jnp.int32, feat.shape, 0)
    last_prev = jnp.where(c == 0, 0.0, prev_ref[7:8, :])
    prev = jnp.where(row == 0, last_prev, pltpu.roll(feat, 1, 0))
    xs = feat + (prev - feat) * mu_ref[...]
    r = xs[:, :W]
    k = xs[:, W:2 * W]
    v = xs[:, 2 * W:3 * W]
    wd = xs[:, 3 * W:3 * W + DECAY_LORA]
    ad = xs[:, 3 * W + DECAY_LORA:]

    pre = -(w0_ref[...] + lora(jnp.tanh(wd), w2_ref))
    softplus = jnp.maximum(pre, 0.0) + jnp.log(1.0 + jnp.exp(-jnp.abs(pre)))
    logw = -jnp.exp(-softplus - 0.5)
    if has_vres:
        mix = jax.nn.sigmoid(v0_ref[...] + lora(lora(v, v1_ref), v2_ref))
        v = v + (vfirst_ref[...] - v) * mix
    else:
        vout_ref[...] = v
    alpha = jax.nn.sigmoid(a0_ref[...] + lora(ad, a2_ref))
    kk = k * kk_ref[...]
    kk = kk / jnp.maximum(jnp.sqrt(head_sum(kk * kk)), 1e-12)
    k = k * (1.0 + (alpha - 1.0) * ka_ref[...])

    ti = lax.broadcasted_iota(jnp.int32, (n_tok, n_tok), 0)
    tj = lax.broadcasted_iota(jnp.int32, (n_tok, n_tok), 1)
    tri = jnp.where((ti >= tj) & (ti // C == tj // C), 1.0, 0.0).astype(BF16)
    cum = _dot_exact_lhs(tri, logw, 3)

    t_idx = lax.broadcasted_iota(jnp.int32, (C, W), 0)
    i_idx = lax.broadcasted_iota(jnp.int32, (C, W), 1) % HEAD_DIM
    lane_head = lax.broadcasted_iota(jnp.int32, (C, W), 1) // HEAD_DIM
    bi = lax.broadcasted_iota(jnp.int32, (W, W), 0)
    bj = lax.broadcasted_iota(jnp.int32, (W, W), 1)
    cst = (
        jnp.where(t_idx == i_idx, 1.0, 0.0),
        t_idx > i_idx,
        t_idx >= i_idx,
        (t_idx // RWKV_SUB) == (i_idx // RWKV_SUB),
        [jnp.where(lane_head == h, 1.0, 0.0).astype(BF16) for h in range(RWKV_HEADS)],
        (bi // HEAD_DIM) == (bj // HEAD_DIM),
        bi == bj,
    )

    a_t = -kk * jnp.exp(cum - logw)
    r_t = r * jnp.exp(cum)
    inv_decay = jnp.exp(-cum)
    b_raw = kk * alpha
    b_t = b_raw * inv_decay
    k_t = k * inv_decay

    chunks = [slice(j * C, (j + 1) * C) for j in range(n_tok // C)]
    cum_end = [cum[rows.stop - 1:rows.stop, :] for rows in chunks]
    to_end = [jnp.exp(ce - cum[rows]) for ce, rows in zip(cum_end, chunks)]
    ys, s_bd = _rwkv_chunks(
        [a_t[rows] for rows in chunks], [r_t[rows] for rows in chunks], [b_t[rows] for rows in chunks],
        [k_t[rows] for rows in chunks], [b_raw[rows] * te for rows, te in zip(chunks, to_end)],
        [k[rows] * te for rows, te in zip(chunks, to_end)], [v[rows] for rows in chunks],
        [jnp.exp(ce) for ce in cum_end], state[...], cst)
    state[...] = s_bd
    y = jnp.concatenate(ys, axis=0)

    inv_n = 1.0 / HEAD_DIM
    mean = head_sum(y) * inv_n
    cen = y - mean
    var = head_sum(cen * cen) * inv_n
    y = cen * lax.rsqrt(var + GN_EPS) * lnw_ref[...] + lnb_ref[...]
    y = y + head_sum(r * k * rk_ref[...]) * v
    z = z_ref[...]
    y_ref[...] = (y * (z * jax.nn.sigmoid(z))).astype(y_ref.dtype)


def _rwkv(feat, z, v_first, p):
    b, t, _ = feat.shape
    n_tok = RWKV_STEP_CHUNKS * RWKV_CHUNK
    assert t % n_tok == 0
    W = RWKV_WIDTH
    has_vres = v_first is not None
    hsum = (np.arange(W)[:, None] // HEAD_DIM == np.arange(W)[None, :] // HEAD_DIM).astype(np.float32)

    def tok(width):
        return pl.BlockSpec((None, n_tok, width), lambda bi, c: (bi, c, 0))

    def full(arr):
        return pl.BlockSpec(arr.shape, lambda bi, c: (0,) * arr.ndim)

    args = [feat, feat, z]
    specs = [tok(SHIFT_WIDTH),
             pl.BlockSpec((None, 8, SHIFT_WIDTH), lambda bi, c: (bi, jnp.maximum(c * (n_tok // 8) - 1, 0), 0)),
             tok(W)]
    if has_vres:
        args.append(v_first)
        specs.append(tok(W))
    names = ["mu", "w0", "w2", "a0", "a2", "k_k", "k_a", "r_k", "ln_w", "ln_b"]
    consts = [p[n] for n in names] + [jnp.asarray(hsum, BF16)]
    if has_vres:
        consts += [p["v0"], p["v1"], p["v2"]]
    args += consts
    specs += [full(a) for a in consts]
    y_shape = jax.ShapeDtypeStruct((b, t, W), BF16)
    if has_vres:
        out_shape, out_specs = y_shape, tok(W)
    else:
        out_shape = (y_shape, jax.ShapeDtypeStruct((b, t, W), F32))
        out_specs = (tok(W), tok(W))
    res = pl.pallas_call(
        functools.partial(_rwkv_kernel, has_vres=has_vres),
        grid=(b, t // n_tok),
        in_specs=specs,
        out_specs=out_specs,
        out_shape=out_shape,
        scratch_shapes=[pltpu.VMEM((W, W), F32)],
        compiler_params=_params(("parallel", "arbitrary")),
    )(*args)
    if has_vres:
        return res, v_first
    return res[0], res[1]


def _merge_kernel(x_ref, ya_ref, yb_ref, yc_ref, mg_ref, bm_ref, pa_ref, pb_ref, pc_ref, wo_ref, fg_ref,
                  o_ref, *, final_norm):
    d = D_MODEL
    mixed = None
    for j, (y_ref, p_ref) in enumerate(((ya_ref, pa_ref), (yb_ref, pb_ref), (yc_ref, pc_ref))):
        gate = jax.nn.sigmoid(mg_ref[:, j * d:(j + 1) * d] + bm_ref[j:j + 1, :])
        term = gate * jnp.dot(y_ref[...], p_ref[...], preferred_element_type=F32)
        mixed = term if mixed is None else mixed + term
    x = x_ref[...] + jnp.dot(mixed.astype(BF16), wo_ref[...], preferred_element_type=F32)
    if final_norm:
        ms = jnp.mean(x * x, axis=-1, keepdims=True)
        x = x * lax.rsqrt(ms + NORM_EPS) * fg_ref[...]
    o_ref[...] = x


def _merge(x2d, ya, yb, yc, mg, bm, pa, pb, pc, wo, fg, final_norm):
    m = x2d.shape[0]
    tm = ROW_TILE

    def rows(width):
        return pl.BlockSpec((tm, width), lambda i: (i, 0))

    def full(arr):
        return pl.BlockSpec(arr.shape, lambda i: (0, 0))

    return pl.pallas_call(
        functools.partial(_merge_kernel, final_norm=final_norm),
        grid=(m // tm,),
        in_specs=[rows(D_MODEL), rows(512), rows(256), rows(256), rows(3 * D_MODEL),
                  full(bm), full(pa), full(pb), full(pc), full(wo), full(fg)],
        out_specs=rows(D_MODEL),
        out_shape=jax.ShapeDtypeStruct((m, D_MODEL), F32),
        compiler_params=_params(("parallel",)),
    )(x2d, ya, yb, yc, mg, bm, pa, pb, pc, wo, fg)


def kernel(x, norm_g, w_in, b_merge, cmp_pe_k, cmp_w1_k, cmp_w2_k, cmp_pe_v, cmp_w1_v, cmp_w2_v, swa_sinks,
           rwkv_mu, rwkv_w0, rwkv_w2, rwkv_a0, rwkv_a2, rwkv_k_k, rwkv_k_a, rwkv_r_k, rwkv_ln_w, rwkv_ln_b,
           rwkv_v0, rwkv_v1, rwkv_v2, proj_a, proj_b, proj_c, w_out, final_g):
    b, t, d = x.shape
    depth = w_in.shape[0]
    m = b * t
    x2d = x.reshape(m, d)
    col_idx = jnp.asarray(_COL_IDX)
    v_first = None
    for l in range(depth):
        w_pad = jnp.concatenate([w_in[l], jnp.zeros((d, 1), w_in.dtype)], axis=1)
        w = jnp.take(w_pad, col_idx, axis=1).astype(BF16)
        outs = _in_proj(x2d, norm_g[l].reshape(1, d), w)
        seg = {name: o.reshape(b, t, o.shape[-1]) for (name, _, _), o in zip(_SEGS, outs)}

        pe = jnp.stack([cmp_pe_k[l], cmp_pe_v[l]]).reshape(2, 1, CMP_BLOCK * HEAD_DIM)
        kcvc = _compress(seg["a_kv_cmp"], pe, jnp.stack([cmp_w1_k[l], cmp_w1_v[l]]),
                         jnp.stack([cmp_w2_k[l], cmp_w2_v[l]]))
        y_a, y_b = _attention(seg["a_q"], seg["a_z"], seg["a_gate"], seg["a_kv_slc"], seg["a_kv_win"], kcvc,
                              seg["b_q"], seg["b_z"], seg["b_kv"], swa_sinks[l])

        row = lambda a: a.reshape(1, -1)
        p = {"mu": row(rwkv_mu[l]), "w0": row(rwkv_w0[l]), "w2": rwkv_w2[l], "a0": row(rwkv_a0[l]),
             "a2": rwkv_a2[l], "k_k": row(rwkv_k_k[l]), "k_a": row(rwkv_k_a[l]), "r_k": row(rwkv_r_k[l]),
             "ln_w": row(rwkv_ln_w[l]), "ln_b": row(rwkv_ln_b[l])}
        if l > 0:
            p.update(v0=row(rwkv_v0[l - 1]), v1=rwkv_v1[l - 1], v2=rwkv_v2[l - 1])
        y_c, v_first = _rwkv(seg["c_shift"], seg["c_z"], v_first, p)

        x2d = _merge(x2d, y_a.reshape(m, -1), y_b.reshape(m, -1), y_c.reshape(m, -1),
                     seg["merge"].reshape(m, -1), b_merge[l], proj_a[l].astype(BF16), proj_b[l].astype(BF16),
                     proj_c[l].astype(BF16), w_out[l].astype(BF16), final_g.reshape(1, d),
                     final_norm=(l == depth - 1))
    return x2d.reshape(b, t, d)
```

```python
import functools

import numpy as np
import jax
import jax.numpy as jnp
from jax import lax
from jax.experimental import pallas as pl
from jax.experimental.pallas import tpu as pltpu

F32 = jnp.float32
BF16 = jnp.bfloat16

D_MODEL = 1024
HEAD_DIM = 64
NSA_HEADS = 8
NSA_GROUPS = 2
NSA_HG = NSA_HEADS // NSA_GROUPS
CMP_STRIDE = 16
CMP_BLOCK = 32
CMP_HIDDEN = 128
SLC_BLOCK = 64
SLC_TOPN = 16
NSA_WINDOW = 512
SWA_HEADS = 4
SWA_GROUPS = 2
SWA_HG = SWA_HEADS // SWA_GROUPS
SWA_WINDOW = 128
RWKV_HEADS = 4
RWKV_WIDTH = RWKV_HEADS * HEAD_DIM
DECAY_LORA = 64
ICLR_LORA = 64
VRES_LORA = 32
SHIFT_WIDTH = 3 * RWKV_WIDTH + DECAY_LORA + ICLR_LORA
NORM_EPS = 1e-6
GN_EPS = 64e-5
NEG_INF = -1e30
FORCE = 1e9

_REF_SEGMENTS = (
    ("a_q", 512), ("a_kv_cmp", 256), ("a_kv_slc", 256), ("a_kv_win", 256), ("a_gate", 24),
    ("a_z", 512), ("b_q", 256), ("b_kv", 256), ("b_z", 256), ("c_shift", SHIFT_WIDTH),
    ("c_z", 256), ("merge", 3 * D_MODEL),
)
N_IN = sum(w for _, w in _REF_SEGMENTS)

VMEM_LIMIT = 56 * 1024 * 1024
ROW_TILE = 256
ATT_TILE = 256
ATT_ROWS = 128
RWKV_CHUNK = 64
RWKV_STEP_CHUNKS = 8
RWKV_GROUP_CHUNKS = 2
RWKV_STAGGER = 4
RWKV_SUB = 16


def _ref_offsets():
    out, off = {}, 0
    for name, width in _REF_SEGMENTS:
        out[name] = off
        off += width
    return out


def _kv_interleave(base):
    idx = []
    for g in range(2):
        idx += list(range(base + 128 + g * 64, base + 128 + (g + 1) * 64))
        idx += list(range(base + g * 64, base + (g + 1) * 64))
    return idx


def _projection_layout():
    o = _ref_offsets()
    segs, idx = [], []

    def add(name, cols, dtype):
        segs.append((name, len(cols), dtype))
        idx.extend(cols)

    add("a_q", list(range(o["a_q"], o["a_q"] + 512)), BF16)
    add("a_z", list(range(o["a_z"], o["a_z"] + 512)), F32)
    add("a_kv_cmp", list(range(o["a_kv_cmp"], o["a_kv_cmp"] + 256)), BF16)
    add("a_kv_slc", _kv_interleave(o["a_kv_slc"]), BF16)
    add("a_kv_win", _kv_interleave(o["a_kv_win"]), BF16)
    gate = []
    for g in range(NSA_GROUPS):
        cols = [N_IN] * 128
        for br in range(3):
            for h in range(NSA_HG):
                cols[br * NSA_HG + h] = o["a_gate"] + br * NSA_HEADS + g * NSA_HG + h
        gate += cols
    add("a_gate", gate, F32)
    add("b_q", list(range(o["b_q"], o["b_q"] + 256)), BF16)
    add("b_kv", _kv_interleave(o["b_kv"]), BF16)
    add("b_z", list(range(o["b_z"], o["b_z"] + 256)), F32)
    add("c_shift", list(range(o["c_shift"], o["c_shift"] + SHIFT_WIDTH)), F32)
    add("c_z", list(range(o["c_z"], o["c_z"] + 256)), F32)
    add("merge", list(range(o["merge"], o["merge"] + 3 * D_MODEL)), F32)
    return tuple(segs), np.asarray(idx, np.int32)


_SEGS, _COL_IDX = _projection_layout()
_N_PROJ = int(_COL_IDX.shape[0])


def _column_runs(idx):
    runs, i = [], 0
    while i < len(idx):
        j = i + 1
        while j < len(idx) and (idx[j] == idx[j - 1] + 1 if idx[i] != N_IN else idx[j] == N_IN):
            j += 1
        runs.append((int(idx[i]), j - i))
        i = j
    return runs


_COL_RUNS = _column_runs(_COL_IDX)


def _params(sem):
    return pltpu.CompilerParams(dimension_semantics=sem, vmem_limit_bytes=VMEM_LIMIT)


def _in_proj_kernel(x_ref, g_ref, w_ref, *out_refs):
    x = x_ref[...]
    ms = jnp.mean(x * x, axis=-1, keepdims=True)
    xn = (x * lax.rsqrt(ms + NORM_EPS) * g_ref[...]).astype(BF16)
    off = 0
    for o_ref, (_, width, _) in zip(out_refs, _SEGS):
        for c0 in range(0, width, 512):
            cw = min(512, width - c0)
            o_ref[:, c0:c0 + cw] = jnp.dot(
                xn, w_ref[:, off + c0:off + c0 + cw], preferred_element_type=F32).astype(o_ref.dtype)
        off += width


def _in_proj(x2d, g, w):
    m = x2d.shape[0]
    tm = ROW_TILE
    out_shape = [jax.ShapeDtypeStruct((m, width), dt) for _, width, dt in _SEGS]
    out_specs = [pl.BlockSpec((tm, width), lambda i: (i, 0)) for _, width, _ in _SEGS]
    return pl.pallas_call(
        _in_proj_kernel,
        grid=(m // tm,),
        in_specs=[
            pl.BlockSpec((tm, D_MODEL), lambda i: (i, 0)),
            pl.BlockSpec((1, D_MODEL), lambda i: (0, 0)),
            pl.BlockSpec((D_MODEL, _N_PROJ), lambda i: (0, 0), pipeline_mode=pl.Buffered(1)),
        ],
        out_specs=out_specs,
        out_shape=out_shape,
        compiler_params=_params(("parallel",)),
    )(x2d, g, w)


def _compress_kernel(z_ref, pe_ref, w1_ref, w2_ref, o_ref):
    z = z_ref[...]
    half = CMP_STRIDE * HEAD_DIM
    w1 = w1_ref[...]
    first = _dot_exact_lhs(z, w1[:half], 3)
    second = _dot_exact_lhs(z, w1[half:], 3)
    n = z.shape[0]
    hid = first + pltpu.roll(second, n - 1, 0)
    hid = hid + _dot3(_split(pe_ref[...]), _split(w1))
    act = hid * jax.nn.sigmoid(hid)
    o_ref[...] = _dot3(_split(act), _split(w2_ref[...]))


def _compress(kv_cmp, pe, w1, w2):
    b, t, _ = kv_cmp.shape
    nch = t // CMP_STRIDE
    z = kv_cmp.reshape(b, nch, CMP_STRIDE, 4, HEAD_DIM)
    z = jnp.transpose(z, (0, 3, 1, 2, 4)).reshape(b, 4, nch, CMP_STRIDE * HEAD_DIM)
    return pl.pallas_call(
        _compress_kernel,
        grid=(b, 4),
        in_specs=[
            pl.BlockSpec((None, None, nch, CMP_STRIDE * HEAD_DIM), lambda i, j: (i, j, 0, 0)),
            pl.BlockSpec((None, 1, CMP_BLOCK * HEAD_DIM), lambda i, j: (j // 2, 0, 0)),
            pl.BlockSpec((None, CMP_BLOCK * HEAD_DIM, CMP_HIDDEN), lambda i, j: (j // 2, 0, 0)),
            pl.BlockSpec((None, CMP_HIDDEN, HEAD_DIM), lambda i, j: (j // 2, 0, 0)),
        ],
        out_specs=pl.BlockSpec((None, None, nch, HEAD_DIM), lambda i, j: (i, j, 0, 0)),
        out_shape=jax.ShapeDtypeStruct((b, 4, nch, HEAD_DIM), F32),
        compiler_params=_params(("parallel", "parallel")),
    )(z, pe, w1, w2)


def _widen_queries(q, n_heads):
    zeros = jnp.zeros((q.shape[0], HEAD_DIM), q.dtype)
    return [jnp.concatenate([zeros, q[:, h * HEAD_DIM:(h + 1) * HEAD_DIM]], axis=1) for h in range(n_heads)]


def _pair_values(kv):
    swapped = jnp.concatenate([kv[:, HEAD_DIM:], kv[:, :HEAD_DIM]], axis=1)
    lane = lax.broadcasted_iota(jnp.int32, kv.shape, 1)
    low = jnp.where(lane < HEAD_DIM, 1.0, 0.0).astype(kv.dtype)
    high = jnp.where(lane < HEAD_DIM, 0.0, 1.0).astype(kv.dtype)
    even = jnp.concatenate([kv * low, low], axis=1)
    odd = jnp.concatenate([swapped * high, high], axis=1)
    return even, odd


def _pair_lanes(even, odd):
    low = lax.broadcasted_iota(jnp.int32, even.shape, 1) < HEAD_DIM
    return jnp.where(low, even, odd)


def _nt(a, b):
    return lax.dot_general(a, b, (((1,), (1,)), ((), ())), preferred_element_type=F32)


def _interleave(*streams, stagger=0):
    results = [None] * len(streams)
    waiting = list(enumerate(streams))
    live = {}
    rounds = 0
    while waiting or live:
        while waiting and waiting[0][0] * stagger <= rounds:
            idx, stream = waiting.pop(0)
            live[idx] = stream
        for idx in list(live):
            try:
                next(live[idx])
            except StopIteration as stop:
                results[idx] = stop.value
                del live[idx]
        rounds += 1
    return results


def _band_pairs(blocks, keys, val_even, val_odd, sinks=None):
    s = [[_nt(qa, keys) + bias for qa in (qa_even, qa_odd)] for qa_even, qa_odd, bias in blocks]
    yield
    m = [[jnp.max(x, axis=-1, keepdims=True) for x in pair] for pair in s]
    if sinks is not None:
        m = [[jnp.maximum(x, sinks[e]) for e, x in enumerate(pair)] for pair in m]
    yield
    p = [[jnp.exp(x - mx).astype(BF16) for x, mx in zip(xs, ms)] for xs, ms in zip(s, m)]
    yield
    outs = []
    for ps, ms in zip(p, m):
        acc = (jnp.dot(ps[0], val_even, preferred_element_type=F32)
               + jnp.dot(ps[1], val_odd, preferred_element_type=F32))
        denom = acc[:, 2 * HEAD_DIM:]
        if sinks is not None:
            shape = (acc.shape[0], 2 * HEAD_DIM)
            denom = denom + _pair_lanes(jnp.broadcast_to(jnp.exp(sinks[0] - ms[0]), shape),
                                        jnp.broadcast_to(jnp.exp(sinks[1] - ms[1]), shape))
        outs.append(acc[:, :2 * HEAD_DIM] / denom)
        yield
    return outs


def _band_bias_table(deltas, rows, width, window):
    r = np.arange(rows)[:, None]
    c = np.arange(width)[None, :]
    out = [np.where((c <= r + d) & (c > r + d - window), 0.0, NEG_INF) for d in deltas]
    return np.stack(out).astype(np.float32)


def _attention_kernel(sink_ref, q_ref, z_ref, gate_ref, kvs_ref, kvw_ref, kc_ref, vc_ref, bq_ref, bz_ref,
                      bkv_ref, negexp_ref, ovl_ref, wbias_ref, cbias_ref, gexp_ref, bbias_ref,
                      o_ref, ob_ref, srhs_sc, spe_sc, spo_sc, wpe_sc, wpo_sc, bpe_sc, bpo_sc, m_sc, acc_sc):
    g = pl.program_id(1)
    i = pl.program_id(2)
    tq = ATT_TILE
    rb = ATT_ROWS
    n_rb = tq // rb
    n_cmp_pad = kc_ref.shape[0]
    n_slc = ovl_ref.shape[0]
    n_pairs = NSA_HG // 2

    @pl.when(i == 0)
    def _():
        kvs = kvs_ref[...]
        srhs_sc[...] = jnp.concatenate([kvs, negexp_ref[...]], axis=1)
        spe_sc[...], spo_sc[...] = _pair_values(kvs)
        wpe_sc[...], wpo_sc[...] = _pair_values(kvw_ref[...])
        bpe_sc[...], bpo_sc[...] = _pair_values(bkv_ref[...])

    scale = jnp.asarray(HEAD_DIM ** -0.5, BF16)
    qas = _widen_queries(q_ref[...] * scale, NSA_HG)

    def window_stream():
        width = NSA_WINDOW + tq
        k0 = pl.multiple_of(jnp.maximum(i * tq - NSA_WINDOW, 0), tq)
        variant = jnp.minimum(i, wbias_ref.shape[0] - 1)
        blocks = [(qas[2 * pair][r0:r0 + rb], qas[2 * pair + 1][r0:r0 + rb],
                   wbias_ref[variant, r0:r0 + rb, :]) for pair in range(n_pairs) for r0 in range(0, tq, rb)]
        outs = yield from _band_pairs(blocks, kvw_ref[pl.ds(k0, width), :], wpe_sc[pl.ds(k0, width), :],
                                      wpo_sc[pl.ds(k0, width), :])
        return jnp.concatenate([jnp.concatenate(outs[pair * n_rb:(pair + 1) * n_rb], axis=0)
                                for pair in range(n_pairs)], axis=1)

    def sink_window_stream():
        qbs = _widen_queries(bq_ref[...] * scale, SWA_HG)
        width = SWA_WINDOW + tq
        k0 = pl.multiple_of(jnp.maximum(i * tq - SWA_WINDOW, 0), SWA_WINDOW)
        variant = jnp.minimum(i, bbias_ref.shape[0] - 1)
        sinks = (sink_ref[g * SWA_HG], sink_ref[g * SWA_HG + 1])
        blocks = [(qbs[0][r0:r0 + rb], qbs[1][r0:r0 + rb], bbias_ref[variant, r0:r0 + rb, :])
                  for r0 in range(0, tq, rb)]
        outs = yield from _band_pairs(blocks, bkv_ref[pl.ds(k0, width), :], bpe_sc[pl.ds(k0, width), :],
                                      bpo_sc[pl.ds(k0, width), :], sinks)
        y = jnp.concatenate(outs, axis=0)
        zb = bz_ref[...]
        ob_ref[...] = (y * (zb * jax.nn.sigmoid(zb))).astype(ob_ref.dtype)

    def compressed_stream():
        zeros_k = jnp.zeros((n_cmp_pad, HEAD_DIM), BF16)
        kc_terms = [jnp.concatenate([zeros_k, t], axis=1) for t in _split(kc_ref[...])]
        vc = vc_ref[...].astype(BF16)
        vc_even = jnp.concatenate([vc, zeros_k], axis=1)
        vc_odd = jnp.concatenate([zeros_k, vc], axis=1)
        tpos = i * tq + lax.broadcasted_iota(jnp.int32, (tq, n_cmp_pad), 0)
        cmp_end = lax.broadcasted_iota(jnp.int32, (tq, n_cmp_pad), 1) * CMP_STRIDE + (CMP_BLOCK - 1)
        valid_c = cmp_end <= tpos
        s = [_nt(qa, kc_terms[0]) + _nt(qa, kc_terms[1]) for qa in qas]
        yield
        s = [jnp.where(valid_c, x, NEG_INF) for x in s]
        m = [jnp.max(x, axis=-1, keepdims=True) for x in s]
        yield
        p = [jnp.where(valid_c, jnp.exp(x - mx), 0.0) for x, mx in zip(s, m)]
        denom = [jnp.sum(x, axis=-1, keepdims=True) for x in p]
        yield
        p = [x * (1.0 / jnp.where(d > 0, d, 1.0)) for x, d in zip(p, denom)]
        p_sum = (p[0] + p[1]) + (p[2] + p[3])
        pb = [x.astype(BF16) for x in p]
        o_cmp = jnp.concatenate(
            [jnp.dot(pb[2 * pair], vc_even, preferred_element_type=F32)
             + jnp.dot(pb[2 * pair + 1], vc_odd, preferred_element_type=F32) for pair in range(n_pairs)], axis=1)
        yield
        imp = None
        rest = p_sum
        for _ in range(3):
            term = rest.astype(BF16)
            rest = rest - term.astype(F32)
            part = _nt(ovl_ref[...], term)
            imp = part if imp is None else imp + part
        yield
        blk = lax.broadcasted_iota(jnp.int32, (n_slc, tq), 0)
        cur = (i * tq + lax.broadcasted_iota(jnp.int32, (n_slc, tq), 1)) // SLC_BLOCK
        forced = (blk == 0) | (blk == cur) | (blk == cur - 1)
        score = jnp.where(forced, FORCE, jnp.where(blk <= cur, imp, -FORCE))
        ranks = [jnp.zeros((n_slc, tq), F32) for _ in range(4)]
        for r in range(n_slc):
            row = score[r:r + 1, :]
            ahead = (row > score) | ((row == score) & (r < blk))
            ranks[r % 4] = ranks[r % 4] + jnp.where(ahead, 1.0, 0.0)
            if r % 4 == 3:
                yield
        rank = (ranks[0] + ranks[1]) + (ranks[2] + ranks[3])
        n_sel = min(SLC_TOPN, n_slc)
        unsel_t = jnp.where((rank < n_sel) & (blk <= cur), 0.0, 1.0)
        unsel_t = jnp.concatenate([unsel_t, jnp.zeros((128 - n_slc, tq), F32)], axis=0)
        return o_cmp, jnp.transpose(unsel_t).astype(BF16)

    o_win, _, (o_cmp, unsel) = _interleave(window_stream(), sink_window_stream(), compressed_stream())

    lhs = [jnp.concatenate([qa, unsel], axis=1) for qa in qas]
    m_sc[...] = jnp.full(m_sc.shape, NEG_INF, F32)
    acc_sc[...] = jnp.zeros(acc_sc.shape, F32)
    subs = [(h, r0) for h in range(NSA_HG) for r0 in range(0, tq, rb)]

    def slc_keys(start, width, bias_lanes):
        rhs = srhs_sc[pl.ds(start, width), :]
        val_even = spe_sc[pl.ds(start, width), :]
        val_odd = spo_sc[pl.ds(start, width), :]
        s = {}
        for h, r0 in subs:
            s[h, r0] = _nt(lhs[h][r0:r0 + rb], rhs)
            if bias_lanes is not None:
                s[h, r0] = s[h, r0] + cbias_ref[r0:r0 + rb, bias_lanes:bias_lanes + width]
        m_prev = {(h, r0): m_sc[h, r0:r0 + rb] for h, r0 in subs}
        m_next = {key: jnp.maximum(m_prev[key], jnp.max(s[key], axis=-1, keepdims=True)) for key in subs}
        p = {key: jnp.exp(s[key] - jnp.concatenate([m_next[key]] * (width // 128), axis=1)).astype(BF16)
             for key in subs}
        alpha = {key: jnp.exp(m_prev[key] - m_next[key]) for key in subs}
        for h, r0 in subs:
            m_sc[h, r0:r0 + rb] = m_next[h, r0]
        for pair in range(n_pairs):
            for r0 in range(0, tq, rb):
                rows = slice(r0, r0 + rb)
                decay = _pair_lanes(alpha[2 * pair, r0], alpha[2 * pair + 1, r0])
                decay = jnp.concatenate([decay, decay], axis=1)
                acc_sc[pair, rows] = (decay * acc_sc[pair, rows]
                                      + jnp.dot(p[2 * pair, r0], val_even, preferred_element_type=F32)
                                      + jnp.dot(p[2 * pair + 1, r0], val_odd, preferred_element_type=F32))

    def wide(j, carry):
        slc_keys(pl.multiple_of(j * (2 * tq), 2 * tq), 2 * tq, None)
        return carry

    lax.fori_loop(0, i // 2, wide, 0)

    @pl.when(i % 2 == 0)
    def _():
        slc_keys(pl.multiple_of(i * tq, tq), tq, tq)

    @pl.when(i % 2 == 1)
    def _():
        slc_keys(pl.multiple_of((i - 1) * tq, 2 * tq), 2 * tq, 0)

    o_slc = []
    for pair in range(n_pairs):
        acc = acc_sc[pair]
        o_slc.append(acc[:, :2 * HEAD_DIM] / acc[:, 2 * HEAD_DIM:])
    o_slc = jnp.concatenate(o_slc, axis=1)

    sg = _split(jax.nn.sigmoid(gate_ref[...]))
    gates = (jnp.dot(sg[0], gexp_ref[...], preferred_element_type=F32)
             + jnp.dot(sg[1], gexp_ref[...], preferred_element_type=F32))
    wd = NSA_HG * HEAD_DIM
    y = gates[:, :wd] * o_cmp + gates[:, wd:2 * wd] * o_slc + gates[:, 2 * wd:] * o_win
    z = z_ref[...]
    o_ref[...] = (y * (z * jax.nn.sigmoid(z))).astype(o_ref.dtype)


def _attention(q, z, gate, kv_slc, kv_win, kcvc, bq, bz, bkv, sinks):
    b, t, _ = q.shape
    tq = ATT_TILE
    assert t % (2 * tq) == 0 and t >= NSA_WINDOW + tq and NSA_WINDOW % tq == 0 and SWA_HG == 2
    assert NSA_GROUPS == SWA_GROUPS
    n_slc = t // SLC_BLOCK
    nch = t // CMP_STRIDE
    wd = NSA_HG * HEAD_DIM
    negexp = np.where(np.arange(t)[:, None] // SLC_BLOCK == np.arange(128)[None, :], -2.0 ** 100, 0.0)
    ci = np.arange(nch)[None, :] * CMP_STRIDE
    sj = np.arange(n_slc)[:, None] * SLC_BLOCK
    overlap_t = ((ci < sj + SLC_BLOCK) & (ci + CMP_BLOCK > sj)).astype(np.float32)
    wbias = _band_bias_table(range(0, NSA_WINDOW + 1, tq), tq, NSA_WINDOW + tq, NSA_WINDOW)
    cbias = np.concatenate([np.zeros((tq, tq), np.float32), _band_bias_table([0], tq, tq, t)[0]], axis=1)
    bbias = _band_bias_table([0, SWA_WINDOW], tq, SWA_WINDOW + tq, SWA_WINDOW)
    gexp = np.zeros((128, 3 * wd), np.float32)
    for br in range(3):
        for h in range(NSA_HG):
            gexp[br * NSA_HG + h, br * wd + h * HEAD_DIM:br * wd + (h + 1) * HEAD_DIM] = 1.0

    def const(arr):
        return pl.BlockSpec(arr.shape, lambda bi, g, i: (0,) * arr.ndim)

    def tile(width):
        return pl.BlockSpec((None, tq, width), lambda bi, g, i: (bi, i, g))

    def keys():
        return pl.BlockSpec((None, t, 128), lambda bi, g, i: (bi, 0, g))

    consts = [jnp.asarray(negexp, BF16), jnp.asarray(overlap_t, BF16), jnp.asarray(wbias),
              jnp.asarray(cbias), jnp.asarray(gexp, BF16), jnp.asarray(bbias)]
    return pl.pallas_call(
        _attention_kernel,
        grid=(b, NSA_GROUPS, t // tq),
        in_specs=[
            pl.BlockSpec(memory_space=pltpu.SMEM),
            tile(256), tile(256), tile(128), keys(), keys(),
            pl.BlockSpec((None, None, nch, HEAD_DIM), lambda bi, g, i: (bi, g, 0, 0)),
            pl.BlockSpec((None, None, nch, HEAD_DIM), lambda bi, g, i: (bi, 2 + g, 0, 0)),
            tile(128), tile(128), keys(),
        ] + [const(a) for a in consts],
        out_specs=(tile(256), tile(128)),
        out_shape=(jax.ShapeDtypeStruct((b, t, 512), BF16), jax.ShapeDtypeStruct((b, t, 256), BF16)),
        scratch_shapes=[pltpu.VMEM((t, 256), BF16)] * 7 + [
            pltpu.VMEM((NSA_HG, tq, 128), F32),
            pltpu.VMEM((NSA_HG // 2, tq, 256), F32),
        ],
        compiler_params=_params(("parallel", "parallel", "arbitrary")),
    )(sinks, q, z, gate, kv_slc, kv_win, kcvc, kcvc, bq, bz, bkv, *consts)


_NN = (((1,), (0,)), ((), ()))
_NT = (((1,), (1,)), ((), ()))


def _split(a):
    hi = a.astype(BF16)
    lo = (a - hi.astype(F32)).astype(BF16)
    return hi, lo


def _dot(a, b, dims=_NN):
    return lax.dot_general(a, b, dims, preferred_element_type=F32)


def _dot3(a, b, dims=_NN):
    rows = a[0].shape[0]
    both = _dot(jnp.concatenate([a[0], a[1]], axis=0), b[0], dims)
    return (both[:rows] + both[rows:]) + _dot(a[0], b[1], dims)


def _dot_exact_lhs(a, b, n_terms):
    out = None
    for _ in range(n_terms):
        term = b.astype(BF16)
        b = b - term.astype(F32)
        part = _dot(a, term)
        out = part if out is None else out + part
    return out


def _dot_exact_rhs(x, ones):
    hi, lo = _split(x)
    return _dot(hi, ones) + _dot(lo, ones)


def _block_diag(pair, head_masks):
    return tuple(jnp.concatenate([x * m for m in head_masks], axis=0) for x in pair)


def _each(fn, *lists):
    return [fn(*items) for items in zip(*lists)]


def _unit_lower_inverse(lows, eye, same_sub, bd):
    diag = _each(lambda low: jnp.where(same_sub, low, 0.0), lows)
    off = _each(lambda low, d: low - d, lows, diag)
    inv = _each(lambda d: eye + d, diag)
    power = diag
    span = 2
    while span < RWKV_SUB:
        ps = _each(_split, power)
        power = _each(lambda p: _dot3(p, bd(p)), ps)
        yield
        inv = _each(lambda iv, pw: _dot3(_split(iv), bd(_split(eye + pw))), inv, power)
        yield
        span *= 2
    inv_s = _each(_split, inv)
    cross = _each(lambda iv, o: _dot3(iv, bd(_split(o))), inv_s, off)
    yield
    total = _each(lambda cr: eye + cr, cross)
    power = cross
    span = 2
    while span < RWKV_CHUNK // RWKV_SUB:
        ps = _each(_split, power)
        power = _each(lambda p: _dot3(p, bd(p)), ps)
        yield
        total = _each(lambda tt, pw: _dot3(_split(tt), bd(_split(eye + pw))), total, power)
        yield
        span *= 2
    return _each(lambda tt, iv: _dot3(_split(tt), bd(iv)), total, inv_s)


def _rwkv_chunk_maps(a_t, r_t, b_t, k_t, b_e, k_e, v, p_end, cst):
    C = RWKV_CHUNK
    eye, strict, incl, same_sub, head_masks = cst
    bd = lambda pair: _block_diag(pair, head_masks)

    ar = _each(lambda a, r: _split(jnp.concatenate([a, r], axis=0)), a_t, r_t)
    g_b = _each(lambda x, b: _dot3(x, bd(_split(b)), _NT), ar, b_t)
    g_k = _each(lambda x, k: _dot3(x, bd(_split(k)), _NT), ar, k_t)
    yield
    a_ab = _each(lambda g: jnp.where(strict, g[:C], 0.0), g_b)
    a_rb = _each(lambda g: _split(jnp.where(incl, g[C:], 0.0)), g_b)
    a_ak = _each(lambda g: _split(jnp.where(strict, g[:C], 0.0)), g_k)
    a_rk = _each(lambda g: _split(jnp.where(incl, g[C:], 0.0)), g_k)
    v_bd = _each(lambda x: bd(_split(x)), v)
    akv = _each(_dot3, a_ak, v_bd)
    yield
    t_inv = yield from _unit_lower_inverse(a_ab, eye, same_sub, bd)
    t_inv = _each(_split, t_inv)
    yield
    w_m = _each(lambda t, a: _dot3(t, bd(_split(a))), t_inv, a_t)
    u0 = _each(lambda t, x: _dot3(t, bd(_split(x))), t_inv, akv)
    yield
    r_m = _each(lambda r, g, w: r + _dot3(g, bd(_split(w))), r_t, a_rb, w_m)
    y0 = _each(lambda g, u, gk, vb: _dot3(g, bd(_split(u))) + _dot3(gk, vb), a_rb, u0, a_rk, v_bd)
    yield
    bk = _each(lambda b, k: _split(jnp.transpose(jnp.concatenate([b, k], axis=0))), b_e, k_e)
    m_add = _each(lambda x, w: _dot3((x[0][:, :C], x[1][:, :C]), _split(w)), bk, w_m)
    n_all = _each(lambda x, u, vv: _dot3(x, _split(jnp.concatenate([u, vv], axis=0))), bk, u0, v)
    yield
    keep = [m.astype(F32) for m in head_masks]

    def own_blocks(full):
        parts = [full[h * HEAD_DIM:(h + 1) * HEAD_DIM, :] * keep[h] for h in range(RWKV_HEADS)]
        return (parts[0] + parts[1]) + (parts[2] + parts[3])

    m_l = _each(lambda m, p: own_blocks(m) + eye * p, m_add, p_end)
    n_l = _each(own_blocks, n_all)
    lhs = _each(lambda r, m: _split(jnp.concatenate([r, m], axis=0)), r_m, m_l)
    return list(zip(lhs, y0, n_l))


def _rwkv_kernel(*refs, has_vres):
    if has_vres:
        (feat_ref, prev_ref, z_ref, vfirst_ref, mu_ref, w0_ref, w2_ref, a0_ref, a2_ref, kk_ref, ka_ref,
         rk_ref, lnw_ref, lnb_ref, hsum_ref, v0_ref, v1_ref, v2_ref, y_ref, state) = refs
    else:
        (feat_ref, prev_ref, z_ref, mu_ref, w0_ref, w2_ref, a0_ref, a2_ref, kk_ref, ka_ref,
         rk_ref, lnw_ref, lnb_ref, hsum_ref, y_ref, vout_ref, state) = refs
    c = pl.program_id(1)
    C = RWKV_CHUNK
    W = RWKV_WIDTH
    n_tok = feat_ref.shape[0]
    group = RWKV_GROUP_CHUNKS * C

    @pl.when(c == 0)
    def _():
        state[...] = jnp.zeros(state.shape, F32)

    def lora(x, w_ref):
        return _dot3(_split(x), _split(w_ref[...]))

    hsum = hsum_ref[...]

    def head_sum(x):
        return _dot_exact_rhs(x, hsum)

    feat = feat_ref[...]
    row = lax.broadcasted_iota(jnp.int32, feat.shape, 0)
    last_prev = jnp.where(c == 0, 0.0, prev_ref[7:8, :])
    prev = jnp.where(row == 0, last_prev, pltpu.roll(feat, 1, 0))
    xs = feat + (prev - feat) * mu_ref[...]

    ti = lax.broadcasted_iota(jnp.int32, (group, group), 0)
    tj = lax.broadcasted_iota(jnp.int32, (group, group), 1)
    tri = jnp.where((ti >= tj) & (ti // C == tj // C), 1.0, 0.0).astype(BF16)
    t_idx = lax.broadcasted_iota(jnp.int32, (C, W), 0)
    i_idx = lax.broadcasted_iota(jnp.int32, (C, W), 1) % HEAD_DIM
    lane_head = lax.broadcasted_iota(jnp.int32, (C, W), 1) // HEAD_DIM
    cst = (
        jnp.where(t_idx == i_idx, 1.0, 0.0),
        t_idx > i_idx,
        t_idx >= i_idx,
        (t_idx // RWKV_SUB) == (i_idx // RWKV_SUB),
        [jnp.where(lane_head == h, 1.0, 0.0).astype(BF16) for h in range(RWKV_HEADS)],
    )
    carry = {"state": state[...], "turn": 0}

    def token_group(idx):
        rows = slice(idx * group, (idx + 1) * group)
        x = xs[rows]
        r = x[:, :W]
        k = x[:, W:2 * W]
        v = x[:, 2 * W:3 * W]
        wd = x[:, 3 * W:3 * W + DECAY_LORA]
        ad = x[:, 3 * W + DECAY_LORA:]
        pre = -(w0_ref[...] + lora(jnp.tanh(wd), w2_ref))
        softplus = jnp.maximum(pre, 0.0) + jnp.log(1.0 + jnp.exp(-jnp.abs(pre)))
        logw = -jnp.exp(-softplus - 0.5)
        yield
        if has_vres:
            mix = jax.nn.sigmoid(v0_ref[...] + lora(lora(v, v1_ref), v2_ref))
            v = v + (vfirst_ref[rows, :] - v) * mix
        else:
            vout_ref[rows, :] = v
        alpha = jax.nn.sigmoid(a0_ref[...] + lora(ad, a2_ref))
        kk = k * kk_ref[...]
        kk = kk / jnp.maximum(jnp.sqrt(head_sum(kk * kk)), 1e-12)
        k = k * (1.0 + (alpha - 1.0) * ka_ref[...])
        yield
        cum = _dot_exact_lhs(tri, logw, 3)
        yield
        a_t = -kk * jnp.exp(cum - logw)
        r_t = r * jnp.exp(cum)
        inv_decay = jnp.exp(-cum)
        b_raw = kk * alpha
        b_t = b_raw * inv_decay
        k_t = k * inv_decay
        chunks = [slice(j * C, (j + 1) * C) for j in range(RWKV_GROUP_CHUNKS)]
        cum_end = [cum[cr.stop - 1:cr.stop, :] for cr in chunks]
        to_end = [jnp.exp(ce - cum[cr]) for ce, cr in zip(cum_end, chunks)]
        yield
        maps = yield from _rwkv_chunk_maps(
            [a_t[cr] for cr in chunks], [r_t[cr] for cr in chunks], [b_t[cr] for cr in chunks],
            [k_t[cr] for cr in chunks], [b_raw[cr] * te for cr, te in zip(chunks, to_end)],
            [k[cr] * te for cr, te in zip(chunks, to_end)], [v[cr] for cr in chunks],
            [jnp.exp(ce) for ce in cum_end], cst)
        while carry["turn"] != idx:
            yield
        s_l = carry["state"]
        ys = []
        for lhs, y0, n_l in maps:
            out = _dot3(lhs, _block_diag(_split(s_l), cst[4]))
            ys.append(out[:C] + y0)
            s_l = out[C:] + n_l
        carry["state"] = s_l
        carry["turn"] = idx + 1
        yield
        y = jnp.concatenate(ys, axis=0)
        inv_n = 1.0 / HEAD_DIM
        mean = head_sum(y) * inv_n
        cen = y - mean
        yield
        var = head_sum(cen * cen) * inv_n
        y = cen * lax.rsqrt(var + GN_EPS) * lnw_ref[...] + lnb_ref[...]
        y = y + head_sum(r * k * rk_ref[...]) * v
        z = z_ref[rows, :]
        y_ref[rows, :] = (y * (z * jax.nn.sigmoid(z))).astype(y_ref.dtype)

    _interleave(*[token_group(idx) for idx in range(n_tok // group)], stagger=RWKV_STAGGER)
    state[...] = carry["state"]


def _rwkv(feat, z, v_first, p):
    b, t, _ = feat.shape
    n_tok = RWKV_STEP_CHUNKS * RWKV_CHUNK
    assert t % n_tok == 0 and RWKV_CHUNK == HEAD_DIM
    W = RWKV_WIDTH
    has_vres = v_first is not None
    hsum = (np.arange(W)[:, None] // HEAD_DIM == np.arange(W)[None, :] // HEAD_DIM).astype(np.float32)

    def tok(width):
        return pl.BlockSpec((None, n_tok, width), lambda bi, c: (bi, c, 0))

    def full(arr):
        return pl.BlockSpec(arr.shape, lambda bi, c: (0,) * arr.ndim)

    args = [feat, feat, z]
    specs = [tok(SHIFT_WIDTH),
             pl.BlockSpec((None, 8, SHIFT_WIDTH), lambda bi, c: (bi, jnp.maximum(c * (n_tok // 8) - 1, 0), 0)),
             tok(W)]
    if has_vres:
        args.append(v_first)
        specs.append(tok(W))
    names = ["mu", "w0", "w2", "a0", "a2", "k_k", "k_a", "r_k", "ln_w", "ln_b"]
    consts = [p[n] for n in names] + [jnp.asarray(hsum, BF16)]
    if has_vres:
        consts += [p["v0"], p["v1"], p["v2"]]
    args += consts
    specs += [full(a) for a in consts]
    y_shape = jax.ShapeDtypeStruct((b, t, W), BF16)
    if has_vres:
        out_shape, out_specs = y_shape, tok(W)
    else:
        out_shape = (y_shape, jax.ShapeDtypeStruct((b, t, W), F32))
        out_specs = (tok(W), tok(W))
    res = pl.pallas_call(
        functools.partial(_rwkv_kernel, has_vres=has_vres),
        grid=(b, t // n_tok),
        in_specs=specs,
        out_specs=out_specs,
        out_shape=out_shape,
        scratch_shapes=[pltpu.VMEM((HEAD_DIM, W), F32)],
        compiler_params=_params(("parallel", "arbitrary")),
    )(*args)
    if has_vres:
        return res, v_first
    return res[0], res[1]


def _merge_kernel(x_ref, ya_ref, yb_ref, yc_ref, mg_ref, bm_ref, pa_ref, pb_ref, pc_ref, wo_ref, fg_ref,
                  o_ref, *, final_norm):
    d = D_MODEL
    mixed = None
    for j, (y_ref, p_ref) in enumerate(((ya_ref, pa_ref), (yb_ref, pb_ref), (yc_ref, pc_ref))):
        gate = jax.nn.sigmoid(mg_ref[:, j * d:(j + 1) * d] + bm_ref[j:j + 1, :])
        term = gate * jnp.dot(y_ref[...], p_ref[...], preferred_element_type=F32)
        mixed = term if mixed is None else mixed + term
    x = x_ref[...] + jnp.dot(mixed.astype(BF16), wo_ref[...], preferred_element_type=F32)
    if final_norm:
        ms = jnp.mean(x * x, axis=-1, keepdims=True)
        x = x * lax.rsqrt(ms + NORM_EPS) * fg_ref[...]
    o_ref[...] = x


def _merge(x2d, ya, yb, yc, mg, bm, pa, pb, pc, wo, fg, final_norm):
    m = x2d.shape[0]
    tm = ROW_TILE

    def rows(width):
        return pl.BlockSpec((tm, width), lambda i: (i, 0))

    def full(arr):
        return pl.BlockSpec(arr.shape, lambda i: (0, 0))

    return pl.pallas_call(
        functools.partial(_merge_kernel, final_norm=final_norm),
        grid=(m // tm,),
        in_specs=[rows(D_MODEL), rows(512), rows(256), rows(256), rows(3 * D_MODEL),
                  full(bm), full(pa), full(pb), full(pc), full(wo), full(fg)],
        out_specs=rows(D_MODEL),
        out_shape=jax.ShapeDtypeStruct((m, D_MODEL), F32),
        compiler_params=_params(("parallel",)),
    )(x2d, ya, yb, yc, mg, bm, pa, pb, pc, wo, fg)


def kernel(x, norm_g, w_in, b_merge, cmp_pe_k, cmp_w1_k, cmp_w2_k, cmp_pe_v, cmp_w1_v, cmp_w2_v, swa_sinks,
           rwkv_mu, rwkv_w0, rwkv_w2, rwkv_a0, rwkv_a2, rwkv_k_k, rwkv_k_a, rwkv_r_k, rwkv_ln_w, rwkv_ln_b,
           rwkv_v0, rwkv_v1, rwkv_v2, proj_a, proj_b, proj_c, w_out, final_g):
    b, t, d = x.shape
    depth = w_in.shape[0]
    m = b * t
    x2d = x.reshape(m, d)
    v_first = None
    for l in range(depth):
        w = jnp.concatenate([jnp.zeros((d, n), BF16) if start == N_IN else w_in[l][:, start:start + n].astype(BF16)
                             for start, n in _COL_RUNS], axis=1)
        outs = _in_proj(x2d, norm_g[l].reshape(1, d), w)
        seg = {name: o.reshape(b, t, o.shape[-1]) for (name, _, _), o in zip(_SEGS, outs)}

        pe = jnp.stack([cmp_pe_k[l], cmp_pe_v[l]]).reshape(2, 1, CMP_BLOCK * HEAD_DIM)
        kcvc = _compress(seg["a_kv_cmp"], pe, jnp.stack([cmp_w1_k[l], cmp_w1_v[l]]),
                         jnp.stack([cmp_w2_k[l], cmp_w2_v[l]]))
        y_a, y_b = _attention(seg["a_q"], seg["a_z"], seg["a_gate"], seg["a_kv_slc"], seg["a_kv_win"], kcvc,
                              seg["b_q"], seg["b_z"], seg["b_kv"], swa_sinks[l])

        row = lambda a: a.reshape(1, -1)
        p = {"mu": row(rwkv_mu[l]), "w0": row(rwkv_w0[l]), "w2": rwkv_w2[l], "a0": row(rwkv_a0[l]),
             "a2": rwkv_a2[l], "k_k": row(rwkv_k_k[l]), "k_a": row(rwkv_k_a[l]), "r_k": row(rwkv_r_k[l]),
             "ln_w": row(rwkv_ln_w[l]), "ln_b": row(rwkv_ln_b[l])}
        if l > 0:
            p.update(v0=row(rwkv_v0[l - 1]), v1=rwkv_v1[l - 1], v2=rwkv_v2[l - 1])
        y_c, v_first = _rwkv(seg["c_shift"], seg["c_z"], v_first, p)

        x2d = _merge(x2d, y_a.reshape(m, -1), y_b.reshape(m, -1), y_c.reshape(m, -1),
                     seg["merge"].reshape(m, -1), b_merge[l], proj_a[l].astype(BF16), proj_b[l].astype(BF16),
                     proj_c[l].astype(BF16), w_out[l].astype(BF16), final_g.reshape(1, d),
                     final_norm=(l == depth - 1))
    return x2d.reshape(b, t, d)
```

```python
import functools

import numpy as np
import jax
import jax.numpy as jnp
from jax import lax
from jax.experimental import pallas as pl
from jax.experimental.pallas import tpu as pltpu

F32 = jnp.float32
BF16 = jnp.bfloat16

D_MODEL = 1024
HEAD_DIM = 64
NSA_HEADS = 8
NSA_GROUPS = 2
NSA_HG = NSA_HEADS // NSA_GROUPS
CMP_STRIDE = 16
CMP_BLOCK = 32
CMP_HIDDEN = 128
SLC_BLOCK = 64
SLC_TOPN = 16
NSA_WINDOW = 512
SWA_HEADS = 4
SWA_GROUPS = 2
SWA_HG = SWA_HEADS // SWA_GROUPS
SWA_WINDOW = 128
RWKV_HEADS = 4
RWKV_WIDTH = RWKV_HEADS * HEAD_DIM
DECAY_LORA = 64
ICLR_LORA = 64
VRES_LORA = 32
SHIFT_WIDTH = 3 * RWKV_WIDTH + DECAY_LORA + ICLR_LORA
NORM_EPS = 1e-6
GN_EPS = 64e-5
NEG_INF = -1e30
FORCE = 1e9

_REF_SEGMENTS = (
    ("a_q", 512), ("a_kv_cmp", 256), ("a_kv_slc", 256), ("a_kv_win", 256), ("a_gate", 24),
    ("a_z", 512), ("b_q", 256), ("b_kv", 256), ("b_z", 256), ("c_shift", SHIFT_WIDTH),
    ("c_z", 256), ("merge", 3 * D_MODEL),
)
N_IN = sum(w for _, w in _REF_SEGMENTS)

VMEM_LIMIT = 56 * 1024 * 1024
ROW_TILE = 256
ATT_TILE = 256
ATT_ROWS = 128
RWKV_CHUNK = 64
RWKV_STEP_CHUNKS = 8
RWKV_GROUP_CHUNKS = 2
RWKV_STAGGER = 4
RWKV_SUB = 16


def _ref_offsets():
    out, off = {}, 0
    for name, width in _REF_SEGMENTS:
        out[name] = off
        off += width
    return out


def _kv_interleave(base):
    idx = []
    for g in range(2):
        idx += list(range(base + 128 + g * 64, base + 128 + (g + 1) * 64))
        idx += list(range(base + g * 64, base + (g + 1) * 64))
    return idx


def _projection_layout():
    o = _ref_offsets()
    segs, idx = [], []

    def add(name, cols, dtype):
        segs.append((name, len(cols), dtype))
        idx.extend(cols)

    add("a_q", list(range(o["a_q"], o["a_q"] + 512)), BF16)
    add("a_z", list(range(o["a_z"], o["a_z"] + 512)), F32)
    add("a_kv_cmp", list(range(o["a_kv_cmp"], o["a_kv_cmp"] + 256)), BF16)
    add("a_kv_slc", _kv_interleave(o["a_kv_slc"]), BF16)
    add("a_kv_win", _kv_interleave(o["a_kv_win"]), BF16)
    gate = []
    for g in range(NSA_GROUPS):
        cols = [N_IN] * 128
        for br in range(3):
            for h in range(NSA_HG):
                cols[br * NSA_HG + h] = o["a_gate"] + br * NSA_HEADS + g * NSA_HG + h
        gate += cols
    add("a_gate", gate, F32)
    add("b_q", list(range(o["b_q"], o["b_q"] + 256)), BF16)
    add("b_kv", _kv_interleave(o["b_kv"]), BF16)
    add("b_z", list(range(o["b_z"], o["b_z"] + 256)), F32)
    add("c_shift", list(range(o["c_shift"], o["c_shift"] + SHIFT_WIDTH)), F32)
    add("c_z", list(range(o["c_z"], o["c_z"] + 256)), F32)
    add("merge", list(range(o["merge"], o["merge"] + 3 * D_MODEL)), BF16)
    return tuple(segs), np.asarray(idx, np.int32)


_SEGS, _COL_IDX = _projection_layout()
_N_PROJ = int(_COL_IDX.shape[0])


def _column_runs(idx):
    runs, i = [], 0
    while i < len(idx):
        j = i + 1
        while j < len(idx) and (idx[j] == idx[j - 1] + 1 if idx[i] != N_IN else idx[j] == N_IN):
            j += 1
        runs.append((int(idx[i]), j - i))
        i = j
    return runs


_COL_RUNS = _column_runs(_COL_IDX)


def _params(sem):
    return pltpu.CompilerParams(dimension_semantics=sem, vmem_limit_bytes=VMEM_LIMIT)


def _in_proj_kernel(x_ref, g_ref, bm_ref, w_ref, *out_refs):
    x = x_ref[...]
    ms = jnp.mean(x * x, axis=-1, keepdims=True)
    xn = (x * lax.rsqrt(ms + NORM_EPS) * g_ref[...]).astype(BF16)
    off = 0
    for o_ref, (name, width, _) in zip(out_refs, _SEGS):
        for c0 in range(0, width, 512):
            cw = min(512, width - c0)
            val = jnp.dot(xn, w_ref[:, off + c0:off + c0 + cw], preferred_element_type=F32)
            if name == "merge":
                val = jax.nn.sigmoid(val + bm_ref[:, c0:c0 + cw])
            o_ref[:, c0:c0 + cw] = val.astype(o_ref.dtype)
        off += width


def _in_proj(x2d, g, bm, w):
    m = x2d.shape[0]
    tm = ROW_TILE
    out_shape = [jax.ShapeDtypeStruct((m, width), dt) for _, width, dt in _SEGS]
    out_specs = [pl.BlockSpec((tm, width), lambda i: (i, 0)) for _, width, _ in _SEGS]
    return pl.pallas_call(
        _in_proj_kernel,
        grid=(m // tm,),
        in_specs=[
            pl.BlockSpec((tm, D_MODEL), lambda i: (i, 0)),
            pl.BlockSpec((1, D_MODEL), lambda i: (0, 0)),
            pl.BlockSpec((1, 3 * D_MODEL), lambda i: (0, 0)),
            pl.BlockSpec((D_MODEL, _N_PROJ), lambda i: (0, 0), pipeline_mode=pl.Buffered(1)),
        ],
        out_specs=out_specs,
        out_shape=out_shape,
        compiler_params=_params(("parallel",)),
    )(x2d, g, bm, w)


def _compress_kernel(z_ref, pe_ref, w1_ref, w2_ref, o_ref):
    z = z_ref[...]
    half = CMP_STRIDE * HEAD_DIM
    w1 = w1_ref[...]
    first = _dot_exact_lhs(z, w1[:half], 3)
    second = _dot_exact_lhs(z, w1[half:], 3)
    n = z.shape[0]
    hid = first + pltpu.roll(second, n - 1, 0)
    hid = hid + _dot3(_split(pe_ref[...]), _split(w1))
    act = hid * jax.nn.sigmoid(hid)
    o_ref[...] = _dot3(_split(act), _split(w2_ref[...]))


def _compress(kv_cmp, pe, w1, w2):
    b, t, _ = kv_cmp.shape
    nch = t // CMP_STRIDE
    z = kv_cmp.reshape(b, nch, CMP_STRIDE, 4, HEAD_DIM)
    z = jnp.transpose(z, (0, 3, 1, 2, 4)).reshape(b, 4, nch, CMP_STRIDE * HEAD_DIM)
    return pl.pallas_call(
        _compress_kernel,
        grid=(b, 4),
        in_specs=[
            pl.BlockSpec((None, None, nch, CMP_STRIDE * HEAD_DIM), lambda i, j: (i, j, 0, 0)),
            pl.BlockSpec((None, 1, CMP_BLOCK * HEAD_DIM), lambda i, j: (j // 2, 0, 0)),
            pl.BlockSpec((None, CMP_BLOCK * HEAD_DIM, CMP_HIDDEN), lambda i, j: (j // 2, 0, 0)),
            pl.BlockSpec((None, CMP_HIDDEN, HEAD_DIM), lambda i, j: (j // 2, 0, 0)),
        ],
        out_specs=pl.BlockSpec((None, None, nch, HEAD_DIM), lambda i, j: (i, j, 0, 0)),
        out_shape=jax.ShapeDtypeStruct((b, 4, nch, HEAD_DIM), F32),
        compiler_params=_params(("parallel", "parallel")),
    )(z, pe, w1, w2)


def _widen_queries(q, n_heads):
    zeros = jnp.zeros((q.shape[0], HEAD_DIM), q.dtype)
    return [jnp.concatenate([zeros, q[:, h * HEAD_DIM:(h + 1) * HEAD_DIM]], axis=1) for h in range(n_heads)]


def _pair_values(kv):
    swapped = jnp.concatenate([kv[:, HEAD_DIM:], kv[:, :HEAD_DIM]], axis=1)
    lane = lax.broadcasted_iota(jnp.int32, kv.shape, 1)
    low = jnp.where(lane < HEAD_DIM, 1.0, 0.0).astype(kv.dtype)
    high = jnp.where(lane < HEAD_DIM, 0.0, 1.0).astype(kv.dtype)
    even = jnp.concatenate([kv * low, low], axis=1)
    odd = jnp.concatenate([swapped * high, high], axis=1)
    return even, odd


def _pair_lanes(even, odd):
    low = lax.broadcasted_iota(jnp.int32, even.shape, 1) < HEAD_DIM
    return jnp.where(low, even, odd)


def _nt(a, b):
    return lax.dot_general(a, b, (((1,), (1,)), ((), ())), preferred_element_type=F32)


def _rounds(streams, stagger=0):
    results = [None] * len(streams)
    waiting = list(enumerate(streams))
    live = {}
    rounds = 0
    while waiting or live:
        while waiting and waiting[0][0] * stagger <= rounds:
            idx, stream = waiting.pop(0)
            live[idx] = stream
        for idx in list(live):
            try:
                next(live[idx])
            except StopIteration as stop:
                results[idx] = stop.value
                del live[idx]
        rounds += 1
        yield
    return results


def _interleave(*streams, stagger=0):
    rounds = _rounds(streams, stagger)
    while True:
        try:
            next(rounds)
        except StopIteration as stop:
            return stop.value


def _band_block(qa_even, qa_odd, keys, val_even, val_odd, bias, sinks=None):
    s = [_nt(qa, keys) + bias for qa in (qa_even, qa_odd)]
    yield
    m = [jnp.max(x, axis=-1, keepdims=True) for x in s]
    if sinks is not None:
        m = [jnp.maximum(x, sinks[e]) for e, x in enumerate(m)]
    p = [jnp.exp(x - mx).astype(BF16) for x, mx in zip(s, m)]
    yield
    acc = (jnp.dot(p[0], val_even, preferred_element_type=F32)
           + jnp.dot(p[1], val_odd, preferred_element_type=F32))
    denom = acc[:, 2 * HEAD_DIM:]
    if sinks is not None:
        shape = (acc.shape[0], 2 * HEAD_DIM)
        denom = denom + _pair_lanes(jnp.broadcast_to(jnp.exp(sinks[0] - m[0]), shape),
                                    jnp.broadcast_to(jnp.exp(sinks[1] - m[1]), shape))
    return acc[:, :2 * HEAD_DIM] / denom


def _band_blocks(blocks, sinks=None):
    return (yield from _rounds([_band_block(*blk, sinks) for blk in blocks], stagger=1))


def _band_bias_table(deltas, rows, width, window):
    r = np.arange(rows)[:, None]
    c = np.arange(width)[None, :]
    out = [np.where((c <= r + d) & (c > r + d - window), 0.0, NEG_INF) for d in deltas]
    return np.stack(out).astype(np.float32)


def _attention_kernel(sink_ref, q_ref, z_ref, gate_ref, kvs_ref, kvw_ref, kc_ref, vc_ref, bq_ref, bz_ref,
                      bkv_ref, negexp_ref, ovl_ref, wbias_ref, cbias_ref, gexp_ref, bbias_ref,
                      o_ref, ob_ref, srhs_sc, spe_sc, spo_sc, wpe_sc, wpo_sc, bpe_sc, bpo_sc, m_sc, acc_sc):
    g = pl.program_id(1)
    i = pl.program_id(2)
    tq = ATT_TILE
    rb = ATT_ROWS
    n_rb = tq // rb
    n_cmp_pad = kc_ref.shape[0]
    n_slc = ovl_ref.shape[0]
    n_pairs = NSA_HG // 2

    @pl.when(i == 0)
    def _():
        kvs = kvs_ref[...]
        srhs_sc[...] = jnp.concatenate([kvs, negexp_ref[...]], axis=1)
        spe_sc[...], spo_sc[...] = _pair_values(kvs)
        wpe_sc[...], wpo_sc[...] = _pair_values(kvw_ref[...])
        bpe_sc[...], bpo_sc[...] = _pair_values(bkv_ref[...])

    scale = jnp.asarray(HEAD_DIM ** -0.5, BF16)
    qas = _widen_queries(q_ref[...] * scale, NSA_HG)

    def band_inputs(kv_ref, pe_sc, po_sc, bias_ref, window, r0):
        q0 = i * tq + r0
        k0 = pl.multiple_of(jnp.maximum(q0 - window, 0), rb)
        width = window + rb
        variant = jnp.minimum(q0 // rb, bias_ref.shape[0] - 1)
        return (kv_ref[pl.ds(k0, width), :], pe_sc[pl.ds(k0, width), :], po_sc[pl.ds(k0, width), :],
                bias_ref[variant])

    def window_stream():
        bands = [band_inputs(kvw_ref, wpe_sc, wpo_sc, wbias_ref, NSA_WINDOW, r0) for r0 in range(0, tq, rb)]
        blocks = [(qas[2 * pair][r0:r0 + rb], qas[2 * pair + 1][r0:r0 + rb]) + bands[r0 // rb]
                  for pair in range(n_pairs) for r0 in range(0, tq, rb)]
        outs = yield from _band_blocks(blocks)
        return jnp.concatenate([jnp.concatenate(outs[pair * n_rb:(pair + 1) * n_rb], axis=0)
                                for pair in range(n_pairs)], axis=1)

    def sink_window_stream():
        qbs = _widen_queries(bq_ref[...] * scale, SWA_HG)
        sinks = (sink_ref[g * SWA_HG], sink_ref[g * SWA_HG + 1])
        blocks = [(qbs[0][r0:r0 + rb], qbs[1][r0:r0 + rb])
                  + band_inputs(bkv_ref, bpe_sc, bpo_sc, bbias_ref, SWA_WINDOW, r0) for r0 in range(0, tq, rb)]
        outs = yield from _band_blocks(blocks, sinks)
        y = jnp.concatenate(outs, axis=0)
        zb = bz_ref[...]
        ob_ref[...] = (y * (zb * jax.nn.sigmoid(zb))).astype(ob_ref.dtype)

    def compressed_stream():
        zeros_k = jnp.zeros((n_cmp_pad, HEAD_DIM), BF16)
        kc_both = jnp.concatenate([jnp.concatenate([zeros_k, t], axis=1) for t in _split(kc_ref[...])], axis=0)
        vc = vc_ref[...].astype(BF16)
        vc_even = jnp.concatenate([vc, zeros_k], axis=1)
        vc_odd = jnp.concatenate([zeros_k, vc], axis=1)
        tpos = i * tq + lax.broadcasted_iota(jnp.int32, (tq, n_cmp_pad), 0)
        cmp_end = lax.broadcasted_iota(jnp.int32, (tq, n_cmp_pad), 1) * CMP_STRIDE + (CMP_BLOCK - 1)
        valid_c = cmp_end <= tpos
        s = [_nt(qa, kc_both) for qa in qas]
        s = [x[:, :n_cmp_pad] + x[:, n_cmp_pad:] for x in s]
        yield
        s = [jnp.where(valid_c, x, NEG_INF) for x in s]
        m = [jnp.max(x, axis=-1, keepdims=True) for x in s]
        yield
        p = [jnp.where(valid_c, jnp.exp(x - mx), 0.0) for x, mx in zip(s, m)]
        denom = [jnp.sum(x, axis=-1, keepdims=True) for x in p]
        yield
        p = [x * (1.0 / jnp.where(d > 0, d, 1.0)) for x, d in zip(p, denom)]
        p_sum = (p[0] + p[1]) + (p[2] + p[3])
        pb = [x.astype(BF16) for x in p]
        o_cmp = jnp.concatenate(
            [jnp.dot(pb[2 * pair], vc_even, preferred_element_type=F32)
             + jnp.dot(pb[2 * pair + 1], vc_odd, preferred_element_type=F32) for pair in range(n_pairs)], axis=1)
        yield
        imp = None
        rest = p_sum
        for _ in range(3):
            term = rest.astype(BF16)
            rest = rest - term.astype(F32)
            part = _nt(ovl_ref[...], term)
            imp = part if imp is None else imp + part
        yield
        blk = lax.broadcasted_iota(jnp.int32, (n_slc, tq), 0)
        cur = (i * tq + lax.broadcasted_iota(jnp.int32, (n_slc, tq), 1)) // SLC_BLOCK
        forced = (blk == 0) | (blk == cur) | (blk == cur - 1)
        score = jnp.where(forced, FORCE, jnp.where(blk <= cur, imp, -FORCE))
        ranks = [jnp.zeros((n_slc, tq), F32) for _ in range(4)]
        for r in range(n_slc):
            row = score[r:r + 1, :]
            ahead = (row > score) | ((row == score) & (r < blk))
            ranks[r % 4] = ranks[r % 4] + jnp.where(ahead, 1.0, 0.0)
            if r % 4 == 3:
                yield
        rank = (ranks[0] + ranks[1]) + (ranks[2] + ranks[3])
        n_sel = min(SLC_TOPN, n_slc)
        unsel_t = jnp.where((rank < n_sel) & (blk <= cur), 0.0, 1.0)
        unsel_t = jnp.concatenate([unsel_t, jnp.zeros((128 - n_slc, tq), F32)], axis=0)
        return o_cmp, jnp.transpose(unsel_t).astype(BF16)

    o_win, _, (o_cmp, unsel) = _interleave(window_stream(), sink_window_stream(), compressed_stream())

    lhs = [jnp.concatenate([qa, unsel], axis=1) for qa in qas]
    m_sc[...] = jnp.full(m_sc.shape, NEG_INF, F32)
    acc_sc[...] = jnp.zeros(acc_sc.shape, F32)
    subs = [(h, r0) for h in range(NSA_HG) for r0 in range(0, tq, rb)]

    def slc_keys(start, width, bias_lanes):
        rhs = srhs_sc[pl.ds(start, width), :]
        val_even = spe_sc[pl.ds(start, width), :]
        val_odd = spo_sc[pl.ds(start, width), :]
        s = {}
        for h, r0 in subs:
            s[h, r0] = _nt(lhs[h][r0:r0 + rb], rhs)
            if bias_lanes is not None:
                s[h, r0] = s[h, r0] + cbias_ref[r0:r0 + rb, bias_lanes:bias_lanes + width]
        m_prev = {(h, r0): m_sc[h, r0:r0 + rb] for h, r0 in subs}
        m_next = {key: jnp.maximum(m_prev[key], jnp.max(s[key], axis=-1, keepdims=True)) for key in subs}
        p = {key: jnp.exp(s[key] - jnp.concatenate([m_next[key]] * (width // 128), axis=1)).astype(BF16)
             for key in subs}
        alpha = {key: jnp.exp(m_prev[key] - m_next[key]) for key in subs}
        for h, r0 in subs:
            m_sc[h, r0:r0 + rb] = m_next[h, r0]
        for pair in range(n_pairs):
            for r0 in range(0, tq, rb):
                rows = slice(r0, r0 + rb)
                decay = _pair_lanes(alpha[2 * pair, r0], alpha[2 * pair + 1, r0])
                decay = jnp.concatenate([decay, decay], axis=1)
                acc_sc[pair, rows] = (decay * acc_sc[pair, rows]
                                      + jnp.dot(p[2 * pair, r0], val_even, preferred_element_type=F32)
                                      + jnp.dot(p[2 * pair + 1, r0], val_odd, preferred_element_type=F32))

    def wide(j, carry):
        slc_keys(pl.multiple_of(j * (2 * tq), 2 * tq), 2 * tq, None)
        return carry

    lax.fori_loop(0, i // 2, wide, 0)

    @pl.when(i % 2 == 0)
    def _():
        slc_keys(pl.multiple_of(i * tq, tq), tq, tq)

    @pl.when(i % 2 == 1)
    def _():
        slc_keys(pl.multiple_of((i - 1) * tq, 2 * tq), 2 * tq, 0)

    o_slc = []
    for pair in range(n_pairs):
        acc = acc_sc[pair]
        o_slc.append(acc[:, :2 * HEAD_DIM] / acc[:, 2 * HEAD_DIM:])
    o_slc = jnp.concatenate(o_slc, axis=1)

    sg = _split(jax.nn.sigmoid(gate_ref[...]))
    gates = (jnp.dot(sg[0], gexp_ref[...], preferred_element_type=F32)
             + jnp.dot(sg[1], gexp_ref[...], preferred_element_type=F32))
    wd = NSA_HG * HEAD_DIM
    y = gates[:, :wd] * o_cmp + gates[:, wd:2 * wd] * o_slc + gates[:, 2 * wd:] * o_win
    z = z_ref[...]
    o_ref[...] = (y * (z * jax.nn.sigmoid(z))).astype(o_ref.dtype)


def _attention(q, z, gate, kv_slc, kv_win, kcvc, bq, bz, bkv, sinks):
    b, t, _ = q.shape
    tq = ATT_TILE
    assert t % (2 * tq) == 0 and t >= NSA_WINDOW + tq and NSA_WINDOW % tq == 0 and SWA_HG == 2
    assert NSA_GROUPS == SWA_GROUPS
    n_slc = t // SLC_BLOCK
    nch = t // CMP_STRIDE
    wd = NSA_HG * HEAD_DIM
    negexp = np.where(np.arange(t)[:, None] // SLC_BLOCK == np.arange(128)[None, :], -2.0 ** 100, 0.0)
    ci = np.arange(nch)[None, :] * CMP_STRIDE
    sj = np.arange(n_slc)[:, None] * SLC_BLOCK
    overlap_t = ((ci < sj + SLC_BLOCK) & (ci + CMP_BLOCK > sj)).astype(np.float32)
    rb = ATT_ROWS
    assert NSA_WINDOW % rb == 0 and SWA_WINDOW % rb == 0 and tq % rb == 0
    wbias = _band_bias_table(range(0, NSA_WINDOW + 1, rb), rb, NSA_WINDOW + rb, NSA_WINDOW)
    cbias = np.concatenate([np.zeros((tq, tq), np.float32), _band_bias_table([0], tq, tq, t)[0]], axis=1)
    bbias = _band_bias_table(range(0, SWA_WINDOW + 1, rb), rb, SWA_WINDOW + rb, SWA_WINDOW)
    gexp = np.zeros((128, 3 * wd), np.float32)
    for br in range(3):
        for h in range(NSA_HG):
            gexp[br * NSA_HG + h, br * wd + h * HEAD_DIM:br * wd + (h + 1) * HEAD_DIM] = 1.0

    def const(arr):
        return pl.BlockSpec(arr.shape, lambda bi, g, i: (0,) * arr.ndim)

    def tile(width):
        return pl.BlockSpec((None, tq, width), lambda bi, g, i: (bi, i, g))

    def keys():
        return pl.BlockSpec((None, t, 128), lambda bi, g, i: (bi, 0, g))

    consts = [jnp.asarray(negexp, BF16), jnp.asarray(overlap_t, BF16), jnp.asarray(wbias),
              jnp.asarray(cbias), jnp.asarray(gexp, BF16), jnp.asarray(bbias)]
    return pl.pallas_call(
        _attention_kernel,
        grid=(b, NSA_GROUPS, t // tq),
        in_specs=[
            pl.BlockSpec(memory_space=pltpu.SMEM),
            tile(256), tile(256), tile(128), keys(), keys(),
            pl.BlockSpec((None, None, nch, HEAD_DIM), lambda bi, g, i: (bi, g, 0, 0)),
            pl.BlockSpec((None, None, nch, HEAD_DIM), lambda bi, g, i: (bi, 2 + g, 0, 0)),
            tile(128), tile(128), keys(),
        ] + [const(a) for a in consts],
        out_specs=(tile(256), tile(128)),
        out_shape=(jax.ShapeDtypeStruct((b, t, 512), BF16), jax.ShapeDtypeStruct((b, t, 256), BF16)),
        scratch_shapes=[pltpu.VMEM((t, 256), BF16)] * 7 + [
            pltpu.VMEM((NSA_HG, tq, 128), F32),
            pltpu.VMEM((NSA_HG // 2, tq, 256), F32),
        ],
        compiler_params=_params(("parallel", "parallel", "arbitrary")),
    )(sinks, q, z, gate, kv_slc, kv_win, kcvc, kcvc, bq, bz, bkv, *consts)


_NN = (((1,), (0,)), ((), ()))
_NT = (((1,), (1,)), ((), ()))


def _split(a):
    hi = a.astype(BF16)
    lo = (a - hi.astype(F32)).astype(BF16)
    return hi, lo


def _dot(a, b, dims=_NN):
    return lax.dot_general(a, b, dims, preferred_element_type=F32)


def _dot3(a, b, dims=_NN):
    rows = a[0].shape[0]
    both = _dot(jnp.concatenate([a[0], a[1]], axis=0), b[0], dims)
    return (both[:rows] + both[rows:]) + _dot(a[0], b[1], dims)


def _dot_exact_lhs(a, b, n_terms):
    out = None
    for _ in range(n_terms):
        term = b.astype(BF16)
        b = b - term.astype(F32)
        part = _dot(a, term)
        out = part if out is None else out + part
    return out


def _dot_exact_rhs(x, ones):
    hi, lo = _split(x)
    return _dot(hi, ones) + _dot(lo, ones)


def _block_diag(pair, head_masks):
    return tuple(jnp.concatenate([x * m for m in head_masks], axis=0) for x in pair)


def _each(fn, *lists):
    return [fn(*items) for items in zip(*lists)]


def _unit_lower_inverse(lows, eye, same_sub, bd):
    diag = _each(lambda low: jnp.where(same_sub, low, 0.0), lows)
    off = _each(lambda low, d: low - d, lows, diag)
    inv = _each(lambda d: eye + d, diag)
    power = diag
    span = 2
    while span < RWKV_SUB:
        ps = _each(_split, power)
        power = _each(lambda p: _dot3(p, bd(p)), ps)
        yield
        inv = _each(lambda iv, pw: _dot3(_split(iv), bd(_split(eye + pw))), inv, power)
        yield
        span *= 2
    inv_s = _each(_split, inv)
    cross = _each(lambda iv, o: _dot3(iv, bd(_split(o))), inv_s, off)
    yield
    total = _each(lambda cr: eye + cr, cross)
    power = cross
    span = 2
    while span < RWKV_CHUNK // RWKV_SUB:
        ps = _each(_split, power)
        power = _each(lambda p: _dot3(p, bd(p)), ps)
        yield
        total = _each(lambda tt, pw: _dot3(_split(tt), bd(_split(eye + pw))), total, power)
        yield
        span *= 2
    return _each(lambda tt, iv: _dot3(_split(tt), bd(iv)), total, inv_s)


def _rwkv_chunk_maps(a_t, r_t, b_t, k_t, b_e, k_e, v, p_end, cst):
    C = RWKV_CHUNK
    eye, strict, incl, same_sub, head_masks = cst
    bd = lambda pair: _block_diag(pair, head_masks)

    ar = _each(lambda a, r: _split(jnp.concatenate([a, r], axis=0)), a_t, r_t)
    g_b = _each(lambda x, b: _dot3(x, bd(_split(b)), _NT), ar, b_t)
    g_k = _each(lambda x, k: _dot3(x, bd(_split(k)), _NT), ar, k_t)
    yield
    a_ab = _each(lambda g: jnp.where(strict, g[:C], 0.0), g_b)
    a_rb = _each(lambda g: _split(jnp.where(incl, g[C:], 0.0)), g_b)
    a_ak = _each(lambda g: _split(jnp.where(strict, g[:C], 0.0)), g_k)
    a_rk = _each(lambda g: _split(jnp.where(incl, g[C:], 0.0)), g_k)
    v_bd = _each(lambda x: bd(_split(x)), v)
    akv = _each(_dot3, a_ak, v_bd)
    yield
    t_inv = yield from _unit_lower_inverse(a_ab, eye, same_sub, bd)
    t_inv = _each(_split, t_inv)
    yield
    w_m = _each(lambda t, a: _dot3(t, bd(_split(a))), t_inv, a_t)
    u0 = _each(lambda t, x: _dot3(t, bd(_split(x))), t_inv, akv)
    yield
    r_m = _each(lambda r, g, w: r + _dot3(g, bd(_split(w))), r_t, a_rb, w_m)
    y0 = _each(lambda g, u, gk, vb: _dot3(g, bd(_split(u))) + _dot3(gk, vb), a_rb, u0, a_rk, v_bd)
    yield
    bk = _each(lambda b, k: _split(jnp.transpose(jnp.concatenate([b, k], axis=0))), b_e, k_e)
    m_add = _each(lambda x, w: _dot3((x[0][:, :C], x[1][:, :C]), _split(w)), bk, w_m)
    n_all = _each(lambda x, u, vv: _dot3(x, _split(jnp.concatenate([u, vv], axis=0))), bk, u0, v)
    yield
    keep = [m.astype(F32) for m in head_masks]

    def own_blocks(full):
        parts = [full[h * HEAD_DIM:(h + 1) * HEAD_DIM, :] * keep[h] for h in range(RWKV_HEADS)]
        return (parts[0] + parts[1]) + (parts[2] + parts[3])

    m_l = _each(lambda m, p: own_blocks(m) + eye * p, m_add, p_end)
    n_l = _each(own_blocks, n_all)
    lhs = _each(lambda r, m: _split(jnp.concatenate([r, m], axis=0)), r_m, m_l)
    return list(zip(lhs, y0, n_l))


def _rwkv_kernel(*refs, has_vres):
    if has_vres:
        (feat_ref, prev_ref, z_ref, vfirst_ref, mu_ref, w0_ref, w2_ref, a0_ref, a2_ref, kk_ref, ka_ref,
         rk_ref, lnw_ref, lnb_ref, hsum_ref, v0_ref, v1_ref, v2_ref, y_ref, state) = refs
    else:
        (feat_ref, prev_ref, z_ref, mu_ref, w0_ref, w2_ref, a0_ref, a2_ref, kk_ref, ka_ref,
         rk_ref, lnw_ref, lnb_ref, hsum_ref, y_ref, vout_ref, state) = refs
    c = pl.program_id(1)
    C = RWKV_CHUNK
    W = RWKV_WIDTH
    n_tok = feat_ref.shape[0]
    group = RWKV_GROUP_CHUNKS * C

    @pl.when(c == 0)
    def _():
        state[...] = jnp.zeros(state.shape, F32)

    def lora(x, w_ref):
        return _dot3(_split(x), _split(w_ref[...]))

    hsum = hsum_ref[...]

    def head_sum(x):
        return _dot_exact_rhs(x, hsum)

    feat = feat_ref[...]
    row = lax.broadcasted_iota(jnp.int32, feat.shape, 0)
    last_prev = jnp.where(c == 0, 0.0, prev_ref[7:8, :])
    prev = jnp.where(row == 0, last_prev, pltpu.roll(feat, 1, 0))
    xs = feat + (prev - feat) * mu_ref[...]

    ti = lax.broadcasted_iota(jnp.int32, (group, group), 0)
    tj = lax.broadcasted_iota(jnp.int32, (group, group), 1)
    tri = jnp.where((ti >= tj) & (ti // C == tj // C), 1.0, 0.0).astype(BF16)
    t_idx = lax.broadcasted_iota(jnp.int32, (C, W), 0)
    i_idx = lax.broadcasted_iota(jnp.int32, (C, W), 1) % HEAD_DIM
    lane_head = lax.broadcasted_iota(jnp.int32, (C, W), 1) // HEAD_DIM
    cst = (
        jnp.where(t_idx == i_idx, 1.0, 0.0),
        t_idx > i_idx,
        t_idx >= i_idx,
        (t_idx // RWKV_SUB) == (i_idx // RWKV_SUB),
        [jnp.where(lane_head == h, 1.0, 0.0).astype(BF16) for h in range(RWKV_HEADS)],
    )
    carry = {"state": state[...], "turn": 0}

    def token_group(idx):
        rows = slice(idx * group, (idx + 1) * group)
        x = xs[rows]
        r = x[:, :W]
        k = x[:, W:2 * W]
        v = x[:, 2 * W:3 * W]
        wd = x[:, 3 * W:3 * W + DECAY_LORA]
        ad = x[:, 3 * W + DECAY_LORA:]
        pre = -(w0_ref[...] + lora(jnp.tanh(wd), w2_ref))
        softplus = jnp.maximum(pre, 0.0) + jnp.log(1.0 + jnp.exp(-jnp.abs(pre)))
        logw = -jnp.exp(-softplus - 0.5)
        yield
        if has_vres:
            mix = jax.nn.sigmoid(v0_ref[...] + lora(lora(v, v1_ref), v2_ref))
            v = v + (vfirst_ref[rows, :] - v) * mix
        else:
            vout_ref[rows, :] = v
        alpha = jax.nn.sigmoid(a0_ref[...] + lora(ad, a2_ref))
        kk = k * kk_ref[...]
        kk = kk / jnp.maximum(jnp.sqrt(head_sum(kk * kk)), 1e-12)
        k = k * (1.0 + (alpha - 1.0) * ka_ref[...])
        yield
        cum = _dot_exact_lhs(tri, logw, 3)
        yield
        a_t = -kk * jnp.exp(cum - logw)
        r_t = r * jnp.exp(cum)
        inv_decay = jnp.exp(-cum)
        b_raw = kk * alpha
        b_t = b_raw * inv_decay
        k_t = k * inv_decay
        chunks = [slice(j * C, (j + 1) * C) for j in range(RWKV_GROUP_CHUNKS)]
        cum_end = [cum[cr.stop - 1:cr.stop, :] for cr in chunks]
        to_end = [jnp.exp(ce - cum[cr]) for ce, cr in zip(cum_end, chunks)]
        yield
        maps = yield from _rwkv_chunk_maps(
            [a_t[cr] for cr in chunks], [r_t[cr] for cr in chunks], [b_t[cr] for cr in chunks],
            [k_t[cr] for cr in chunks], [b_raw[cr] * te for cr, te in zip(chunks, to_end)],
            [k[cr] * te for cr, te in zip(chunks, to_end)], [v[cr] for cr in chunks],
            [jnp.exp(ce) for ce in cum_end], cst)
        while carry["turn"] != idx:
            yield
        s_l = carry["state"]
        ys = []
        for lhs, y0, n_l in maps:
            out = _dot3(lhs, _block_diag(_split(s_l), cst[4]))
            ys.append(out[:C] + y0)
            s_l = out[C:] + n_l
        carry["state"] = s_l
        carry["turn"] = idx + 1
        yield
        y = jnp.concatenate(ys, axis=0)
        inv_n = 1.0 / HEAD_DIM
        mean = head_sum(y) * inv_n
        cen = y - mean
        yield
        var = head_sum(cen * cen) * inv_n
        y = cen * lax.rsqrt(var + GN_EPS) * lnw_ref[...] + lnb_ref[...]
        y = y + head_sum(r * k * rk_ref[...]) * v
        z = z_ref[rows, :]
        y_ref[rows, :] = (y * (z * jax.nn.sigmoid(z))).astype(y_ref.dtype)

    _interleave(*[token_group(idx) for idx in range(n_tok // group)], stagger=RWKV_STAGGER)
    state[...] = carry["state"]


def _rwkv(feat, z, v_first, p):
    b, t, _ = feat.shape
    n_tok = RWKV_STEP_CHUNKS * RWKV_CHUNK
    assert t % n_tok == 0 and RWKV_CHUNK == HEAD_DIM
    W = RWKV_WIDTH
    has_vres = v_first is not None
    hsum = (np.arange(W)[:, None] // HEAD_DIM == np.arange(W)[None, :] // HEAD_DIM).astype(np.float32)

    def tok(width):
        return pl.BlockSpec((None, n_tok, width), lambda bi, c: (bi, c, 0))

    def full(arr):
        return pl.BlockSpec(arr.shape, lambda bi, c: (0,) * arr.ndim)

    args = [feat, feat, z]
    specs = [tok(SHIFT_WIDTH),
             pl.BlockSpec((None, 8, SHIFT_WIDTH), lambda bi, c: (bi, jnp.maximum(c * (n_tok // 8) - 1, 0), 0)),
             tok(W)]
    if has_vres:
        args.append(v_first)
        specs.append(tok(W))
    names = ["mu", "w0", "w2", "a0", "a2", "k_k", "k_a", "r_k", "ln_w", "ln_b"]
    consts = [p[n] for n in names] + [jnp.asarray(hsum, BF16)]
    if has_vres:
        consts += [p["v0"], p["v1"], p["v2"]]
    args += consts
    specs += [full(a) for a in consts]
    y_shape = jax.ShapeDtypeStruct((b, t, W), BF16)
    if has_vres:
        out_shape, out_specs = y_shape, tok(W)
    else:
        out_shape = (y_shape, jax.ShapeDtypeStruct((b, t, W), F32))
        out_specs = (tok(W), tok(W))
    res = pl.pallas_call(
        functools.partial(_rwkv_kernel, has_vres=has_vres),
        grid=(b, t // n_tok),
        in_specs=specs,
        out_specs=out_specs,
        out_shape=out_shape,
        scratch_shapes=[pltpu.VMEM((HEAD_DIM, W), F32)],
        compiler_params=_params(("parallel", "arbitrary")),
    )(*args)
    if has_vres:
        return res, v_first
    return res[0], res[1]


def _merge_kernel(x_ref, ya_ref, yb_ref, yc_ref, mg_ref, pa_ref, pb_ref, pc_ref, wo_ref, fg_ref,
                  o_ref, *, final_norm):
    d = D_MODEL
    mixed = None
    for j, (y_ref, p_ref) in enumerate(((ya_ref, pa_ref), (yb_ref, pb_ref), (yc_ref, pc_ref))):
        gate = mg_ref[:, j * d:(j + 1) * d].astype(F32)
        term = gate * jnp.dot(y_ref[...], p_ref[...], preferred_element_type=F32)
        mixed = term if mixed is None else mixed + term
    x = x_ref[...] + jnp.dot(mixed.astype(BF16), wo_ref[...], preferred_element_type=F32)
    if final_norm:
        ms = jnp.mean(x * x, axis=-1, keepdims=True)
        x = x * lax.rsqrt(ms + NORM_EPS) * fg_ref[...]
    o_ref[...] = x


def _merge(x2d, ya, yb, yc, mg, pa, pb, pc, wo, fg, final_norm):
    m = x2d.shape[0]
    tm = ROW_TILE

    def rows(width):
        return pl.BlockSpec((tm, width), lambda i: (i, 0))

    def full(arr):
        return pl.BlockSpec(arr.shape, lambda i: (0, 0))

    return pl.pallas_call(
        functools.partial(_merge_kernel, final_norm=final_norm),
        grid=(m // tm,),
        in_specs=[rows(D_MODEL), rows(512), rows(256), rows(256), rows(3 * D_MODEL),
                  full(pa), full(pb), full(pc), full(wo), full(fg)],
        out_specs=rows(D_MODEL),
        out_shape=jax.ShapeDtypeStruct((m, D_MODEL), F32),
        compiler_params=_params(("parallel",)),
    )(x2d, ya, yb, yc, mg, pa, pb, pc, wo, fg)


def kernel(x, norm_g, w_in, b_merge, cmp_pe_k, cmp_w1_k, cmp_w2_k, cmp_pe_v, cmp_w1_v, cmp_w2_v, swa_sinks,
           rwkv_mu, rwkv_w0, rwkv_w2, rwkv_a0, rwkv_a2, rwkv_k_k, rwkv_k_a, rwkv_r_k, rwkv_ln_w, rwkv_ln_b,
           rwkv_v0, rwkv_v1, rwkv_v2, proj_a, proj_b, proj_c, w_out, final_g):
    b, t, d = x.shape
    depth = w_in.shape[0]
    m = b * t
    x2d = x.reshape(m, d)
    v_first = None
    for l in range(depth):
        w = jnp.concatenate([jnp.zeros((d, n), BF16) if start == N_IN else w_in[l][:, start:start + n].astype(BF16)
                             for start, n in _COL_RUNS], axis=1)
        outs = _in_proj(x2d, norm_g[l].reshape(1, d), b_merge[l].reshape(1, -1), w)
        seg = {name: o.reshape(b, t, o.shape[-1]) for (name, _, _), o in zip(_SEGS, outs)}

        pe = jnp.stack([cmp_pe_k[l], cmp_pe_v[l]]).reshape(2, 1, CMP_BLOCK * HEAD_DIM)
        kcvc = _compress(seg["a_kv_cmp"], pe, jnp.stack([cmp_w1_k[l], cmp_w1_v[l]]),
                         jnp.stack([cmp_w2_k[l], cmp_w2_v[l]]))
        y_a, y_b = _attention(seg["a_q"], seg["a_z"], seg["a_gate"], seg["a_kv_slc"], seg["a_kv_win"], kcvc,
                              seg["b_q"], seg["b_z"], seg["b_kv"], swa_sinks[l])

        row = lambda a: a.reshape(1, -1)
        p = {"mu": row(rwkv_mu[l]), "w0": row(rwkv_w0[l]), "w2": rwkv_w2[l], "a0": row(rwkv_a0[l]),
             "a2": rwkv_a2[l], "k_k": row(rwkv_k_k[l]), "k_a": row(rwkv_k_a[l]), "r_k": row(rwkv_r_k[l]),
             "ln_w": row(rwkv_ln_w[l]), "ln_b": row(rwkv_ln_b[l])}
        if l > 0:
            p.update(v0=row(rwkv_v0[l - 1]), v1=rwkv_v1[l - 1], v2=rwkv_v2[l - 1])
        y_c, v_first = _rwkv(seg["c_shift"], seg["c_z"], v_first, p)

        x2d = _merge(x2d, y_a.reshape(m, -1), y_b.reshape(m, -1), y_c.reshape(m, -1),
                     seg["merge"].reshape(m, -1), proj_a[l].astype(BF16), proj_b[l].astype(BF16),
                     proj_c[l].astype(BF16), w_out[l].astype(BF16), final_g.reshape(1, d),
                     final_norm=(l == depth - 1))
    return x2d.reshape(b, t, d)
```

```python
import functools

import numpy as np
import jax
import jax.numpy as jnp
from jax import lax
from jax.experimental import pallas as pl
from jax.experimental.pallas import tpu as pltpu

F32 = jnp.float32
BF16 = jnp.bfloat16

D_MODEL = 1024
HEAD_DIM = 64
NSA_HEADS = 8
NSA_GROUPS = 2
NSA_HG = NSA_HEADS // NSA_GROUPS
CMP_STRIDE = 16
CMP_BLOCK = 32
CMP_HIDDEN = 128
SLC_BLOCK = 64
SLC_TOPN = 16
NSA_WINDOW = 512
SWA_HEADS = 4
SWA_GROUPS = 2
SWA_HG = SWA_HEADS // SWA_GROUPS
SWA_WINDOW = 128
RWKV_HEADS = 4
RWKV_WIDTH = RWKV_HEADS * HEAD_DIM
DECAY_LORA = 64
ICLR_LORA = 64
VRES_LORA = 32
SHIFT_WIDTH = 3 * RWKV_WIDTH + DECAY_LORA + ICLR_LORA
NORM_EPS = 1e-6
GN_EPS = 64e-5
NEG_INF = -1e30
FORCE = 1e9

_REF_SEGMENTS = (
    ("a_q", 512), ("a_kv_cmp", 256), ("a_kv_slc", 256), ("a_kv_win", 256), ("a_gate", 24),
    ("a_z", 512), ("b_q", 256), ("b_kv", 256), ("b_z", 256), ("c_shift", SHIFT_WIDTH),
    ("c_z", 256), ("merge", 3 * D_MODEL),
)
N_IN = sum(w for _, w in _REF_SEGMENTS)

VMEM_LIMIT = 56 * 1024 * 1024
ROW_TILE = 256
ATT_TILE = 256
ATT_ROWS = 128
RWKV_CHUNK = 64
RWKV_STEP_CHUNKS = 8
RWKV_GROUP_CHUNKS = 2
RWKV_STAGGER = 4
RWKV_SUB = 16


def _ref_offsets():
    out, off = {}, 0
    for name, width in _REF_SEGMENTS:
        out[name] = off
        off += width
    return out


def _kv_interleave(base):
    idx = []
    for g in range(2):
        idx += list(range(base + 128 + g * 64, base + 128 + (g + 1) * 64))
        idx += list(range(base + g * 64, base + (g + 1) * 64))
    return idx


def _projection_layout():
    o = _ref_offsets()
    segs, idx = [], []

    def add(name, cols, dtype, parts=1):
        segs.append((name, len(cols), dtype, parts))
        idx.extend(cols)

    add("a_q", list(range(o["a_q"], o["a_q"] + 512)), BF16)
    add("a_z", list(range(o["a_z"], o["a_z"] + 512)), F32)
    add("a_kv_cmp", list(range(o["a_kv_cmp"], o["a_kv_cmp"] + 256)), BF16, parts=4)
    add("a_kv_slc", _kv_interleave(o["a_kv_slc"]), BF16)
    add("a_kv_win", _kv_interleave(o["a_kv_win"]), BF16)
    gate = []
    for g in range(NSA_GROUPS):
        cols = [N_IN] * 128
        for br in range(3):
            for h in range(NSA_HG):
                cols[br * NSA_HG + h] = o["a_gate"] + br * NSA_HEADS + g * NSA_HG + h
        gate += cols
    add("a_gate", gate, F32)
    add("b_q", list(range(o["b_q"], o["b_q"] + 256)), BF16)
    add("b_kv", _kv_interleave(o["b_kv"]), BF16)
    add("b_z", list(range(o["b_z"], o["b_z"] + 256)), F32)
    add("c_shift", list(range(o["c_shift"], o["c_shift"] + SHIFT_WIDTH)), F32)
    add("c_z", list(range(o["c_z"], o["c_z"] + 256)), F32)
    add("merge", list(range(o["merge"], o["merge"] + 3 * D_MODEL)), BF16)
    return tuple(segs), np.asarray(idx, np.int32)


_SEGS, _COL_IDX = _projection_layout()
_N_PROJ = int(_COL_IDX.shape[0])


def _column_runs(idx):
    runs, i = [], 0
    while i < len(idx):
        j = i + 1
        while j < len(idx) and (idx[j] == idx[j - 1] + 1 if idx[i] != N_IN else idx[j] == N_IN):
            j += 1
        runs.append((int(idx[i]), j - i))
        i = j
    return runs


_COL_RUNS = _column_runs(_COL_IDX)


def _params(sem):
    return pltpu.CompilerParams(dimension_semantics=sem, vmem_limit_bytes=VMEM_LIMIT)


def _in_proj_kernel(x_ref, g_ref, bm_ref, w_ref, *out_refs):
    x = x_ref[...]
    ms = jnp.mean(x * x, axis=-1, keepdims=True)
    xn = (x * lax.rsqrt(ms + NORM_EPS) * g_ref[...]).astype(BF16)
    off = 0
    refs = iter(out_refs)
    for name, width, _, parts in _SEGS:
        outs = [next(refs) for _ in range(parts)]
        for c0 in range(0, width, 512):
            cw = min(512, width - c0)
            val = jnp.dot(xn, w_ref[:, off + c0:off + c0 + cw], preferred_element_type=F32)
            if name == "merge":
                val = jax.nn.sigmoid(val + bm_ref[:, c0:c0 + cw])
            if parts == 1:
                outs[0][:, c0:c0 + cw] = val.astype(outs[0].dtype)
            else:
                pw = width // parts
                for j, o_ref in enumerate(outs):
                    o_ref[...] = val[:, j * pw:(j + 1) * pw].astype(o_ref.dtype)
        off += width


def _in_proj(x2d, g, bm, w):
    m = x2d.shape[0]
    tm = ROW_TILE
    widths = [(width // parts, dt) for _, width, dt, parts in _SEGS for _ in range(parts)]
    assert all(width <= 512 for _, width, _, parts in _SEGS if parts > 1)
    out_shape = [jax.ShapeDtypeStruct((m, width), dt) for width, dt in widths]
    out_specs = [pl.BlockSpec((tm, width), lambda i: (i, 0)) for width, _ in widths]
    return pl.pallas_call(
        _in_proj_kernel,
        grid=(m // tm,),
        in_specs=[
            pl.BlockSpec((tm, D_MODEL), lambda i: (i, 0)),
            pl.BlockSpec((1, D_MODEL), lambda i: (0, 0)),
            pl.BlockSpec((1, 3 * D_MODEL), lambda i: (0, 0)),
            pl.BlockSpec((D_MODEL, _N_PROJ), lambda i: (0, 0), pipeline_mode=pl.Buffered(1)),
        ],
        out_specs=out_specs,
        out_shape=out_shape,
        compiler_params=_params(("parallel",)),
    )(x2d, g, bm, w)


def _compress_kernel(*refs):
    z_refs, (pe_ref, w1_ref, w2_ref, o_ref) = refs[:-4], refs[-4:]
    half = CMP_STRIDE * HEAD_DIM
    for j, z_ref in enumerate(z_refs):
        kv = j // NSA_GROUPS
        z = z_ref[...]
        w1 = w1_ref[kv]
        first = _dot_exact_lhs(z, w1[:half], 3)
        second = _dot_exact_lhs(z, w1[half:], 3)
        n = z.shape[0]
        hid = first + pltpu.roll(second, n - 1, 0)
        hid = hid + _dot3(_split(pe_ref[kv]), _split(w1))
        act = hid * jax.nn.sigmoid(hid)
        o_ref[j] = _dot3(_split(act), _split(w2_ref[kv]))


def _compress(kv_cmp, pe, w1, w2):
    b, t, _ = kv_cmp[0].shape
    nch = t // CMP_STRIDE
    zs = [a.reshape(b, nch, CMP_STRIDE * HEAD_DIM) for a in kv_cmp]

    def full(arr):
        return pl.BlockSpec(arr.shape, lambda i: (0,) * arr.ndim)

    return pl.pallas_call(
        _compress_kernel,
        grid=(b,),
        in_specs=[pl.BlockSpec((None, nch, CMP_STRIDE * HEAD_DIM), lambda i: (i, 0, 0)) for _ in zs]
        + [full(pe), full(w1), full(w2)],
        out_specs=pl.BlockSpec((None, len(zs), nch, HEAD_DIM), lambda i: (i, 0, 0, 0)),
        out_shape=jax.ShapeDtypeStruct((b, len(zs), nch, HEAD_DIM), F32),
        compiler_params=_params(("parallel",)),
    )(*zs, pe, w1, w2)


def _widen_queries(q, n_heads):
    zeros = jnp.zeros((q.shape[0], HEAD_DIM), q.dtype)
    return [jnp.concatenate([zeros, q[:, h * HEAD_DIM:(h + 1) * HEAD_DIM]], axis=1) for h in range(n_heads)]


def _pair_values(kv):
    swapped = jnp.concatenate([kv[:, HEAD_DIM:], kv[:, :HEAD_DIM]], axis=1)
    lane = lax.broadcasted_iota(jnp.int32, kv.shape, 1)
    low = jnp.where(lane < HEAD_DIM, 1.0, 0.0).astype(kv.dtype)
    high = jnp.where(lane < HEAD_DIM, 0.0, 1.0).astype(kv.dtype)
    even = jnp.concatenate([kv * low, low], axis=1)
    odd = jnp.concatenate([swapped * high, high], axis=1)
    return even, odd


def _pair_lanes(even, odd):
    low = lax.broadcasted_iota(jnp.int32, even.shape, 1) < HEAD_DIM
    return jnp.where(low, even, odd)


def _nt(a, b):
    return lax.dot_general(a, b, (((1,), (1,)), ((), ())), preferred_element_type=F32)


def _rounds(streams, stagger=0):
    results = [None] * len(streams)
    waiting = list(enumerate(streams))
    live = {}
    rounds = 0
    while waiting or live:
        while waiting and waiting[0][0] * stagger <= rounds:
            idx, stream = waiting.pop(0)
            live[idx] = stream
        for idx in list(live):
            try:
                next(live[idx])
            except StopIteration as stop:
                results[idx] = stop.value
                del live[idx]
        rounds += 1
        yield
    return results


def _interleave(*streams, stagger=0):
    rounds = _rounds(streams, stagger)
    while True:
        try:
            next(rounds)
        except StopIteration as stop:
            return stop.value


def _band_block(qa_even, qa_odd, keys, val_even, val_odd, bias, sinks=None):
    s = [_nt(qa, keys) + bias for qa in (qa_even, qa_odd)]
    yield
    m = [jnp.max(x, axis=-1, keepdims=True) for x in s]
    if sinks is not None:
        m = [jnp.maximum(x, sinks[e]) for e, x in enumerate(m)]
    p = [jnp.exp(x - mx).astype(BF16) for x, mx in zip(s, m)]
    yield
    acc = (jnp.dot(p[0], val_even, preferred_element_type=F32)
           + jnp.dot(p[1], val_odd, preferred_element_type=F32))
    denom = acc[:, 2 * HEAD_DIM:]
    if sinks is not None:
        shape = (acc.shape[0], 2 * HEAD_DIM)
        denom = denom + _pair_lanes(jnp.broadcast_to(jnp.exp(sinks[0] - m[0]), shape),
                                    jnp.broadcast_to(jnp.exp(sinks[1] - m[1]), shape))
    return acc[:, :2 * HEAD_DIM] / denom


def _band_blocks(blocks, sinks=None):
    return (yield from _rounds([_band_block(*blk, sinks) for blk in blocks], stagger=1))


def _band_bias_table(deltas, rows, width, window):
    r = np.arange(rows)[:, None]
    c = np.arange(width)[None, :]
    out = [np.where((c <= r + d) & (c > r + d - window), 0.0, NEG_INF) for d in deltas]
    return np.stack(out).astype(np.float32)


def _attention_kernel(sink_ref, q_ref, z_ref, gate_ref, kvs_ref, kvw_ref, kcvc_ref, bq_ref, bz_ref,
                      bkv_ref, negexp_ref, ovl_ref, wbias_ref, cbias_ref, gexp_ref, bbias_ref,
                      o_ref, ob_ref, srhs_sc, spe_sc, spo_sc, wpe_sc, wpo_sc, bpe_sc, bpo_sc, m_sc, acc_sc):
    i = pl.program_id(1)
    tq = ATT_TILE
    rb = ATT_ROWS
    n_rb = tq // rb
    n_cmp_pad = kcvc_ref.shape[1]
    n_slc = ovl_ref.shape[0]
    n_pairs = NSA_HG // 2
    groups = range(NSA_GROUPS)
    wd = NSA_HG * HEAD_DIM
    wb = SWA_HG * HEAD_DIM
    kv_lanes = [slice(g * 2 * HEAD_DIM, (g + 1) * 2 * HEAD_DIM) for g in groups]

    @pl.when(i == 0)
    def _():
        for g in groups:
            kvs = kvs_ref[:, kv_lanes[g]]
            srhs_sc[g] = jnp.concatenate([kvs, negexp_ref[...]], axis=1)
            spe_sc[g], spo_sc[g] = _pair_values(kvs)
            wpe_sc[g], wpo_sc[g] = _pair_values(kvw_ref[:, kv_lanes[g]])
            bpe_sc[g], bpo_sc[g] = _pair_values(bkv_ref[:, kv_lanes[g]])

    scale = jnp.asarray(HEAD_DIM ** -0.5, BF16)
    qas = [_widen_queries(q_ref[:, g * wd:(g + 1) * wd] * scale, NSA_HG) for g in groups]

    def band_inputs(g, kv_ref, pe_sc, po_sc, bias_ref, window, r0):
        q0 = i * tq + r0
        k0 = pl.multiple_of(jnp.maximum(q0 - window, 0), rb)
        width = window + rb
        variant = jnp.minimum(q0 // rb, bias_ref.shape[0] - 1)
        return (kv_ref[pl.ds(k0, width), kv_lanes[g]], pe_sc[g, pl.ds(k0, width), :],
                po_sc[g, pl.ds(k0, width), :], bias_ref[variant])

    def window_stream(g):
        bands = [band_inputs(g, kvw_ref, wpe_sc, wpo_sc, wbias_ref, NSA_WINDOW, r0) for r0 in range(0, tq, rb)]
        blocks = [(qas[g][2 * pair][r0:r0 + rb], qas[g][2 * pair + 1][r0:r0 + rb]) + bands[r0 // rb]
                  for pair in range(n_pairs) for r0 in range(0, tq, rb)]
        outs = yield from _band_blocks(blocks)
        return jnp.concatenate([jnp.concatenate(outs[pair * n_rb:(pair + 1) * n_rb], axis=0)
                                for pair in range(n_pairs)], axis=1)

    def sink_window_stream(g):
        lanes = slice(g * wb, (g + 1) * wb)
        qbs = _widen_queries(bq_ref[:, lanes] * scale, SWA_HG)
        sinks = (sink_ref[g * SWA_HG], sink_ref[g * SWA_HG + 1])
        blocks = [(qbs[0][r0:r0 + rb], qbs[1][r0:r0 + rb])
                  + band_inputs(g, bkv_ref, bpe_sc, bpo_sc, bbias_ref, SWA_WINDOW, r0) for r0 in range(0, tq, rb)]
        outs = yield from _band_blocks(blocks, sinks)
        y = jnp.concatenate(outs, axis=0)
        zb = bz_ref[:, lanes]
        ob_ref[:, lanes] = (y * (zb * jax.nn.sigmoid(zb))).astype(ob_ref.dtype)

    def compressed_stream(g):
        zeros_k = jnp.zeros((n_cmp_pad, HEAD_DIM), BF16)
        kc_both = jnp.concatenate([jnp.concatenate([zeros_k, t], axis=1) for t in _split(kcvc_ref[g])], axis=0)
        vc = kcvc_ref[NSA_GROUPS + g].astype(BF16)
        vc_even = jnp.concatenate([vc, zeros_k], axis=1)
        vc_odd = jnp.concatenate([zeros_k, vc], axis=1)
        tpos = i * tq + lax.broadcasted_iota(jnp.int32, (tq, n_cmp_pad), 0)
        cmp_end = lax.broadcasted_iota(jnp.int32, (tq, n_cmp_pad), 1) * CMP_STRIDE + (CMP_BLOCK - 1)
        valid_c = cmp_end <= tpos
        s = [_nt(qa, kc_both) for qa in qas[g]]
        s = [x[:, :n_cmp_pad] + x[:, n_cmp_pad:] for x in s]
        yield
        s = [jnp.where(valid_c, x, NEG_INF) for x in s]
        m = [jnp.max(x, axis=-1, keepdims=True) for x in s]
        yield
        p = [jnp.where(valid_c, jnp.exp(x - mx), 0.0) for x, mx in zip(s, m)]
        denom = [jnp.sum(x, axis=-1, keepdims=True) for x in p]
        yield
        p = [x * (1.0 / jnp.where(d > 0, d, 1.0)) for x, d in zip(p, denom)]
        p_sum = (p[0] + p[1]) + (p[2] + p[3])
        pb = [x.astype(BF16) for x in p]
        o_cmp = jnp.concatenate(
            [jnp.dot(pb[2 * pair], vc_even, preferred_element_type=F32)
             + jnp.dot(pb[2 * pair + 1], vc_odd, preferred_element_type=F32) for pair in range(n_pairs)], axis=1)
        yield
        imp = None
        rest = p_sum
        for _ in range(3):
            term = rest.astype(BF16)
            rest = rest - term.astype(F32)
            part = _nt(ovl_ref[...], term)
            imp = part if imp is None else imp + part
        yield
        blk = lax.broadcasted_iota(jnp.int32, (n_slc, tq), 0)
        cur = (i * tq + lax.broadcasted_iota(jnp.int32, (n_slc, tq), 1)) // SLC_BLOCK
        forced = (blk == 0) | (blk == cur) | (blk == cur - 1)
        score = jnp.where(forced, FORCE, jnp.where(blk <= cur, imp, -FORCE))
        ranks = [jnp.zeros((n_slc, tq), F32) for _ in range(4)]
        for r in range(n_slc):
            row = score[r:r + 1, :]
            ahead = (row > score) | ((row == score) & (r < blk))
            ranks[r % 4] = ranks[r % 4] + jnp.where(ahead, 1.0, 0.0)
            if r % 4 == 3:
                yield
        rank = (ranks[0] + ranks[1]) + (ranks[2] + ranks[3])
        n_sel = min(SLC_TOPN, n_slc)
        unsel_t = jnp.where((rank < n_sel) & (blk <= cur), 0.0, 1.0)
        unsel_t = jnp.concatenate([unsel_t, jnp.zeros((128 - n_slc, tq), F32)], axis=0)
        return o_cmp, jnp.transpose(unsel_t).astype(BF16)

    results = _interleave(*[stream(g) for g in groups
                            for stream in (window_stream, sink_window_stream, compressed_stream)])
    o_win = [results[3 * g] for g in groups]
    o_cmp = [results[3 * g + 2][0] for g in groups]
    unsel = [results[3 * g + 2][1] for g in groups]

    lhs = {(g, h): jnp.concatenate([qas[g][h], unsel[g]], axis=1) for g in groups for h in range(NSA_HG)}
    m_sc[...] = jnp.full(m_sc.shape, NEG_INF, F32)
    acc_sc[...] = jnp.zeros(acc_sc.shape, F32)
    subs = [(g, h, r0) for g in groups for h in range(NSA_HG) for r0 in range(0, tq, rb)]

    def slc_keys(start, width, bias_lanes):
        rhs = [srhs_sc[g, pl.ds(start, width), :] for g in groups]
        val_even = [spe_sc[g, pl.ds(start, width), :] for g in groups]
        val_odd = [spo_sc[g, pl.ds(start, width), :] for g in groups]
        s = {}
        for g, h, r0 in subs:
            s[g, h, r0] = _nt(lhs[g, h][r0:r0 + rb], rhs[g])
            if bias_lanes is not None:
                s[g, h, r0] = s[g, h, r0] + cbias_ref[r0:r0 + rb, bias_lanes:bias_lanes + width]
        m_prev = {(g, h, r0): m_sc[g * NSA_HG + h, r0:r0 + rb] for g, h, r0 in subs}
        m_next = {key: jnp.maximum(m_prev[key], jnp.max(s[key], axis=-1, keepdims=True)) for key in subs}
        p = {key: jnp.exp(s[key] - jnp.concatenate([m_next[key]] * (width // 128), axis=1)).astype(BF16)
             for key in subs}
        alpha = {key: jnp.exp(m_prev[key] - m_next[key]) for key in subs}
        for g, h, r0 in subs:
            m_sc[g * NSA_HG + h, r0:r0 + rb] = m_next[g, h, r0]
        for g in groups:
            for pair in range(n_pairs):
                for r0 in range(0, tq, rb):
                    rows = slice(r0, r0 + rb)
                    decay = _pair_lanes(alpha[g, 2 * pair, r0], alpha[g, 2 * pair + 1, r0])
                    decay = jnp.concatenate([decay, decay], axis=1)
                    acc_sc[g * n_pairs + pair, rows] = (
                        decay * acc_sc[g * n_pairs + pair, rows]
                        + jnp.dot(p[g, 2 * pair, r0], val_even[g], preferred_element_type=F32)
                        + jnp.dot(p[g, 2 * pair + 1, r0], val_odd[g], preferred_element_type=F32))

    def wide(j, carry):
        slc_keys(pl.multiple_of(j * (2 * tq), 2 * tq), 2 * tq, None)
        return carry

    lax.fori_loop(0, i // 2, wide, 0)

    @pl.when(i % 2 == 0)
    def _():
        slc_keys(pl.multiple_of(i * tq, tq), tq, tq)

    @pl.when(i % 2 == 1)
    def _():
        slc_keys(pl.multiple_of((i - 1) * tq, 2 * tq), 2 * tq, 0)

    for g in groups:
        o_slc = []
        for pair in range(n_pairs):
            acc = acc_sc[g * n_pairs + pair]
            o_slc.append(acc[:, :2 * HEAD_DIM] / acc[:, 2 * HEAD_DIM:])
        o_slc = jnp.concatenate(o_slc, axis=1)
        sg = _split(jax.nn.sigmoid(gate_ref[:, g * 128:(g + 1) * 128]))
        gates = (jnp.dot(sg[0], gexp_ref[...], preferred_element_type=F32)
                 + jnp.dot(sg[1], gexp_ref[...], preferred_element_type=F32))
        y = gates[:, :wd] * o_cmp[g] + gates[:, wd:2 * wd] * o_slc + gates[:, 2 * wd:] * o_win[g]
        z = z_ref[:, g * wd:(g + 1) * wd]
        o_ref[:, g * wd:(g + 1) * wd] = (y * (z * jax.nn.sigmoid(z))).astype(o_ref.dtype)


def _attention(q, z, gate, kv_slc, kv_win, kcvc, bq, bz, bkv, sinks):
    b, t, _ = q.shape
    tq = ATT_TILE
    assert t % (2 * tq) == 0 and t >= NSA_WINDOW + tq and NSA_WINDOW % tq == 0 and SWA_HG == 2
    assert NSA_GROUPS == SWA_GROUPS
    n_slc = t // SLC_BLOCK
    nch = t // CMP_STRIDE
    wd = NSA_HG * HEAD_DIM
    negexp = np.where(np.arange(t)[:, None] // SLC_BLOCK == np.arange(128)[None, :], -2.0 ** 100, 0.0)
    ci = np.arange(nch)[None, :] * CMP_STRIDE
    sj = np.arange(n_slc)[:, None] * SLC_BLOCK
    overlap_t = ((ci < sj + SLC_BLOCK) & (ci + CMP_BLOCK > sj)).astype(np.float32)
    rb = ATT_ROWS
    assert NSA_WINDOW % rb == 0 and SWA_WINDOW % rb == 0 and tq % rb == 0
    wbias = _band_bias_table(range(0, NSA_WINDOW + 1, rb), rb, NSA_WINDOW + rb, NSA_WINDOW)
    cbias = np.concatenate([np.zeros((tq, tq), np.float32), _band_bias_table([0], tq, tq, t)[0]], axis=1)
    bbias = _band_bias_table(range(0, SWA_WINDOW + 1, rb), rb, SWA_WINDOW + rb, SWA_WINDOW)
    gexp = np.zeros((128, 3 * wd), np.float32)
    for br in range(3):
        for h in range(NSA_HG):
            gexp[br * NSA_HG + h, br * wd + h * HEAD_DIM:br * wd + (h + 1) * HEAD_DIM] = 1.0

    def const(arr):
        return pl.BlockSpec(arr.shape, lambda bi, i: (0,) * arr.ndim)

    def tile(arr):
        return pl.BlockSpec((None, tq, arr.shape[-1]), lambda bi, i: (bi, i, 0))

    def whole(arr):
        return pl.BlockSpec((None,) + arr.shape[1:], lambda bi, i: (bi,) + (0,) * (arr.ndim - 1))

    consts = [jnp.asarray(negexp, BF16), jnp.asarray(overlap_t, BF16), jnp.asarray(wbias),
              jnp.asarray(cbias), jnp.asarray(gexp, BF16), jnp.asarray(bbias)]
    key_scratch = pltpu.VMEM((NSA_GROUPS, t, 256), BF16)
    return pl.pallas_call(
        _attention_kernel,
        grid=(b, t // tq),
        in_specs=[pl.BlockSpec(memory_space=pltpu.SMEM), tile(q), tile(z), tile(gate), whole(kv_slc),
                  whole(kv_win), whole(kcvc), tile(bq), tile(bz), whole(bkv)] + [const(a) for a in consts],
        out_specs=(pl.BlockSpec((None, tq, 512), lambda bi, i: (bi, i, 0)),
                   pl.BlockSpec((None, tq, 256), lambda bi, i: (bi, i, 0))),
        out_shape=(jax.ShapeDtypeStruct((b, t, 512), BF16), jax.ShapeDtypeStruct((b, t, 256), BF16)),
        scratch_shapes=[key_scratch] * 7 + [
            pltpu.VMEM((NSA_GROUPS * NSA_HG, tq, 128), F32),
            pltpu.VMEM((NSA_GROUPS * NSA_HG // 2, tq, 256), F32),
        ],
        compiler_params=_params(("parallel", "arbitrary")),
    )(sinks, q, z, gate, kv_slc, kv_win, kcvc, bq, bz, bkv, *consts)


_NN = (((1,), (0,)), ((), ()))
_NT = (((1,), (1,)), ((), ()))


def _split(a):
    hi = a.astype(BF16)
    lo = (a - hi.astype(F32)).astype(BF16)
    return hi, lo


def _dot(a, b, dims=_NN):
    return lax.dot_general(a, b, dims, preferred_element_type=F32)


def _dot3(a, b, dims=_NN):
    rows = a[0].shape[0]
    both = _dot(jnp.concatenate([a[0], a[1]], axis=0), b[0], dims)
    return (both[:rows] + both[rows:]) + _dot(a[0], b[1], dims)


def _dot_exact_lhs(a, b, n_terms):
    out = None
    for _ in range(n_terms):
        term = b.astype(BF16)
        b = b - term.astype(F32)
        part = _dot(a, term)
        out = part if out is None else out + part
    return out


def _dot_exact_rhs(x, ones):
    hi, lo = _split(x)
    return _dot(hi, ones) + _dot(lo, ones)


def _block_diag(pair, head_masks):
    return tuple(jnp.concatenate([x * m for m in head_masks], axis=0) for x in pair)


def _each(fn, *lists):
    return [fn(*items) for items in zip(*lists)]


def _unit_lower_inverse(lows, eye, same_sub, bd):
    diag = _each(lambda low: jnp.where(same_sub, low, 0.0), lows)
    off = _each(lambda low, d: low - d, lows, diag)
    inv = _each(lambda d: eye + d, diag)
    power = diag
    span = 2
    while span < RWKV_SUB:
        ps = _each(_split, power)
        power = _each(lambda p: _dot3(p, bd(p)), ps)
        yield
        inv = _each(lambda iv, pw: _dot3(_split(iv), bd(_split(eye + pw))), inv, power)
        yield
        span *= 2
    inv_s = _each(_split, inv)
    cross = _each(lambda iv, o: _dot3(iv, bd(_split(o))), inv_s, off)
    yield
    total = _each(lambda cr: eye + cr, cross)
    power = cross
    span = 2
    while span < RWKV_CHUNK // RWKV_SUB:
        ps = _each(_split, power)
        power = _each(lambda p: _dot3(p, bd(p)), ps)
        yield
        total = _each(lambda tt, pw: _dot3(_split(tt), bd(_split(eye + pw))), total, power)
        yield
        span *= 2
    return _each(lambda tt, iv: _dot3(_split(tt), bd(iv)), total, inv_s)


def _rwkv_chunk_maps(a_t, r_t, b_t, k_t, b_e, k_e, v, p_end, cst):
    C = RWKV_CHUNK
    eye, strict, incl, same_sub, head_masks = cst
    bd = lambda pair: _block_diag(pair, head_masks)

    ar = _each(lambda a, r: _split(jnp.concatenate([a, r], axis=0)), a_t, r_t)
    g_b = _each(lambda x, b: _dot3(x, bd(_split(b)), _NT), ar, b_t)
    g_k = _each(lambda x, k: _dot3(x, bd(_split(k)), _NT), ar, k_t)
    yield
    a_ab = _each(lambda g: jnp.where(strict, g[:C], 0.0), g_b)
    a_rb = _each(lambda g: _split(jnp.where(incl, g[C:], 0.0)), g_b)
    a_ak = _each(lambda g: _split(jnp.where(strict, g[:C], 0.0)), g_k)
    a_rk = _each(lambda g: _split(jnp.where(incl, g[C:], 0.0)), g_k)
    v_bd = _each(lambda x: bd(_split(x)), v)
    akv = _each(_dot3, a_ak, v_bd)
    yield
    t_inv = yield from _unit_lower_inverse(a_ab, eye, same_sub, bd)
    t_inv = _each(_split, t_inv)
    yield
    w_m = _each(lambda t, a: _dot3(t, bd(_split(a))), t_inv, a_t)
    u0 = _each(lambda t, x: _dot3(t, bd(_split(x))), t_inv, akv)
    yield
    r_m = _each(lambda r, g, w: r + _dot3(g, bd(_split(w))), r_t, a_rb, w_m)
    y0 = _each(lambda g, u, gk, vb: _dot3(g, bd(_split(u))) + _dot3(gk, vb), a_rb, u0, a_rk, v_bd)
    yield
    bk = _each(lambda b, k: _split(jnp.transpose(jnp.concatenate([b, k], axis=0))), b_e, k_e)
    m_add = _each(lambda x, w: _dot3((x[0][:, :C], x[1][:, :C]), _split(w)), bk, w_m)
    n_all = _each(lambda x, u, vv: _dot3(x, _split(jnp.concatenate([u, vv], axis=0))), bk, u0, v)
    yield
    keep = [m.astype(F32) for m in head_masks]

    def own_blocks(full):
        parts = [full[h * HEAD_DIM:(h + 1) * HEAD_DIM, :] * keep[h] for h in range(RWKV_HEADS)]
        return (parts[0] + parts[1]) + (parts[2] + parts[3])

    m_l = _each(lambda m, p: own_blocks(m) + eye * p, m_add, p_end)
    n_l = _each(own_blocks, n_all)
    lhs = _each(lambda r, m: _split(jnp.concatenate([r, m], axis=0)), r_m, m_l)
    return list(zip(lhs, y0, n_l))


def _rwkv_kernel(*refs, has_vres):
    if has_vres:
        (feat_ref, prev_ref, z_ref, vfirst_ref, mu_ref, w0_ref, w2_ref, a0_ref, a2_ref, kk_ref, ka_ref,
         rk_ref, lnw_ref, lnb_ref, hsum_ref, v0_ref, v1_ref, v2_ref, y_ref, state) = refs
    else:
        (feat_ref, prev_ref, z_ref, mu_ref, w0_ref, w2_ref, a0_ref, a2_ref, kk_ref, ka_ref,
         rk_ref, lnw_ref, lnb_ref, hsum_ref, y_ref, vout_ref, state) = refs
    c = pl.program_id(1)
    C = RWKV_CHUNK
    W = RWKV_WIDTH
    n_tok = feat_ref.shape[0]
    group = RWKV_GROUP_CHUNKS * C

    @pl.when(c == 0)
    def _():
        state[...] = jnp.zeros(state.shape, F32)

    def lora(x, w_ref):
        return _dot3(_split(x), _split(w_ref[...]))

    hsum = hsum_ref[...]

    def head_sum(x):
        return _dot_exact_rhs(x, hsum)

    feat = feat_ref[...]
    row = lax.broadcasted_iota(jnp.int32, feat.shape, 0)
    last_prev = jnp.where(c == 0, 0.0, prev_ref[7:8, :])
    prev = jnp.where(row == 0, last_prev, pltpu.roll(feat, 1, 0))
    xs = feat + (prev - feat) * mu_ref[...]

    ti = lax.broadcasted_iota(jnp.int32, (group, group), 0)
    tj = lax.broadcasted_iota(jnp.int32, (group, group), 1)
    tri = jnp.where((ti >= tj) & (ti // C == tj // C), 1.0, 0.0).astype(BF16)
    t_idx = lax.broadcasted_iota(jnp.int32, (C, W), 0)
    i_idx = lax.broadcasted_iota(jnp.int32, (C, W), 1) % HEAD_DIM
    lane_head = lax.broadcasted_iota(jnp.int32, (C, W), 1) // HEAD_DIM
    cst = (
        jnp.where(t_idx == i_idx, 1.0, 0.0),
        t_idx > i_idx,
        t_idx >= i_idx,
        (t_idx // RWKV_SUB) == (i_idx // RWKV_SUB),
        [jnp.where(lane_head == h, 1.0, 0.0).astype(BF16) for h in range(RWKV_HEADS)],
    )
    carry = {"state": state[...], "turn": 0}

    def token_group(idx):
        rows = slice(idx * group, (idx + 1) * group)
        x = xs[rows]
        r = x[:, :W]
        k = x[:, W:2 * W]
        v = x[:, 2 * W:3 * W]
        wd = x[:, 3 * W:3 * W + DECAY_LORA]
        ad = x[:, 3 * W + DECAY_LORA:]
        pre = -(w0_ref[...] + lora(jnp.tanh(wd), w2_ref))
        softplus = jnp.maximum(pre, 0.0) + jnp.log(1.0 + jnp.exp(-jnp.abs(pre)))
        logw = -jnp.exp(-softplus - 0.5)
        yield
        if has_vres:
            mix = jax.nn.sigmoid(v0_ref[...] + lora(lora(v, v1_ref), v2_ref))
            v = v + (vfirst_ref[rows, :] - v) * mix
        else:
            vout_ref[rows, :] = v
        alpha = jax.nn.sigmoid(a0_ref[...] + lora(ad, a2_ref))
        kk = k * kk_ref[...]
        kk = kk / jnp.maximum(jnp.sqrt(head_sum(kk * kk)), 1e-12)
        k = k * (1.0 + (alpha - 1.0) * ka_ref[...])
        yield
        cum = _dot_exact_lhs(tri, logw, 3)
        yield
        a_t = -kk * jnp.exp(cum - logw)
        r_t = r * jnp.exp(cum)
        inv_decay = jnp.exp(-cum)
        b_raw = kk * alpha
        b_t = b_raw * inv_decay
        k_t = k * inv_decay
        chunks = [slice(j * C, (j + 1) * C) for j in range(RWKV_GROUP_CHUNKS)]
        cum_end = [cum[cr.stop - 1:cr.stop, :] for cr in chunks]
        to_end = [jnp.exp(ce - cum[cr]) for ce, cr in zip(cum_end, chunks)]
        yield
        maps = yield from _rwkv_chunk_maps(
            [a_t[cr] for cr in chunks], [r_t[cr] for cr in chunks], [b_t[cr] for cr in chunks],
            [k_t[cr] for cr in chunks], [b_raw[cr] * te for cr, te in zip(chunks, to_end)],
            [k[cr] * te for cr, te in zip(chunks, to_end)], [v[cr] for cr in chunks],
            [jnp.exp(ce) for ce in cum_end], cst)
        while carry["turn"] != idx:
            yield
        s_l = carry["state"]
        ys = []
        for lhs, y0, n_l in maps:
            out = _dot3(lhs, _block_diag(_split(s_l), cst[4]))
            ys.append(out[:C] + y0)
            s_l = out[C:] + n_l
        carry["state"] = s_l
        carry["turn"] = idx + 1
        yield
        y = jnp.concatenate(ys, axis=0)
        inv_n = 1.0 / HEAD_DIM
        mean = head_sum(y) * inv_n
        cen = y - mean
        yield
        var = head_sum(cen * cen) * inv_n
        y = cen * lax.rsqrt(var + GN_EPS) * lnw_ref[...] + lnb_ref[...]
        y = y + head_sum(r * k * rk_ref[...]) * v
        z = z_ref[rows, :]
        y_ref[rows, :] = (y * (z * jax.nn.sigmoid(z))).astype(y_ref.dtype)

    _interleave(*[token_group(idx) for idx in range(n_tok // group)], stagger=RWKV_STAGGER)
    state[...] = carry["state"]


def _rwkv(feat, z, v_first, p):
    b, t, _ = feat.shape
    n_tok = RWKV_STEP_CHUNKS * RWKV_CHUNK
    assert t % n_tok == 0 and RWKV_CHUNK == HEAD_DIM
    W = RWKV_WIDTH
    has_vres = v_first is not None
    hsum = (np.arange(W)[:, None] // HEAD_DIM == np.arange(W)[None, :] // HEAD_DIM).astype(np.float32)

    def tok(width):
        return pl.BlockSpec((None, n_tok, width), lambda bi, c: (bi, c, 0))

    def full(arr):
        return pl.BlockSpec(arr.shape, lambda bi, c: (0,) * arr.ndim)

    args = [feat, feat, z]
    specs = [tok(SHIFT_WIDTH),
             pl.BlockSpec((None, 8, SHIFT_WIDTH), lambda bi, c: (bi, jnp.maximum(c * (n_tok // 8) - 1, 0), 0)),
             tok(W)]
    if has_vres:
        args.append(v_first)
        specs.append(tok(W))
    names = ["mu", "w0", "w2", "a0", "a2", "k_k", "k_a", "r_k", "ln_w", "ln_b"]
    consts = [p[n] for n in names] + [jnp.asarray(hsum, BF16)]
    if has_vres:
        consts += [p["v0"], p["v1"], p["v2"]]
    args += consts
    specs += [full(a) for a in consts]
    y_shape = jax.ShapeDtypeStruct((b, t, W), BF16)
    if has_vres:
        out_shape, out_specs = y_shape, tok(W)
    else:
        out_shape = (y_shape, jax.ShapeDtypeStruct((b, t, W), F32))
        out_specs = (tok(W), tok(W))
    res = pl.pallas_call(
        functools.partial(_rwkv_kernel, has_vres=has_vres),
        grid=(b, t // n_tok),
        in_specs=specs,
        out_specs=out_specs,
        out_shape=out_shape,
        scratch_shapes=[pltpu.VMEM((HEAD_DIM, W), F32)],
        compiler_params=_params(("parallel", "arbitrary")),
    )(*args)
    if has_vres:
        return res, v_first
    return res[0], res[1]


def _merge_kernel(x_ref, ya_ref, yb_ref, yc_ref, mg_ref, pa_ref, pb_ref, pc_ref, wo_ref, fg_ref,
                  o_ref, *, final_norm):
    d = D_MODEL
    mixed = None
    for j, (y_ref, p_ref) in enumerate(((ya_ref, pa_ref), (yb_ref, pb_ref), (yc_ref, pc_ref))):
        gate = mg_ref[:, j * d:(j + 1) * d].astype(F32)
        term = gate * jnp.dot(y_ref[...], p_ref[...], preferred_element_type=F32)
        mixed = term if mixed is None else mixed + term
    x = x_ref[...] + jnp.dot(mixed.astype(BF16), wo_ref[...], preferred_element_type=F32)
    if final_norm:
        ms = jnp.mean(x * x, axis=-1, keepdims=True)
        x = x * lax.rsqrt(ms + NORM_EPS) * fg_ref[...]
    o_ref[...] = x


def _merge(x2d, ya, yb, yc, mg, pa, pb, pc, wo, fg, final_norm):
    m = x2d.shape[0]
    tm = ROW_TILE

    def rows(width):
        return pl.BlockSpec((tm, width), lambda i: (i, 0))

    def full(arr):
        return pl.BlockSpec(arr.shape, lambda i: (0, 0))

    return pl.pallas_call(
        functools.partial(_merge_kernel, final_norm=final_norm),
        grid=(m // tm,),
        in_specs=[rows(D_MODEL), rows(512), rows(256), rows(256), rows(3 * D_MODEL),
                  full(pa), full(pb), full(pc), full(wo), full(fg)],
        out_specs=rows(D_MODEL),
        out_shape=jax.ShapeDtypeStruct((m, D_MODEL), F32),
        compiler_params=_params(("parallel",)),
    )(x2d, ya, yb, yc, mg, pa, pb, pc, wo, fg)


def kernel(x, norm_g, w_in, b_merge, cmp_pe_k, cmp_w1_k, cmp_w2_k, cmp_pe_v, cmp_w1_v, cmp_w2_v, swa_sinks,
           rwkv_mu, rwkv_w0, rwkv_w2, rwkv_a0, rwkv_a2, rwkv_k_k, rwkv_k_a, rwkv_r_k, rwkv_ln_w, rwkv_ln_b,
           rwkv_v0, rwkv_v1, rwkv_v2, proj_a, proj_b, proj_c, w_out, final_g):
    b, t, d = x.shape
    depth = w_in.shape[0]
    m = b * t
    x2d = x.reshape(m, d)
    w_all = jnp.concatenate([jnp.zeros((depth, d, n), BF16) if start == N_IN
                             else w_in[:, :, start:start + n].astype(BF16) for start, n in _COL_RUNS], axis=2)
    pa, pb, pc, wo = (a.astype(BF16) for a in (proj_a, proj_b, proj_c, w_out))
    v_first = None
    for l in range(depth):
        outs = iter(_in_proj(x2d, norm_g[l].reshape(1, d), b_merge[l].reshape(1, -1), w_all[l]))
        seg = {}
        for name, _, _, parts in _SEGS:
            arrs = [next(outs) for _ in range(parts)]
            arrs = [a.reshape(b, t, a.shape[-1]) for a in arrs]
            seg[name] = arrs[0] if parts == 1 else arrs

        pe = jnp.stack([cmp_pe_k[l], cmp_pe_v[l]]).reshape(2, 1, CMP_BLOCK * HEAD_DIM)
        kcvc = _compress(seg["a_kv_cmp"], pe, jnp.stack([cmp_w1_k[l], cmp_w1_v[l]]),
                         jnp.stack([cmp_w2_k[l], cmp_w2_v[l]]))
        y_a, y_b = _attention(seg["a_q"], seg["a_z"], seg["a_gate"], seg["a_kv_slc"], seg["a_kv_win"], kcvc,
                              seg["b_q"], seg["b_z"], seg["b_kv"], swa_sinks[l])

        row = lambda a: a.reshape(1, -1)
        p = {"mu": row(rwkv_mu[l]), "w0": row(rwkv_w0[l]), "w2": rwkv_w2[l], "a0": row(rwkv_a0[l]),
             "a2": rwkv_a2[l], "k_k": row(rwkv_k_k[l]), "k_a": row(rwkv_k_a[l]), "r_k": row(rwkv_r_k[l]),
             "ln_w": row(rwkv_ln_w[l]), "ln_b": row(rwkv_ln_b[l])}
        if l > 0:
            p.update(v0=row(rwkv_v0[l - 1]), v1=rwkv_v1[l - 1], v2=rwkv_v2[l - 1])
        y_c, v_first = _rwkv(seg["c_shift"], seg["c_z"], v_first, p)

        x2d = _merge(x2d, y_a.reshape(m, -1), y_b.reshape(m, -1), y_c.reshape(m, -1),
                     seg["merge"].reshape(m, -1), pa[l], pb[l], pc[l], wo[l], final_g.reshape(1, d),
                     final_norm=(l == depth - 1))
    return x2d.reshape(b, t, d)
```

```python
import functools

import numpy as np
import jax
import jax.numpy as jnp
from jax import lax
from jax.experimental import pallas as pl
from jax.experimental.pallas import tpu as pltpu

F32 = jnp.float32
BF16 = jnp.bfloat16

D_MODEL = 1024
HEAD_DIM = 64
NSA_HEADS = 8
NSA_GROUPS = 2
NSA_HG = NSA_HEADS // NSA_GROUPS
CMP_STRIDE = 16
CMP_BLOCK = 32
CMP_HIDDEN = 128
SLC_BLOCK = 64
SLC_TOPN = 16
NSA_WINDOW = 512
SWA_HEADS = 4
SWA_GROUPS = 2
SWA_HG = SWA_HEADS // SWA_GROUPS
SWA_WINDOW = 128
RWKV_HEADS = 4
RWKV_WIDTH = RWKV_HEADS * HEAD_DIM
DECAY_LORA = 64
ICLR_LORA = 64
VRES_LORA = 32
SHIFT_WIDTH = 3 * RWKV_WIDTH + DECAY_LORA + ICLR_LORA
NORM_EPS = 1e-6
GN_EPS = 64e-5
NEG_INF = -1e30
FORCE = 1e9

_REF_SEGMENTS = (
    ("a_q", 512), ("a_kv_cmp", 256), ("a_kv_slc", 256), ("a_kv_win", 256), ("a_gate", 24),
    ("a_z", 512), ("b_q", 256), ("b_kv", 256), ("b_z", 256), ("c_shift", SHIFT_WIDTH),
    ("c_z", 256), ("merge", 3 * D_MODEL),
)
N_IN = sum(w for _, w in _REF_SEGMENTS)

VMEM_LIMIT = 56 * 1024 * 1024
ROW_TILE = 256
ATT_TILE = 256
ATT_ROWS = 128
RWKV_CHUNK = 64
RWKV_STEP_CHUNKS = 8
RWKV_GROUP_CHUNKS = 2
RWKV_STAGGER = 4
RWKV_SUB = 16


def _ref_offsets():
    out, off = {}, 0
    for name, width in _REF_SEGMENTS:
        out[name] = off
        off += width
    return out


def _kv_interleave(base):
    idx = []
    for g in range(2):
        idx += list(range(base + 128 + g * 64, base + 128 + (g + 1) * 64))
        idx += list(range(base + g * 64, base + (g + 1) * 64))
    return idx


def _projection_layout():
    o = _ref_offsets()
    segs, idx = [], []

    def add(name, cols, dtype, parts=1):
        segs.append((name, len(cols), dtype, parts))
        idx.extend(cols)

    add("a_q", list(range(o["a_q"], o["a_q"] + 512)), BF16)
    add("a_z", list(range(o["a_z"], o["a_z"] + 512)), F32)
    add("a_kv_cmp", list(range(o["a_kv_cmp"], o["a_kv_cmp"] + 256)), BF16, parts=4)
    add("a_kv_slc", _kv_interleave(o["a_kv_slc"]), BF16)
    add("a_kv_win", _kv_interleave(o["a_kv_win"]), BF16)
    gate = []
    for g in range(NSA_GROUPS):
        cols = [N_IN] * 128
        for br in range(3):
            for h in range(NSA_HG):
                cols[br * NSA_HG + h] = o["a_gate"] + br * NSA_HEADS + g * NSA_HG + h
        gate += cols
    add("a_gate", gate, F32)
    add("b_q", list(range(o["b_q"], o["b_q"] + 256)), BF16)
    add("b_kv", _kv_interleave(o["b_kv"]), BF16)
    add("b_z", list(range(o["b_z"], o["b_z"] + 256)), F32)
    add("c_shift", list(range(o["c_shift"], o["c_shift"] + SHIFT_WIDTH)), F32)
    add("c_z", list(range(o["c_z"], o["c_z"] + 256)), F32)
    return tuple(segs), np.asarray(idx, np.int32)


_SEGS, _COL_IDX = _projection_layout()
_N_PROJ = int(_COL_IDX.shape[0])


def _column_runs(idx):
    runs, i = [], 0
    while i < len(idx):
        j = i + 1
        while j < len(idx) and (idx[j] == idx[j - 1] + 1 if idx[i] != N_IN else idx[j] == N_IN):
            j += 1
        runs.append((int(idx[i]), j - i))
        i = j
    return runs


_COL_RUNS = _column_runs(_COL_IDX)


def _params(sem):
    return pltpu.CompilerParams(dimension_semantics=sem, vmem_limit_bytes=VMEM_LIMIT)


def _in_proj_kernel(x_ref, g_ref, w_ref, *out_refs):
    x = x_ref[...]
    ms = jnp.mean(x * x, axis=-1, keepdims=True)
    xn = (x * lax.rsqrt(ms + NORM_EPS) * g_ref[...]).astype(BF16)
    off = 0
    refs = iter(out_refs)
    for _, width, _, parts in _SEGS:
        outs = [next(refs) for _ in range(parts)]
        for c0 in range(0, width, 512):
            cw = min(512, width - c0)
            val = jnp.dot(xn, w_ref[:, off + c0:off + c0 + cw], preferred_element_type=F32)
            if parts == 1:
                outs[0][:, c0:c0 + cw] = val.astype(outs[0].dtype)
            else:
                pw = width // parts
                for j, o_ref in enumerate(outs):
                    o_ref[...] = val[:, j * pw:(j + 1) * pw].astype(o_ref.dtype)
        off += width


def _in_proj(x2d, g, w):
    m = x2d.shape[0]
    tm = ROW_TILE
    widths = [(width // parts, dt) for _, width, dt, parts in _SEGS for _ in range(parts)]
    assert all(width <= 512 for _, width, _, parts in _SEGS if parts > 1)
    out_shape = [jax.ShapeDtypeStruct((m, width), dt) for width, dt in widths]
    out_specs = [pl.BlockSpec((tm, width), lambda i: (i, 0)) for width, _ in widths]
    return pl.pallas_call(
        _in_proj_kernel,
        grid=(m // tm,),
        in_specs=[
            pl.BlockSpec((tm, D_MODEL), lambda i: (i, 0)),
            pl.BlockSpec((1, D_MODEL), lambda i: (0, 0)),
            pl.BlockSpec((D_MODEL, _N_PROJ), lambda i: (0, 0), pipeline_mode=pl.Buffered(1)),
        ],
        out_specs=out_specs,
        out_shape=out_shape,
        compiler_params=_params(("parallel",)),
    )(x2d, g, w)


def _compress_kernel(*refs):
    z_refs, (pe_ref, w1_ref, w2_ref, o_ref) = refs[:-4], refs[-4:]
    half = CMP_STRIDE * HEAD_DIM
    for j, z_ref in enumerate(z_refs):
        kv = j // NSA_GROUPS
        z = z_ref[...]
        w1 = w1_ref[kv]
        first = _dot_exact_lhs(z, w1[:half], 3)
        second = _dot_exact_lhs(z, w1[half:], 3)
        n = z.shape[0]
        hid = first + pltpu.roll(second, n - 1, 0)
        hid = hid + _dot3(_split(pe_ref[kv]), _split(w1))
        act = hid * jax.nn.sigmoid(hid)
        o_ref[j] = _dot3(_split(act), _split(w2_ref[kv]))


def _compress(kv_cmp, pe, w1, w2):
    b, t, _ = kv_cmp[0].shape
    nch = t // CMP_STRIDE
    zs = [a.reshape(b, nch, CMP_STRIDE * HEAD_DIM) for a in kv_cmp]

    def full(arr):
        return pl.BlockSpec(arr.shape, lambda i: (0,) * arr.ndim)

    return pl.pallas_call(
        _compress_kernel,
        grid=(b,),
        in_specs=[pl.BlockSpec((None, nch, CMP_STRIDE * HEAD_DIM), lambda i: (i, 0, 0)) for _ in zs]
        + [full(pe), full(w1), full(w2)],
        out_specs=pl.BlockSpec((None, len(zs), nch, HEAD_DIM), lambda i: (i, 0, 0, 0)),
        out_shape=jax.ShapeDtypeStruct((b, len(zs), nch, HEAD_DIM), F32),
        compiler_params=_params(("parallel",)),
    )(*zs, pe, w1, w2)


def _widen_queries(q, n_heads):
    zeros = jnp.zeros((q.shape[0], HEAD_DIM), q.dtype)
    return [jnp.concatenate([zeros, q[:, h * HEAD_DIM:(h + 1) * HEAD_DIM]], axis=1) for h in range(n_heads)]


def _pair_values(kv):
    swapped = jnp.concatenate([kv[:, HEAD_DIM:], kv[:, :HEAD_DIM]], axis=1)
    lane = lax.broadcasted_iota(jnp.int32, kv.shape, 1)
    low = jnp.where(lane < HEAD_DIM, 1.0, 0.0).astype(kv.dtype)
    high = jnp.where(lane < HEAD_DIM, 0.0, 1.0).astype(kv.dtype)
    even = jnp.concatenate([kv * low, low], axis=1)
    odd = jnp.concatenate([swapped * high, high], axis=1)
    return even, odd


def _pair_lanes(even, odd):
    low = lax.broadcasted_iota(jnp.int32, even.shape, 1) < HEAD_DIM
    return jnp.where(low, even, odd)


def _nt(a, b):
    return lax.dot_general(a, b, (((1,), (1,)), ((), ())), preferred_element_type=F32)


def _rounds(streams, stagger=0):
    results = [None] * len(streams)
    waiting = list(enumerate(streams))
    live = {}
    rounds = 0
    while waiting or live:
        while waiting and waiting[0][0] * stagger <= rounds:
            idx, stream = waiting.pop(0)
            live[idx] = stream
        for idx in list(live):
            try:
                next(live[idx])
            except StopIteration as stop:
                results[idx] = stop.value
                del live[idx]
        rounds += 1
        yield
    return results


def _interleave(*streams, stagger=0):
    rounds = _rounds(streams, stagger)
    while True:
        try:
            next(rounds)
        except StopIteration as stop:
            return stop.value


def _band_block(qa_even, qa_odd, keys, val_even, val_odd, bias, sinks=None):
    s = [_nt(qa, keys) + bias for qa in (qa_even, qa_odd)]
    yield
    m = [jnp.max(x, axis=-1, keepdims=True) for x in s]
    if sinks is not None:
        m = [jnp.maximum(x, sinks[e]) for e, x in enumerate(m)]
    p = [jnp.exp(x - mx).astype(BF16) for x, mx in zip(s, m)]
    yield
    acc = (jnp.dot(p[0], val_even, preferred_element_type=F32)
           + jnp.dot(p[1], val_odd, preferred_element_type=F32))
    denom = acc[:, 2 * HEAD_DIM:]
    if sinks is not None:
        shape = (acc.shape[0], 2 * HEAD_DIM)
        denom = denom + _pair_lanes(jnp.broadcast_to(jnp.exp(sinks[0] - m[0]), shape),
                                    jnp.broadcast_to(jnp.exp(sinks[1] - m[1]), shape))
    return acc[:, :2 * HEAD_DIM] / denom


def _band_blocks(blocks, sinks=None):
    return (yield from _rounds([_band_block(*blk, sinks) for blk in blocks], stagger=1))


def _band_bias_table(deltas, rows, width, window):
    r = np.arange(rows)[:, None]
    c = np.arange(width)[None, :]
    out = [np.where((c <= r + d) & (c > r + d - window), 0.0, NEG_INF) for d in deltas]
    return np.stack(out).astype(np.float32)


def _attention_kernel(sink_ref, q_ref, z_ref, gate_ref, kvs_ref, kvw_ref, kcvc_ref, bq_ref, bz_ref,
                      bkv_ref, negexp_ref, ovl_ref, wbias_ref, cbias_ref, gexp_ref, bbias_ref,
                      o_ref, ob_ref, srhs_sc, spe_sc, spo_sc, wpe_sc, wpo_sc, bpe_sc, bpo_sc, m_sc, acc_sc):
    i = pl.program_id(1)
    tq = ATT_TILE
    rb = ATT_ROWS
    n_rb = tq // rb
    n_cmp_pad = kcvc_ref.shape[1]
    n_slc = ovl_ref.shape[0]
    n_pairs = NSA_HG // 2
    groups = range(NSA_GROUPS)
    wd = NSA_HG * HEAD_DIM
    wb = SWA_HG * HEAD_DIM
    kv_lanes = [slice(g * 2 * HEAD_DIM, (g + 1) * 2 * HEAD_DIM) for g in groups]

    @pl.when(i == 0)
    def _():
        for g in groups:
            kvs = kvs_ref[:, kv_lanes[g]]
            srhs_sc[g] = jnp.concatenate([kvs, negexp_ref[...]], axis=1)
            spe_sc[g], spo_sc[g] = _pair_values(kvs)
            wpe_sc[g], wpo_sc[g] = _pair_values(kvw_ref[:, kv_lanes[g]])
            bpe_sc[g], bpo_sc[g] = _pair_values(bkv_ref[:, kv_lanes[g]])

    scale = jnp.asarray(HEAD_DIM ** -0.5, BF16)
    qas = [_widen_queries(q_ref[:, g * wd:(g + 1) * wd] * scale, NSA_HG) for g in groups]

    def band_inputs(g, kv_ref, pe_sc, po_sc, bias_ref, window, r0):
        q0 = i * tq + r0
        k0 = pl.multiple_of(jnp.maximum(q0 - window, 0), rb)
        width = window + rb
        variant = jnp.minimum(q0 // rb, bias_ref.shape[0] - 1)
        return (kv_ref[pl.ds(k0, width), kv_lanes[g]], pe_sc[g, pl.ds(k0, width), :],
                po_sc[g, pl.ds(k0, width), :], bias_ref[variant])

    def window_stream(g):
        bands = [band_inputs(g, kvw_ref, wpe_sc, wpo_sc, wbias_ref, NSA_WINDOW, r0) for r0 in range(0, tq, rb)]
        blocks = [(qas[g][2 * pair][r0:r0 + rb], qas[g][2 * pair + 1][r0:r0 + rb]) + bands[r0 // rb]
                  for pair in range(n_pairs) for r0 in range(0, tq, rb)]
        outs = yield from _band_blocks(blocks)
        return jnp.concatenate([jnp.concatenate(outs[pair * n_rb:(pair + 1) * n_rb], axis=0)
                                for pair in range(n_pairs)], axis=1)

    def sink_window_stream(g):
        lanes = slice(g * wb, (g + 1) * wb)
        qbs = _widen_queries(bq_ref[:, lanes] * scale, SWA_HG)
        sinks = (sink_ref[g * SWA_HG], sink_ref[g * SWA_HG + 1])
        blocks = [(qbs[0][r0:r0 + rb], qbs[1][r0:r0 + rb])
                  + band_inputs(g, bkv_ref, bpe_sc, bpo_sc, bbias_ref, SWA_WINDOW, r0) for r0 in range(0, tq, rb)]
        outs = yield from _band_blocks(blocks, sinks)
        y = jnp.concatenate(outs, axis=0)
        zb = bz_ref[:, lanes]
        ob_ref[:, lanes] = (y * (zb * jax.nn.sigmoid(zb))).astype(ob_ref.dtype)

    def compressed_stream(g):
        zeros_k = jnp.zeros((n_cmp_pad, HEAD_DIM), BF16)
        kc_both = jnp.concatenate([jnp.concatenate([zeros_k, t], axis=1) for t in _split(kcvc_ref[g])], axis=0)
        vc = kcvc_ref[NSA_GROUPS + g].astype(BF16)
        vc_even = jnp.concatenate([vc, zeros_k], axis=1)
        vc_odd = jnp.concatenate([zeros_k, vc], axis=1)
        tpos = i * tq + lax.broadcasted_iota(jnp.int32, (tq, n_cmp_pad), 0)
        cmp_end = lax.broadcasted_iota(jnp.int32, (tq, n_cmp_pad), 1) * CMP_STRIDE + (CMP_BLOCK - 1)
        valid_c = cmp_end <= tpos
        s = [_nt(qa, kc_both) for qa in qas[g]]
        s = [x[:, :n_cmp_pad] + x[:, n_cmp_pad:] for x in s]
        yield
        s = [jnp.where(valid_c, x, NEG_INF) for x in s]
        m = [jnp.max(x, axis=-1, keepdims=True) for x in s]
        yield
        p = [jnp.where(valid_c, jnp.exp(x - mx), 0.0) for x, mx in zip(s, m)]
        denom = [jnp.sum(x, axis=-1, keepdims=True) for x in p]
        yield
        p = [x * (1.0 / jnp.where(d > 0, d, 1.0)) for x, d in zip(p, denom)]
        p_sum = (p[0] + p[1]) + (p[2] + p[3])
        pb = [x.astype(BF16) for x in p]
        o_cmp = jnp.concatenate(
            [jnp.dot(pb[2 * pair], vc_even, preferred_element_type=F32)
             + jnp.dot(pb[2 * pair + 1], vc_odd, preferred_element_type=F32) for pair in range(n_pairs)], axis=1)
        yield
        imp = None
        rest = p_sum
        for _ in range(3):
            term = rest.astype(BF16)
            rest = rest - term.astype(F32)
            part = _nt(ovl_ref[...], term)
            imp = part if imp is None else imp + part
        yield
        blk = lax.broadcasted_iota(jnp.int32, (n_slc, tq), 0)
        cur = (i * tq + lax.broadcasted_iota(jnp.int32, (n_slc, tq), 1)) // SLC_BLOCK
        forced = (blk == 0) | (blk == cur) | (blk == cur - 1)
        score = jnp.where(forced, FORCE, jnp.where(blk <= cur, imp, -FORCE))
        ranks = [jnp.zeros((n_slc, tq), F32) for _ in range(4)]
        for r in range(n_slc):
            row = score[r:r + 1, :]
            ahead = (row > score) | ((row == score) & (r < blk))
            ranks[r % 4] = ranks[r % 4] + jnp.where(ahead, 1.0, 0.0)
            if r % 4 == 3:
                yield
        rank = (ranks[0] + ranks[1]) + (ranks[2] + ranks[3])
        n_sel = min(SLC_TOPN, n_slc)
        unsel_t = jnp.where((rank < n_sel) & (blk <= cur), 0.0, 1.0)
        unsel_t = jnp.concatenate([unsel_t, jnp.zeros((128 - n_slc, tq), F32)], axis=0)
        return o_cmp, jnp.transpose(unsel_t).astype(BF16)

    results = _interleave(*[stream(g) for g in groups
                            for stream in (window_stream, sink_window_stream, compressed_stream)])
    o_win = [results[3 * g] for g in groups]
    o_cmp = [results[3 * g + 2][0] for g in groups]
    unsel = [results[3 * g + 2][1] for g in groups]

    lhs = {(g, h): jnp.concatenate([qas[g][h], unsel[g]], axis=1) for g in groups for h in range(NSA_HG)}
    m_sc[...] = jnp.full(m_sc.shape, NEG_INF, F32)
    acc_sc[...] = jnp.zeros(acc_sc.shape, F32)
    subs = [(g, h, r0) for g in groups for h in range(NSA_HG) for r0 in range(0, tq, rb)]

    def slc_keys(start, width, bias_lanes):
        rhs = [srhs_sc[g, pl.ds(start, width), :] for g in groups]
        val_even = [spe_sc[g, pl.ds(start, width), :] for g in groups]
        val_odd = [spo_sc[g, pl.ds(start, width), :] for g in groups]
        s = {}
        for g, h, r0 in subs:
            s[g, h, r0] = _nt(lhs[g, h][r0:r0 + rb], rhs[g])
            if bias_lanes is not None:
                s[g, h, r0] = s[g, h, r0] + cbias_ref[r0:r0 + rb, bias_lanes:bias_lanes + width]
        m_prev = {(g, h, r0): m_sc[g * NSA_HG + h, r0:r0 + rb] for g, h, r0 in subs}
        m_next = {key: jnp.maximum(m_prev[key], jnp.max(s[key], axis=-1, keepdims=True)) for key in subs}
        p = {key: jnp.exp(s[key] - jnp.concatenate([m_next[key]] * (width // 128), axis=1)).astype(BF16)
             for key in subs}
        alpha = {key: jnp.exp(m_prev[key] - m_next[key]) for key in subs}
        for g, h, r0 in subs:
            m_sc[g * NSA_HG + h, r0:r0 + rb] = m_next[g, h, r0]
        for g in groups:
            for pair in range(n_pairs):
                for r0 in range(0, tq, rb):
                    rows = slice(r0, r0 + rb)
                    decay = _pair_lanes(alpha[g, 2 * pair, r0], alpha[g, 2 * pair + 1, r0])
                    decay = jnp.concatenate([decay, decay], axis=1)
                    acc_sc[g * n_pairs + pair, rows] = (
                        decay * acc_sc[g * n_pairs + pair, rows]
                        + jnp.dot(p[g, 2 * pair, r0], val_even[g], preferred_element_type=F32)
                        + jnp.dot(p[g, 2 * pair + 1, r0], val_odd[g], preferred_element_type=F32))

    def wide(j, carry):
        slc_keys(pl.multiple_of(j * (2 * tq), 2 * tq), 2 * tq, None)
        return carry

    lax.fori_loop(0, i // 2, wide, 0)

    @pl.when(i % 2 == 0)
    def _():
        slc_keys(pl.multiple_of(i * tq, tq), tq, tq)

    @pl.when(i % 2 == 1)
    def _():
        slc_keys(pl.multiple_of((i - 1) * tq, 2 * tq), 2 * tq, 0)

    for g in groups:
        o_slc = []
        for pair in range(n_pairs):
            acc = acc_sc[g * n_pairs + pair]
            o_slc.append(acc[:, :2 * HEAD_DIM] / acc[:, 2 * HEAD_DIM:])
        o_slc = jnp.concatenate(o_slc, axis=1)
        sg = _split(jax.nn.sigmoid(gate_ref[:, g * 128:(g + 1) * 128]))
        gates = (jnp.dot(sg[0], gexp_ref[...], preferred_element_type=F32)
                 + jnp.dot(sg[1], gexp_ref[...], preferred_element_type=F32))
        y = gates[:, :wd] * o_cmp[g] + gates[:, wd:2 * wd] * o_slc + gates[:, 2 * wd:] * o_win[g]
        z = z_ref[:, g * wd:(g + 1) * wd]
        o_ref[:, g * wd:(g + 1) * wd] = (y * (z * jax.nn.sigmoid(z))).astype(o_ref.dtype)


def _attention(q, z, gate, kv_slc, kv_win, kcvc, bq, bz, bkv, sinks):
    b, t, _ = q.shape
    tq = ATT_TILE
    assert t % (2 * tq) == 0 and t >= NSA_WINDOW + tq and NSA_WINDOW % tq == 0 and SWA_HG == 2
    assert NSA_GROUPS == SWA_GROUPS
    n_slc = t // SLC_BLOCK
    nch = t // CMP_STRIDE
    wd = NSA_HG * HEAD_DIM
    negexp = np.where(np.arange(t)[:, None] // SLC_BLOCK == np.arange(128)[None, :], -2.0 ** 100, 0.0)
    ci = np.arange(nch)[None, :] * CMP_STRIDE
    sj = np.arange(n_slc)[:, None] * SLC_BLOCK
    overlap_t = ((ci < sj + SLC_BLOCK) & (ci + CMP_BLOCK > sj)).astype(np.float32)
    rb = ATT_ROWS
    assert NSA_WINDOW % rb == 0 and SWA_WINDOW % rb == 0 and tq % rb == 0
    wbias = _band_bias_table(range(0, NSA_WINDOW + 1, rb), rb, NSA_WINDOW + rb, NSA_WINDOW)
    cbias = np.concatenate([np.zeros((tq, tq), np.float32), _band_bias_table([0], tq, tq, t)[0]], axis=1)
    bbias = _band_bias_table(range(0, SWA_WINDOW + 1, rb), rb, SWA_WINDOW + rb, SWA_WINDOW)
    gexp = np.zeros((128, 3 * wd), np.float32)
    for br in range(3):
        for h in range(NSA_HG):
            gexp[br * NSA_HG + h, br * wd + h * HEAD_DIM:br * wd + (h + 1) * HEAD_DIM] = 1.0

    def const(arr):
        return pl.BlockSpec(arr.shape, lambda bi, i: (0,) * arr.ndim)

    def tile(arr):
        return pl.BlockSpec((None, tq, arr.shape[-1]), lambda bi, i: (bi, i, 0))

    def whole(arr):
        return pl.BlockSpec((None,) + arr.shape[1:], lambda bi, i: (bi,) + (0,) * (arr.ndim - 1))

    consts = [jnp.asarray(negexp, BF16), jnp.asarray(overlap_t, BF16), jnp.asarray(wbias),
              jnp.asarray(cbias), jnp.asarray(gexp, BF16), jnp.asarray(bbias)]
    key_scratch = pltpu.VMEM((NSA_GROUPS, t, 256), BF16)
    return pl.pallas_call(
        _attention_kernel,
        grid=(b, t // tq),
        in_specs=[pl.BlockSpec(memory_space=pltpu.SMEM), tile(q), tile(z), tile(gate), whole(kv_slc),
                  whole(kv_win), whole(kcvc), tile(bq), tile(bz), whole(bkv)] + [const(a) for a in consts],
        out_specs=(pl.BlockSpec((None, tq, 512), lambda bi, i: (bi, i, 0)),
                   pl.BlockSpec((None, tq, 256), lambda bi, i: (bi, i, 0))),
        out_shape=(jax.ShapeDtypeStruct((b, t, 512), BF16), jax.ShapeDtypeStruct((b, t, 256), BF16)),
        scratch_shapes=[key_scratch] * 7 + [
            pltpu.VMEM((NSA_GROUPS * NSA_HG, tq, 128), F32),
            pltpu.VMEM((NSA_GROUPS * NSA_HG // 2, tq, 256), F32),
        ],
        compiler_params=_params(("parallel", "arbitrary")),
    )(sinks, q, z, gate, kv_slc, kv_win, kcvc, bq, bz, bkv, *consts)


_NN = (((1,), (0,)), ((), ()))
_NT = (((1,), (1,)), ((), ()))


def _split(a):
    hi = a.astype(BF16)
    lo = (a - hi.astype(F32)).astype(BF16)
    return hi, lo


def _dot(a, b, dims=_NN):
    return lax.dot_general(a, b, dims, preferred_element_type=F32)


def _dot3(a, b, dims=_NN):
    rows = a[0].shape[0]
    both = _dot(jnp.concatenate([a[0], a[1]], axis=0), b[0], dims)
    return (both[:rows] + both[rows:]) + _dot(a[0], b[1], dims)


def _dot_exact_lhs(a, b, n_terms):
    out = None
    for _ in range(n_terms):
        term = b.astype(BF16)
        b = b - term.astype(F32)
        part = _dot(a, term)
        out = part if out is None else out + part
    return out


def _dot_exact_rhs(x, ones):
    hi, lo = _split(x)
    return _dot(hi, ones) + _dot(lo, ones)


def _block_diag(pair, head_masks):
    return tuple(jnp.concatenate([x * m for m in head_masks], axis=0) for x in pair)


def _each(fn, *lists):
    return [fn(*items) for items in zip(*lists)]


def _unit_lower_inverse(lows, eye, same_sub, bd):
    diag = _each(lambda low: jnp.where(same_sub, low, 0.0), lows)
    off = _each(lambda low, d: low - d, lows, diag)
    inv = _each(lambda d: eye + d, diag)
    power = diag
    span = 2
    while span < RWKV_SUB:
        ps = _each(_split, power)
        power = _each(lambda p: _dot3(p, bd(p)), ps)
        yield
        inv = _each(lambda iv, pw: _dot3(_split(iv), bd(_split(eye + pw))), inv, power)
        yield
        span *= 2
    inv_s = _each(_split, inv)
    cross = _each(lambda iv, o: _dot3(iv, bd(_split(o))), inv_s, off)
    yield
    total = _each(lambda cr: eye + cr, cross)
    power = cross
    span = 2
    while span < RWKV_CHUNK // RWKV_SUB:
        ps = _each(_split, power)
        power = _each(lambda p: _dot3(p, bd(p)), ps)
        yield
        total = _each(lambda tt, pw: _dot3(_split(tt), bd(_split(eye + pw))), total, power)
        yield
        span *= 2
    return _each(lambda tt, iv: _dot3(_split(tt), bd(iv)), total, inv_s)


def _rwkv_chunk_maps(a_t, r_t, b_t, k_t, b_e, k_e, v, p_end, cst):
    C = RWKV_CHUNK
    eye, strict, incl, same_sub, head_masks = cst
    bd = lambda pair: _block_diag(pair, head_masks)

    ar = _each(lambda a, r: _split(jnp.concatenate([a, r], axis=0)), a_t, r_t)
    g_b = _each(lambda x, b: _dot3(x, bd(_split(b)), _NT), ar, b_t)
    g_k = _each(lambda x, k: _dot3(x, bd(_split(k)), _NT), ar, k_t)
    yield
    a_ab = _each(lambda g: jnp.where(strict, g[:C], 0.0), g_b)
    a_rb = _each(lambda g: _split(jnp.where(incl, g[C:], 0.0)), g_b)
    a_ak = _each(lambda g: _split(jnp.where(strict, g[:C], 0.0)), g_k)
    a_rk = _each(lambda g: _split(jnp.where(incl, g[C:], 0.0)), g_k)
    v_bd = _each(lambda x: bd(_split(x)), v)
    akv = _each(_dot3, a_ak, v_bd)
    yield
    t_inv = yield from _unit_lower_inverse(a_ab, eye, same_sub, bd)
    t_inv = _each(_split, t_inv)
    yield
    w_m = _each(lambda t, a: _dot3(t, bd(_split(a))), t_inv, a_t)
    u0 = _each(lambda t, x: _dot3(t, bd(_split(x))), t_inv, akv)
    yield
    r_m = _each(lambda r, g, w: r + _dot3(g, bd(_split(w))), r_t, a_rb, w_m)
    y0 = _each(lambda g, u, gk, vb: _dot3(g, bd(_split(u))) + _dot3(gk, vb), a_rb, u0, a_rk, v_bd)
    yield
    bk = _each(lambda b, k: _split(jnp.transpose(jnp.concatenate([b, k], axis=0))), b_e, k_e)
    m_add = _each(lambda x, w: _dot3((x[0][:, :C], x[1][:, :C]), _split(w)), bk, w_m)
    n_all = _each(lambda x, u, vv: _dot3(x, _split(jnp.concatenate([u, vv], axis=0))), bk, u0, v)
    yield
    keep = [m.astype(F32) for m in head_masks]

    def own_blocks(full):
        parts = [full[h * HEAD_DIM:(h + 1) * HEAD_DIM, :] * keep[h] for h in range(RWKV_HEADS)]
        return (parts[0] + parts[1]) + (parts[2] + parts[3])

    m_l = _each(lambda m, p: own_blocks(m) + eye * p, m_add, p_end)
    n_l = _each(own_blocks, n_all)
    lhs = _each(lambda r, m: _split(jnp.concatenate([r, m], axis=0)), r_m, m_l)
    return list(zip(lhs, y0, n_l))


def _rwkv_kernel(*refs, has_vres):
    if has_vres:
        (feat_ref, prev_ref, z_ref, vfirst_ref, mu_ref, w0_ref, w2_ref, a0_ref, a2_ref, kk_ref, ka_ref,
         rk_ref, lnw_ref, lnb_ref, hsum_ref, v0_ref, v1_ref, v2_ref, y_ref, state) = refs
    else:
        (feat_ref, prev_ref, z_ref, mu_ref, w0_ref, w2_ref, a0_ref, a2_ref, kk_ref, ka_ref,
         rk_ref, lnw_ref, lnb_ref, hsum_ref, y_ref, vout_ref, state) = refs
    c = pl.program_id(1)
    C = RWKV_CHUNK
    W = RWKV_WIDTH
    n_tok = feat_ref.shape[0]
    group = RWKV_GROUP_CHUNKS * C

    @pl.when(c == 0)
    def _():
        state[...] = jnp.zeros(state.shape, F32)

    def lora(x, w_ref):
        return _dot3(_split(x), _split(w_ref[...]))

    hsum = hsum_ref[...]

    def head_sum(x):
        return _dot_exact_rhs(x, hsum)

    feat = feat_ref[...]
    row = lax.broadcasted_iota(jnp.int32, feat.shape, 0)
    last_prev = jnp.where(c == 0, 0.0, prev_ref[7:8, :])
    prev = jnp.where(row == 0, last_prev, pltpu.roll(feat, 1, 0))
    xs = feat + (prev - feat) * mu_ref[...]

    ti = lax.broadcasted_iota(jnp.int32, (group, group), 0)
    tj = lax.broadcasted_iota(jnp.int32, (group, group), 1)
    tri = jnp.where((ti >= tj) & (ti // C == tj // C), 1.0, 0.0).astype(BF16)
    t_idx = lax.broadcasted_iota(jnp.int32, (C, W), 0)
    i_idx = lax.broadcasted_iota(jnp.int32, (C, W), 1) % HEAD_DIM
    lane_head = lax.broadcasted_iota(jnp.int32, (C, W), 1) // HEAD_DIM
    cst = (
        jnp.where(t_idx == i_idx, 1.0, 0.0),
        t_idx > i_idx,
        t_idx >= i_idx,
        (t_idx // RWKV_SUB) == (i_idx // RWKV_SUB),
        [jnp.where(lane_head == h, 1.0, 0.0).astype(BF16) for h in range(RWKV_HEADS)],
    )
    carry = {"state": state[...], "turn": 0}

    def token_group(idx):
        rows = slice(idx * group, (idx + 1) * group)
        x = xs[rows]
        r = x[:, :W]
        k = x[:, W:2 * W]
        v = x[:, 2 * W:3 * W]
        wd = x[:, 3 * W:3 * W + DECAY_LORA]
        ad = x[:, 3 * W + DECAY_LORA:]
        pre = -(w0_ref[...] + lora(jnp.tanh(wd), w2_ref))
        softplus = jnp.maximum(pre, 0.0) + jnp.log(1.0 + jnp.exp(-jnp.abs(pre)))
        logw = -jnp.exp(-softplus - 0.5)
        yield
        if has_vres:
            mix = jax.nn.sigmoid(v0_ref[...] + lora(lora(v, v1_ref), v2_ref))
            v = v + (vfirst_ref[rows, :] - v) * mix
        else:
            vout_ref[rows, :] = v
        alpha = jax.nn.sigmoid(a0_ref[...] + lora(ad, a2_ref))
        kk = k * kk_ref[...]
        kk = kk / jnp.maximum(jnp.sqrt(head_sum(kk * kk)), 1e-12)
        k = k * (1.0 + (alpha - 1.0) * ka_ref[...])
        yield
        cum = _dot_exact_lhs(tri, logw, 3)
        yield
        a_t = -kk * jnp.exp(cum - logw)
        r_t = r * jnp.exp(cum)
        inv_decay = jnp.exp(-cum)
        b_raw = kk * alpha
        b_t = b_raw * inv_decay
        k_t = k * inv_decay
        chunks = [slice(j * C, (j + 1) * C) for j in range(RWKV_GROUP_CHUNKS)]
        cum_end = [cum[cr.stop - 1:cr.stop, :] for cr in chunks]
        to_end = [jnp.exp(ce - cum[cr]) for ce, cr in zip(cum_end, chunks)]
        yield
        maps = yield from _rwkv_chunk_maps(
            [a_t[cr] for cr in chunks], [r_t[cr] for cr in chunks], [b_t[cr] for cr in chunks],
            [k_t[cr] for cr in chunks], [b_raw[cr] * te for cr, te in zip(chunks, to_end)],
            [k[cr] * te for cr, te in zip(chunks, to_end)], [v[cr] for cr in chunks],
            [jnp.exp(ce) for ce in cum_end], cst)
        while carry["turn"] != idx:
            yield
        s_l = carry["state"]
        ys = []
        for lhs, y0, n_l in maps:
            out = _dot3(lhs, _block_diag(_split(s_l), cst[4]))
            ys.append(out[:C] + y0)
            s_l = out[C:] + n_l
        carry["state"] = s_l
        carry["turn"] = idx + 1
        yield
        y = jnp.concatenate(ys, axis=0)
        inv_n = 1.0 / HEAD_DIM
        mean = head_sum(y) * inv_n
        cen = y - mean
        yield
        var = head_sum(cen * cen) * inv_n
        y = cen * lax.rsqrt(var + GN_EPS) * lnw_ref[...] + lnb_ref[...]
        y = y + head_sum(r * k * rk_ref[...]) * v
        z = z_ref[rows, :]
        y_ref[rows, :] = (y * (z * jax.nn.sigmoid(z))).astype(y_ref.dtype)

    _interleave(*[token_group(idx) for idx in range(n_tok // group)], stagger=RWKV_STAGGER)
    state[...] = carry["state"]


def _rwkv(feat, z, v_first, p):
    b, t, _ = feat.shape
    n_tok = RWKV_STEP_CHUNKS * RWKV_CHUNK
    assert t % n_tok == 0 and RWKV_CHUNK == HEAD_DIM
    W = RWKV_WIDTH
    has_vres = v_first is not None
    hsum = (np.arange(W)[:, None] // HEAD_DIM == np.arange(W)[None, :] // HEAD_DIM).astype(np.float32)

    def tok(width):
        return pl.BlockSpec((None, n_tok, width), lambda bi, c: (bi, c, 0))

    def full(arr):
        return pl.BlockSpec(arr.shape, lambda bi, c: (0,) * arr.ndim)

    args = [feat, feat, z]
    specs = [tok(SHIFT_WIDTH),
             pl.BlockSpec((None, 8, SHIFT_WIDTH), lambda bi, c: (bi, jnp.maximum(c * (n_tok // 8) - 1, 0), 0)),
             tok(W)]
    if has_vres:
        args.append(v_first)
        specs.append(tok(W))
    names = ["mu", "w0", "w2", "a0", "a2", "k_k", "k_a", "r_k", "ln_w", "ln_b"]
    consts = [p[n] for n in names] + [jnp.asarray(hsum, BF16)]
    if has_vres:
        consts += [p["v0"], p["v1"], p["v2"]]
    args += consts
    specs += [full(a) for a in consts]
    y_shape = jax.ShapeDtypeStruct((b, t, W), BF16)
    if has_vres:
        out_shape, out_specs = y_shape, tok(W)
    else:
        out_shape = (y_shape, jax.ShapeDtypeStruct((b, t, W), F32))
        out_specs = (tok(W), tok(W))
    res = pl.pallas_call(
        functools.partial(_rwkv_kernel, has_vres=has_vres),
        grid=(b, t // n_tok),
        in_specs=specs,
        out_specs=out_specs,
        out_shape=out_shape,
        scratch_shapes=[pltpu.VMEM((HEAD_DIM, W), F32)],
        compiler_params=_params(("parallel", "arbitrary")),
    )(*args)
    if has_vres:
        return res, v_first
    return res[0], res[1]


def _merge_kernel(x_ref, ya_ref, yb_ref, yc_ref, g_ref, wm_ref, bm_ref, pa_ref, pb_ref, pc_ref, wo_ref, fg_ref,
                  o_ref, *, final_norm):
    d = D_MODEL
    x = x_ref[...]
    ms = jnp.mean(x * x, axis=-1, keepdims=True)
    xn = (x * lax.rsqrt(ms + NORM_EPS) * g_ref[...]).astype(BF16)
    mixed = None
    for j, (y_ref, p_ref) in enumerate(((ya_ref, pa_ref), (yb_ref, pb_ref), (yc_ref, pc_ref))):
        cols = slice(j * d, (j + 1) * d)
        logits = jnp.dot(xn, wm_ref[:, cols], preferred_element_type=F32) + bm_ref[:, cols]
        term = jax.nn.sigmoid(logits) * jnp.dot(y_ref[...], p_ref[...], preferred_element_type=F32)
        mixed = term if mixed is None else mixed + term
    x = x + jnp.dot(mixed.astype(BF16), wo_ref[...], preferred_element_type=F32)
    if final_norm:
        ms = jnp.mean(x * x, axis=-1, keepdims=True)
        x = x * lax.rsqrt(ms + NORM_EPS) * fg_ref[...]
    o_ref[...] = x


def _merge(x2d, ya, yb, yc, g, wm, bm, pa, pb, pc, wo, fg, final_norm):
    m = x2d.shape[0]
    tm = ROW_TILE

    def rows(arr):
        return pl.BlockSpec((tm, arr.shape[-1]), lambda i: (i, 0))

    def full(arr):
        return pl.BlockSpec(arr.shape, lambda i: (0, 0))

    consts = [g, wm, bm, pa, pb, pc, wo, fg]
    return pl.pallas_call(
        functools.partial(_merge_kernel, final_norm=final_norm),
        grid=(m // tm,),
        in_specs=[rows(a) for a in (x2d, ya, yb, yc)] + [full(a) for a in consts],
        out_specs=rows(x2d),
        out_shape=jax.ShapeDtypeStruct((m, D_MODEL), F32),
        compiler_params=_params(("parallel",)),
    )(x2d, ya, yb, yc, *consts)


def kernel(x, norm_g, w_in, b_merge, cmp_pe_k, cmp_w1_k, cmp_w2_k, cmp_pe_v, cmp_w1_v, cmp_w2_v, swa_sinks,
           rwkv_mu, rwkv_w0, rwkv_w2, rwkv_a0, rwkv_a2, rwkv_k_k, rwkv_k_a, rwkv_r_k, rwkv_ln_w, rwkv_ln_b,
           rwkv_v0, rwkv_v1, rwkv_v2, proj_a, proj_b, proj_c, w_out, final_g):
    b, t, d = x.shape
    depth = w_in.shape[0]
    m = b * t
    x2d = x.reshape(m, d)
    o_merge = _ref_offsets()["merge"]
    v_first = None
    for l in range(depth):
        w = jnp.concatenate([jnp.zeros((d, n), BF16) if start == N_IN else w_in[l][:, start:start + n].astype(BF16)
                             for start, n in _COL_RUNS], axis=1)
        g_l = norm_g[l].reshape(1, d)
        outs = iter(_in_proj(x2d, g_l, w))
        seg = {}
        for name, _, _, parts in _SEGS:
            arrs = [next(outs) for _ in range(parts)]
            arrs = [a.reshape(b, t, a.shape[-1]) for a in arrs]
            seg[name] = arrs[0] if parts == 1 else arrs

        pe = jnp.stack([cmp_pe_k[l], cmp_pe_v[l]]).reshape(2, 1, CMP_BLOCK * HEAD_DIM)
        kcvc = _compress(seg["a_kv_cmp"], pe, jnp.stack([cmp_w1_k[l], cmp_w1_v[l]]),
                         jnp.stack([cmp_w2_k[l], cmp_w2_v[l]]))
        y_a, y_b = _attention(seg["a_q"], seg["a_z"], seg["a_gate"], seg["a_kv_slc"], seg["a_kv_win"], kcvc,
                              seg["b_q"], seg["b_z"], seg["b_kv"], swa_sinks[l])

        row = lambda a: a.reshape(1, -1)
        p = {"mu": row(rwkv_mu[l]), "w0": row(rwkv_w0[l]), "w2": rwkv_w2[l], "a0": row(rwkv_a0[l]),
             "a2": rwkv_a2[l], "k_k": row(rwkv_k_k[l]), "k_a": row(rwkv_k_a[l]), "r_k": row(rwkv_r_k[l]),
             "ln_w": row(rwkv_ln_w[l]), "ln_b": row(rwkv_ln_b[l])}
        if l > 0:
            p.update(v0=row(rwkv_v0[l - 1]), v1=rwkv_v1[l - 1], v2=rwkv_v2[l - 1])
        y_c, v_first = _rwkv(seg["c_shift"], seg["c_z"], v_first, p)

        w_merge = w_in[l][:, o_merge:o_merge + 3 * d].astype(BF16)
        x2d = _merge(x2d, y_a.reshape(m, -1), y_b.reshape(m, -1), y_c.reshape(m, -1), g_l, w_merge,
                     b_merge[l].reshape(1, -1), proj_a[l].astype(BF16), proj_b[l].astype(BF16),
                     proj_c[l].astype(BF16), w_out[l].astype(BF16), final_g.reshape(1, d),
                     final_norm=(l == depth - 1))
    return x2d.reshape(b, t, d)
```

```python
import functools

import numpy as np
import jax
import jax.numpy as jnp
from jax import lax
from jax.experimental import pallas as pl
from jax.experimental.pallas import tpu as pltpu

F32 = jnp.float32
BF16 = jnp.bfloat16

D_MODEL = 1024
HEAD_DIM = 64
NSA_HEADS = 8
NSA_GROUPS = 2
NSA_HG = NSA_HEADS // NSA_GROUPS
CMP_STRIDE = 16
CMP_BLOCK = 32
CMP_HIDDEN = 128
SLC_BLOCK = 64
SLC_TOPN = 16
NSA_WINDOW = 512
SWA_HEADS = 4
SWA_GROUPS = 2
SWA_HG = SWA_HEADS // SWA_GROUPS
SWA_WINDOW = 128
RWKV_HEADS = 4
RWKV_WIDTH = RWKV_HEADS * HEAD_DIM
DECAY_LORA = 64
ICLR_LORA = 64
VRES_LORA = 32
SHIFT_WIDTH = 3 * RWKV_WIDTH + DECAY_LORA + ICLR_LORA
NORM_EPS = 1e-6
GN_EPS = 64e-5
NEG_INF = -1e30
FORCE = 1e9

_REF_SEGMENTS = (
    ("a_q", 512), ("a_kv_cmp", 256), ("a_kv_slc", 256), ("a_kv_win", 256), ("a_gate", 24),
    ("a_z", 512), ("b_q", 256), ("b_kv", 256), ("b_z", 256), ("c_shift", SHIFT_WIDTH),
    ("c_z", 256), ("merge", 3 * D_MODEL),
)
N_IN = sum(w for _, w in _REF_SEGMENTS)

VMEM_LIMIT = 56 * 1024 * 1024
ROW_TILE = 512
ATT_TILE = 256
ATT_ROWS = 128
RWKV_CHUNK = 64
RWKV_STEP_CHUNKS = 8
RWKV_GROUP_CHUNKS = 2
RWKV_STAGGER = 4
RWKV_SUB = 16


def _ref_offsets():
    out, off = {}, 0
    for name, width in _REF_SEGMENTS:
        out[name] = off
        off += width
    return out


def _kv_interleave(base):
    idx = []
    for g in range(2):
        idx += list(range(base + 128 + g * 64, base + 128 + (g + 1) * 64))
        idx += list(range(base + g * 64, base + (g + 1) * 64))
    return idx


def _projection_layout():
    o = _ref_offsets()
    segs, idx = [], []

    def add(name, cols, dtype, parts=1):
        segs.append((name, len(cols), dtype, parts))
        idx.extend(cols)

    add("a_q", list(range(o["a_q"], o["a_q"] + 512)), BF16)
    add("a_z", list(range(o["a_z"], o["a_z"] + 512)), F32)
    add("a_kv_cmp", list(range(o["a_kv_cmp"], o["a_kv_cmp"] + 256)), BF16, parts=4)
    add("a_kv_slc", _kv_interleave(o["a_kv_slc"]), BF16)
    add("a_kv_win", _kv_interleave(o["a_kv_win"]), BF16)
    gate = []
    for g in range(NSA_GROUPS):
        cols = [N_IN] * 128
        for br in range(3):
            for h in range(NSA_HG):
                cols[br * NSA_HG + h] = o["a_gate"] + br * NSA_HEADS + g * NSA_HG + h
        gate += cols
    add("a_gate", gate, F32)
    add("b_q", list(range(o["b_q"], o["b_q"] + 256)), BF16)
    add("b_kv", _kv_interleave(o["b_kv"]), BF16)
    add("b_z", list(range(o["b_z"], o["b_z"] + 256)), F32)
    add("c_shift", list(range(o["c_shift"], o["c_shift"] + SHIFT_WIDTH)), F32)
    add("c_z", list(range(o["c_z"], o["c_z"] + 256)), F32)
    return tuple(segs), np.asarray(idx, np.int32)


_SEGS, _COL_IDX = _projection_layout()
_N_PROJ = int(_COL_IDX.shape[0])


def _column_runs(idx):
    runs, i = [], 0
    while i < len(idx):
        j = i + 1
        while j < len(idx) and (idx[j] == idx[j - 1] + 1 if idx[i] != N_IN else idx[j] == N_IN):
            j += 1
        runs.append((int(idx[i]), j - i))
        i = j
    return runs


_COL_RUNS = _column_runs(_COL_IDX)


def _params(sem):
    return pltpu.CompilerParams(dimension_semantics=sem, vmem_limit_bytes=VMEM_LIMIT)


def _in_proj_kernel(x_ref, g_ref, w_ref, *refs):
    out_refs, chunk_sc = refs[:-1], refs[-1]
    x = x_ref[...]
    ms = jnp.mean(x * x, axis=-1, keepdims=True)
    xn = (x * lax.rsqrt(ms + NORM_EPS) * g_ref[...]).astype(BF16)
    off = 0
    refs = iter(out_refs)
    for _, width, _, parts in _SEGS:
        outs = [next(refs) for _ in range(parts)]
        for c0 in range(0, width, 512):
            cw = min(512, width - c0)
            val = jnp.dot(xn, w_ref[:, off + c0:off + c0 + cw], preferred_element_type=F32)
            if parts == 1:
                outs[0][:, c0:c0 + cw] = val.astype(outs[0].dtype)
            else:
                pw = width // parts
                per_slab = 128 // pw
                for h in range(width // 128):
                    chunk_sc[h] = val[:, h * 128:(h + 1) * 128]
                n_chunks = val.shape[0] // CMP_STRIDE
                for l in range(CMP_STRIDE):
                    for h in range(width // 128):
                        rows = chunk_sc[h, pl.ds(l, n_chunks, stride=CMP_STRIDE), :]
                        for jj in range(per_slab):
                            o_ref = outs[h * per_slab + jj]
                            o_ref[:, l * pw:(l + 1) * pw] = rows[:, jj * pw:(jj + 1) * pw].astype(o_ref.dtype)
        off += width


def _in_proj(x2d, g, w):
    m = x2d.shape[0]
    tm = ROW_TILE
    out_shape, out_specs, chunk_width = [], [], 0
    for _, width, dt, parts in _SEGS:
        if parts == 1:
            out_shape.append(jax.ShapeDtypeStruct((m, width), dt))
            out_specs.append(pl.BlockSpec((tm, width), lambda i: (i, 0)))
        else:
            assert width <= 512 and chunk_width in (0, width)
            chunk_width = width
            cols = width // parts * CMP_STRIDE
            out_shape += [jax.ShapeDtypeStruct((m // CMP_STRIDE, cols), dt)] * parts
            out_specs += [pl.BlockSpec((tm // CMP_STRIDE, cols), lambda i: (i, 0))] * parts
    return pl.pallas_call(
        _in_proj_kernel,
        grid=(m // tm,),
        in_specs=[
            pl.BlockSpec((tm, D_MODEL), lambda i: (i, 0)),
            pl.BlockSpec((1, D_MODEL), lambda i: (0, 0)),
            pl.BlockSpec((D_MODEL, _N_PROJ), lambda i: (0, 0), pipeline_mode=pl.Buffered(1)),
        ],
        out_specs=out_specs,
        out_shape=out_shape,
        scratch_shapes=[pltpu.VMEM((chunk_width // 128, tm, 128), F32)],
        compiler_params=_params(("parallel",)),
    )(x2d, g, w)


def _compress_kernel(*refs):
    z_refs, (pe_ref, w1_ref, w2_ref, o_ref) = refs[:-4], refs[-4:]
    half = CMP_STRIDE * HEAD_DIM
    for j, z_ref in enumerate(z_refs):
        kv = j // NSA_GROUPS
        z = z_ref[...]
        w1 = w1_ref[kv]
        first = _dot_exact_lhs(z, w1[:half], 3)
        second = _dot_exact_lhs(z, w1[half:], 3)
        n = z.shape[0]
        hid = first + pltpu.roll(second, n - 1, 0)
        hid = hid + _dot3(_split(pe_ref[kv]), _split(w1))
        act = hid * jax.nn.sigmoid(hid)
        o_ref[j] = _dot3(_split(act), _split(w2_ref[kv]))


def _compress(kv_cmp, pe, w1, w2):
    zs = list(kv_cmp)
    b, nch, _ = zs[0].shape

    def full(arr):
        return pl.BlockSpec(arr.shape, lambda i: (0,) * arr.ndim)

    return pl.pallas_call(
        _compress_kernel,
        grid=(b,),
        in_specs=[pl.BlockSpec((None, nch, CMP_STRIDE * HEAD_DIM), lambda i: (i, 0, 0)) for _ in zs]
        + [full(pe), full(w1), full(w2)],
        out_specs=pl.BlockSpec((None, len(zs), nch, HEAD_DIM), lambda i: (i, 0, 0, 0)),
        out_shape=jax.ShapeDtypeStruct((b, len(zs), nch, HEAD_DIM), F32),
        compiler_params=_params(("parallel",)),
    )(*zs, pe, w1, w2)


def _widen_queries(q, n_heads):
    zeros = jnp.zeros((q.shape[0], HEAD_DIM), q.dtype)
    return [jnp.concatenate([zeros, q[:, h * HEAD_DIM:(h + 1) * HEAD_DIM]], axis=1) for h in range(n_heads)]


def _pair_values(kv):
    swapped = jnp.concatenate([kv[:, HEAD_DIM:], kv[:, :HEAD_DIM]], axis=1)
    lane = lax.broadcasted_iota(jnp.int32, kv.shape, 1)
    low = jnp.where(lane < HEAD_DIM, 1.0, 0.0).astype(kv.dtype)
    high = jnp.where(lane < HEAD_DIM, 0.0, 1.0).astype(kv.dtype)
    even = jnp.concatenate([kv * low, low], axis=1)
    odd = jnp.concatenate([swapped * high, high], axis=1)
    return even, odd


def _pair_lanes(even, odd):
    low = lax.broadcasted_iota(jnp.int32, even.shape, 1) < HEAD_DIM
    return jnp.where(low, even, odd)


def _nt(a, b):
    return lax.dot_general(a, b, (((1,), (1,)), ((), ())), preferred_element_type=F32)


def _rounds(streams, stagger=0):
    results = [None] * len(streams)
    waiting = list(enumerate(streams))
    live = {}
    rounds = 0
    while waiting or live:
        while waiting and waiting[0][0] * stagger <= rounds:
            idx, stream = waiting.pop(0)
            live[idx] = stream
        for idx in list(live):
            try:
                next(live[idx])
            except StopIteration as stop:
                results[idx] = stop.value
                del live[idx]
        rounds += 1
        yield
    return results


def _interleave(*streams, stagger=0):
    rounds = _rounds(streams, stagger)
    while True:
        try:
            next(rounds)
        except StopIteration as stop:
            return stop.value


def _band_block(qa_even, qa_odd, keys, val_even, val_odd, bias, sinks=None):
    s = [_nt(qa, keys) + bias for qa in (qa_even, qa_odd)]
    yield
    m = [jnp.max(x, axis=-1, keepdims=True) for x in s]
    if sinks is not None:
        m = [jnp.maximum(x, sinks[e]) for e, x in enumerate(m)]
    p = [jnp.exp(x - mx).astype(BF16) for x, mx in zip(s, m)]
    yield
    acc = (jnp.dot(p[0], val_even, preferred_element_type=F32)
           + jnp.dot(p[1], val_odd, preferred_element_type=F32))
    denom = acc[:, 2 * HEAD_DIM:]
    if sinks is not None:
        shape = (acc.shape[0], 2 * HEAD_DIM)
        denom = denom + _pair_lanes(jnp.broadcast_to(jnp.exp(sinks[0] - m[0]), shape),
                                    jnp.broadcast_to(jnp.exp(sinks[1] - m[1]), shape))
    return acc[:, :2 * HEAD_DIM] / denom


def _band_blocks(blocks, sinks=None):
    return (yield from _rounds([_band_block(*blk, sinks) for blk in blocks], stagger=1))


def _band_bias_table(deltas, rows, width, window):
    r = np.arange(rows)[:, None]
    c = np.arange(width)[None, :]
    out = [np.where((c <= r + d) & (c > r + d - window), 0.0, NEG_INF) for d in deltas]
    return np.stack(out).astype(np.float32)


def _attention_kernel(sink_ref, q_ref, z_ref, gate_ref, kvs_ref, kvw_ref, kcvc_ref, bq_ref, bz_ref,
                      bkv_ref, negexp_ref, ovl_ref, wbias_ref, cbias_ref, gexp_ref, bbias_ref,
                      o_ref, ob_ref, srhs_sc, spe_sc, spo_sc, wpe_sc, wpo_sc, bpe_sc, bpo_sc, m_sc, acc_sc):
    i = pl.program_id(1)
    tq = ATT_TILE
    rb = ATT_ROWS
    n_rb = tq // rb
    n_cmp_pad = kcvc_ref.shape[1]
    n_slc = ovl_ref.shape[0]
    n_pairs = NSA_HG // 2
    groups = range(NSA_GROUPS)
    wd = NSA_HG * HEAD_DIM
    wb = SWA_HG * HEAD_DIM
    kv_lanes = [slice(g * 2 * HEAD_DIM, (g + 1) * 2 * HEAD_DIM) for g in groups]

    @pl.when(i == 0)
    def _():
        for g in groups:
            kvs = kvs_ref[:, kv_lanes[g]]
            srhs_sc[g] = jnp.concatenate([kvs, negexp_ref[...]], axis=1)
            spe_sc[g], spo_sc[g] = _pair_values(kvs)
            wpe_sc[g], wpo_sc[g] = _pair_values(kvw_ref[:, kv_lanes[g]])
            bpe_sc[g], bpo_sc[g] = _pair_values(bkv_ref[:, kv_lanes[g]])

    scale = jnp.asarray(HEAD_DIM ** -0.5, BF16)
    qas = [_widen_queries(q_ref[:, g * wd:(g + 1) * wd] * scale, NSA_HG) for g in groups]

    def band_inputs(g, kv_ref, pe_sc, po_sc, bias_ref, window, r0):
        q0 = i * tq + r0
        k0 = pl.multiple_of(jnp.maximum(q0 - window, 0), rb)
        width = window + rb
        variant = jnp.minimum(q0 // rb, bias_ref.shape[0] - 1)
        return (kv_ref[pl.ds(k0, width), kv_lanes[g]], pe_sc[g, pl.ds(k0, width), :],
                po_sc[g, pl.ds(k0, width), :], bias_ref[variant])

    def window_stream(g):
        bands = [band_inputs(g, kvw_ref, wpe_sc, wpo_sc, wbias_ref, NSA_WINDOW, r0) for r0 in range(0, tq, rb)]
        blocks = [(qas[g][2 * pair][r0:r0 + rb], qas[g][2 * pair + 1][r0:r0 + rb]) + bands[r0 // rb]
                  for pair in range(n_pairs) for r0 in range(0, tq, rb)]
        outs = yield from _band_blocks(blocks)
        return jnp.concatenate([jnp.concatenate(outs[pair * n_rb:(pair + 1) * n_rb], axis=0)
                                for pair in range(n_pairs)], axis=1)

    def sink_window_stream(g):
        lanes = slice(g * wb, (g + 1) * wb)
        qbs = _widen_queries(bq_ref[:, lanes] * scale, SWA_HG)
        sinks = (sink_ref[g * SWA_HG], sink_ref[g * SWA_HG + 1])
        blocks = [(qbs[0][r0:r0 + rb], qbs[1][r0:r0 + rb])
                  + band_inputs(g, bkv_ref, bpe_sc, bpo_sc, bbias_ref, SWA_WINDOW, r0) for r0 in range(0, tq, rb)]
        outs = yield from _band_blocks(blocks, sinks)
        y = jnp.concatenate(outs, axis=0)
        zb = bz_ref[:, lanes]
        ob_ref[:, lanes] = (y * (zb * jax.nn.sigmoid(zb))).astype(ob_ref.dtype)

    def compressed_stream(g):
        zeros_k = jnp.zeros((n_cmp_pad, HEAD_DIM), BF16)
        kc_both = jnp.concatenate([jnp.concatenate([zeros_k, t], axis=1) for t in _split(kcvc_ref[g])], axis=0)
        vc = kcvc_ref[NSA_GROUPS + g].astype(BF16)
        vc_even = jnp.concatenate([vc, zeros_k], axis=1)
        vc_odd = jnp.concatenate([zeros_k, vc], axis=1)
        tpos = i * tq + lax.broadcasted_iota(jnp.int32, (tq, n_cmp_pad), 0)
        cmp_end = lax.broadcasted_iota(jnp.int32, (tq, n_cmp_pad), 1) * CMP_STRIDE + (CMP_BLOCK - 1)
        valid_c = cmp_end <= tpos
        s = [_nt(qa, kc_both) for qa in qas[g]]
        s = [x[:, :n_cmp_pad] + x[:, n_cmp_pad:] for x in s]
        yield
        s = [jnp.where(valid_c, x, NEG_INF) for x in s]
        m = [jnp.max(x, axis=-1, keepdims=True) for x in s]
        yield
        p = [jnp.where(valid_c, jnp.exp(x - mx), 0.0) for x, mx in zip(s, m)]
        denom = [jnp.sum(x, axis=-1, keepdims=True) for x in p]
        yield
        p = [x * (1.0 / jnp.where(d > 0, d, 1.0)) for x, d in zip(p, denom)]
        p_sum = (p[0] + p[1]) + (p[2] + p[3])
        pb = [x.astype(BF16) for x in p]
        o_cmp = jnp.concatenate(
            [jnp.dot(pb[2 * pair], vc_even, preferred_element_type=F32)
             + jnp.dot(pb[2 * pair + 1], vc_odd, preferred_element_type=F32) for pair in range(n_pairs)], axis=1)
        yield
        imp = None
        rest = p_sum
        for _ in range(3):
            term = rest.astype(BF16)
            rest = rest - term.astype(F32)
            part = _nt(ovl_ref[...], term)
            imp = part if imp is None else imp + part
        yield
        blk = lax.broadcasted_iota(jnp.int32, (n_slc, tq), 0)
        cur = (i * tq + lax.broadcasted_iota(jnp.int32, (n_slc, tq), 1)) // SLC_BLOCK
        forced = (blk == 0) | (blk == cur) | (blk == cur - 1)
        score = jnp.where(forced, FORCE, jnp.where(blk <= cur, imp, -FORCE))
        ranks = [jnp.zeros((n_slc, tq), F32) for _ in range(4)]
        for r in range(n_slc):
            row = score[r:r + 1, :]
            ahead = (row > score) | ((row == score) & (r < blk))
            ranks[r % 4] = ranks[r % 4] + jnp.where(ahead, 1.0, 0.0)
            if r % 4 == 3:
                yield
        rank = (ranks[0] + ranks[1]) + (ranks[2] + ranks[3])
        n_sel = min(SLC_TOPN, n_slc)
        unsel_t = jnp.where((rank < n_sel) & (blk <= cur), 0.0, 1.0)
        unsel_t = jnp.concatenate([unsel_t, jnp.zeros((128 - n_slc, tq), F32)], axis=0)
        return o_cmp, jnp.transpose(unsel_t).astype(BF16)

    results = _interleave(*[stream(g) for g in groups
                            for stream in (window_stream, sink_window_stream, compressed_stream)])
    o_win = [results[3 * g] for g in groups]
    o_cmp = [results[3 * g + 2][0] for g in groups]
    unsel = [results[3 * g + 2][1] for g in groups]

    lhs = {(g, h): jnp.concatenate([qas[g][h], unsel[g]], axis=1) for g in groups for h in range(NSA_HG)}
    m_sc[...] = jnp.full(m_sc.shape, NEG_INF, F32)
    acc_sc[...] = jnp.zeros(acc_sc.shape, F32)
    subs = [(g, h, r0) for g in groups for h in range(NSA_HG) for r0 in range(0, tq, rb)]

    def slc_keys(start, width, bias_lanes):
        rhs = [srhs_sc[g, pl.ds(start, width), :] for g in groups]
        val_even = [spe_sc[g, pl.ds(start, width), :] for g in groups]
        val_odd = [spo_sc[g, pl.ds(start, width), :] for g in groups]
        s = {}
        for g, h, r0 in subs:
            s[g, h, r0] = _nt(lhs[g, h][r0:r0 + rb], rhs[g])
            if bias_lanes is not None:
                s[g, h, r0] = s[g, h, r0] + cbias_ref[r0:r0 + rb, bias_lanes:bias_lanes + width]
        m_prev = {(g, h, r0): m_sc[g * NSA_HG + h, r0:r0 + rb] for g, h, r0 in subs}
        m_next = {key: jnp.maximum(m_prev[key], jnp.max(s[key], axis=-1, keepdims=True)) for key in subs}
        p = {key: jnp.exp(s[key] - jnp.concatenate([m_next[key]] * (width // 128), axis=1)).astype(BF16)
             for key in subs}
        alpha = {key: jnp.exp(m_prev[key] - m_next[key]) for key in subs}
        for g, h, r0 in subs:
            m_sc[g * NSA_HG + h, r0:r0 + rb] = m_next[g, h, r0]
        for g in groups:
            for pair in range(n_pairs):
                for r0 in range(0, tq, rb):
                    rows = slice(r0, r0 + rb)
                    decay = _pair_lanes(alpha[g, 2 * pair, r0], alpha[g, 2 * pair + 1, r0])
                    decay = jnp.concatenate([decay, decay], axis=1)
                    acc_sc[g * n_pairs + pair, rows] = (
                        decay * acc_sc[g * n_pairs + pair, rows]
                        + jnp.dot(p[g, 2 * pair, r0], val_even[g], preferred_element_type=F32)
                        + jnp.dot(p[g, 2 * pair + 1, r0], val_odd[g], preferred_element_type=F32))

    def wide(j, carry):
        slc_keys(pl.multiple_of(j * (2 * tq), 2 * tq), 2 * tq, None)
        return carry

    lax.fori_loop(0, i // 2, wide, 0)

    @pl.when(i % 2 == 0)
    def _():
        slc_keys(pl.multiple_of(i * tq, tq), tq, tq)

    @pl.when(i % 2 == 1)
    def _():
        slc_keys(pl.multiple_of((i - 1) * tq, 2 * tq), 2 * tq, 0)

    for g in groups:
        o_slc = []
        for pair in range(n_pairs):
            acc = acc_sc[g * n_pairs + pair]
            o_slc.append(acc[:, :2 * HEAD_DIM] / acc[:, 2 * HEAD_DIM:])
        o_slc = jnp.concatenate(o_slc, axis=1)
        sg = _split(jax.nn.sigmoid(gate_ref[:, g * 128:(g + 1) * 128]))
        gates = (jnp.dot(sg[0], gexp_ref[...], preferred_element_type=F32)
                 + jnp.dot(sg[1], gexp_ref[...], preferred_element_type=F32))
        y = gates[:, :wd] * o_cmp[g] + gates[:, wd:2 * wd] * o_slc + gates[:, 2 * wd:] * o_win[g]
        z = z_ref[:, g * wd:(g + 1) * wd]
        o_ref[:, g * wd:(g + 1) * wd] = (y * (z * jax.nn.sigmoid(z))).astype(o_ref.dtype)


def _attention(q, z, gate, kv_slc, kv_win, kcvc, bq, bz, bkv, sinks):
    b, t, _ = q.shape
    tq = ATT_TILE
    assert t % (2 * tq) == 0 and t >= NSA_WINDOW + tq and NSA_WINDOW % tq == 0 and SWA_HG == 2
    assert NSA_GROUPS == SWA_GROUPS
    n_slc = t // SLC_BLOCK
    nch = t // CMP_STRIDE
    wd = NSA_HG * HEAD_DIM
    negexp = np.where(np.arange(t)[:, None] // SLC_BLOCK == np.arange(128)[None, :], -2.0 ** 100, 0.0)
    ci = np.arange(nch)[None, :] * CMP_STRIDE
    sj = np.arange(n_slc)[:, None] * SLC_BLOCK
    overlap_t = ((ci < sj + SLC_BLOCK) & (ci + CMP_BLOCK > sj)).astype(np.float32)
    rb = ATT_ROWS
    assert NSA_WINDOW % rb == 0 and SWA_WINDOW % rb == 0 and tq % rb == 0
    wbias = _band_bias_table(range(0, NSA_WINDOW + 1, rb), rb, NSA_WINDOW + rb, NSA_WINDOW)
    cbias = np.concatenate([np.zeros((tq, tq), np.float32), _band_bias_table([0], tq, tq, t)[0]], axis=1)
    bbias = _band_bias_table(range(0, SWA_WINDOW + 1, rb), rb, SWA_WINDOW + rb, SWA_WINDOW)
    gexp = np.zeros((128, 3 * wd), np.float32)
    for br in range(3):
        for h in range(NSA_HG):
            gexp[br * NSA_HG + h, br * wd + h * HEAD_DIM:br * wd + (h + 1) * HEAD_DIM] = 1.0

    def const(arr):
        return pl.BlockSpec(arr.shape, lambda bi, i: (0,) * arr.ndim)

    def tile(arr):
        return pl.BlockSpec((None, tq, arr.shape[-1]), lambda bi, i: (bi, i, 0))

    def whole(arr):
        return pl.BlockSpec((None,) + arr.shape[1:], lambda bi, i: (bi,) + (0,) * (arr.ndim - 1))

    consts = [jnp.asarray(negexp, BF16), jnp.asarray(overlap_t, BF16), jnp.asarray(wbias),
              jnp.asarray(cbias), jnp.asarray(gexp, BF16), jnp.asarray(bbias)]
    key_scratch = pltpu.VMEM((NSA_GROUPS, t, 256), BF16)
    return pl.pallas_call(
        _attention_kernel,
        grid=(b, t // tq),
        in_specs=[pl.BlockSpec(memory_space=pltpu.SMEM), tile(q), tile(z), tile(gate), whole(kv_slc),
                  whole(kv_win), whole(kcvc), tile(bq), tile(bz), whole(bkv)] + [const(a) for a in consts],
        out_specs=(pl.BlockSpec((None, tq, 512), lambda bi, i: (bi, i, 0)),
                   pl.BlockSpec((None, tq, 256), lambda bi, i: (bi, i, 0))),
        out_shape=(jax.ShapeDtypeStruct((b, t, 512), BF16), jax.ShapeDtypeStruct((b, t, 256), BF16)),
        scratch_shapes=[key_scratch] * 7 + [
            pltpu.VMEM((NSA_GROUPS * NSA_HG, tq, 128), F32),
            pltpu.VMEM((NSA_GROUPS * NSA_HG // 2, tq, 256), F32),
        ],
        compiler_params=_params(("parallel", "arbitrary")),
    )(sinks, q, z, gate, kv_slc, kv_win, kcvc, bq, bz, bkv, *consts)


_NN = (((1,), (0,)), ((), ()))
_NT = (((1,), (1,)), ((), ()))


def _split(a):
    hi = a.astype(BF16)
    lo = (a - hi.astype(F32)).astype(BF16)
    return hi, lo


def _dot(a, b, dims=_NN):
    return lax.dot_general(a, b, dims, preferred_element_type=F32)


def _dot3(a, b, dims=_NN):
    rows = a[0].shape[0]
    both = _dot(jnp.concatenate([a[0], a[1]], axis=0), b[0], dims)
    return (both[:rows] + both[rows:]) + _dot(a[0], b[1], dims)


def _dot_exact_lhs(a, b, n_terms):
    out = None
    for _ in range(n_terms):
        term = b.astype(BF16)
        b = b - term.astype(F32)
        part = _dot(a, term)
        out = part if out is None else out + part
    return out


def _dot_exact_rhs(x, ones):
    hi, lo = _split(x)
    return _dot(hi, ones) + _dot(lo, ones)


def _block_diag(pair, head_masks):
    return tuple(jnp.concatenate([x * m for m in head_masks], axis=0) for x in pair)


def _each(fn, *lists):
    return [fn(*items) for items in zip(*lists)]


def _unit_lower_inverse(lows, eye, same_sub, bd):
    diag = _each(lambda low: jnp.where(same_sub, low, 0.0), lows)
    off = _each(lambda low, d: low - d, lows, diag)
    inv = _each(lambda d: eye + d, diag)
    power = diag
    span = 2
    while span < RWKV_SUB:
        ps = _each(_split, power)
        power = _each(lambda p: _dot3(p, bd(p)), ps)
        yield
        inv = _each(lambda iv, pw: _dot3(_split(iv), bd(_split(eye + pw))), inv, power)
        yield
        span *= 2
    inv_s = _each(_split, inv)
    cross = _each(lambda iv, o: _dot3(iv, bd(_split(o))), inv_s, off)
    yield
    total = _each(lambda cr: eye + cr, cross)
    power = cross
    span = 2
    while span < RWKV_CHUNK // RWKV_SUB:
        ps = _each(_split, power)
        power = _each(lambda p: _dot3(p, bd(p)), ps)
        yield
        total = _each(lambda tt, pw: _dot3(_split(tt), bd(_split(eye + pw))), total, power)
        yield
        span *= 2
    return _each(lambda tt, iv: _dot3(_split(tt), bd(iv)), total, inv_s)


def _rwkv_chunk_maps(a_t, r_t, b_t, k_t, b_e, k_e, v, p_end, cst):
    C = RWKV_CHUNK
    eye, strict, incl, same_sub, head_masks = cst
    bd = lambda pair: _block_diag(pair, head_masks)

    ar = _each(lambda a, r: _split(jnp.concatenate([a, r], axis=0)), a_t, r_t)
    g_b = _each(lambda x, b: _dot3(x, bd(_split(b)), _NT), ar, b_t)
    g_k = _each(lambda x, k: _dot3(x, bd(_split(k)), _NT), ar, k_t)
    yield
    a_ab = _each(lambda g: jnp.where(strict, g[:C], 0.0), g_b)
    a_rb = _each(lambda g: _split(jnp.where(incl, g[C:], 0.0)), g_b)
    a_ak = _each(lambda g: _split(jnp.where(strict, g[:C], 0.0)), g_k)
    a_rk = _each(lambda g: _split(jnp.where(incl, g[C:], 0.0)), g_k)
    v_bd = _each(lambda x: bd(_split(x)), v)
    akv = _each(_dot3, a_ak, v_bd)
    yield
    t_inv = yield from _unit_lower_inverse(a_ab, eye, same_sub, bd)
    t_inv = _each(_split, t_inv)
    yield
    w_m = _each(lambda t, a: _dot3(t, bd(_split(a))), t_inv, a_t)
    u0 = _each(lambda t, x: _dot3(t, bd(_split(x))), t_inv, akv)
    yield
    r_m = _each(lambda r, g, w: r + _dot3(g, bd(_split(w))), r_t, a_rb, w_m)
    y0 = _each(lambda g, u, gk, vb: _dot3(g, bd(_split(u))) + _dot3(gk, vb), a_rb, u0, a_rk, v_bd)
    yield
    bk = _each(lambda b, k: _split(jnp.transpose(jnp.concatenate([b, k], axis=0))), b_e, k_e)
    m_add = _each(lambda x, w: _dot3((x[0][:, :C], x[1][:, :C]), _split(w)), bk, w_m)
    n_all = _each(lambda x, u, vv: _dot3(x, _split(jnp.concatenate([u, vv], axis=0))), bk, u0, v)
    yield
    keep = [m.astype(F32) for m in head_masks]

    def own_blocks(full):
        parts = [full[h * HEAD_DIM:(h + 1) * HEAD_DIM, :] * keep[h] for h in range(RWKV_HEADS)]
        return (parts[0] + parts[1]) + (parts[2] + parts[3])

    m_l = _each(lambda m, p: own_blocks(m) + eye * p, m_add, p_end)
    n_l = _each(own_blocks, n_all)
    lhs = _each(lambda r, m: _split(jnp.concatenate([r, m], axis=0)), r_m, m_l)
    return list(zip(lhs, y0, n_l))


def _rwkv_kernel(*refs, has_vres):
    if has_vres:
        (feat_ref, prev_ref, z_ref, vfirst_ref, mu_ref, w0_ref, w2_ref, a0_ref, a2_ref, kk_ref, ka_ref,
         rk_ref, lnw_ref, lnb_ref, hsum_ref, v0_ref, v1_ref, v2_ref, y_ref, state) = refs
    else:
        (feat_ref, prev_ref, z_ref, mu_ref, w0_ref, w2_ref, a0_ref, a2_ref, kk_ref, ka_ref,
         rk_ref, lnw_ref, lnb_ref, hsum_ref, y_ref, vout_ref, state) = refs
    c = pl.program_id(1)
    C = RWKV_CHUNK
    W = RWKV_WIDTH
    n_tok = feat_ref.shape[0]
    group = RWKV_GROUP_CHUNKS * C

    @pl.when(c == 0)
    def _():
        state[...] = jnp.zeros(state.shape, F32)

    def lora(x, w_ref):
        return _dot3(_split(x), _split(w_ref[...]))

    hsum = hsum_ref[...]

    def head_sum(x):
        return _dot_exact_rhs(x, hsum)

    feat = feat_ref[...]
    row = lax.broadcasted_iota(jnp.int32, feat.shape, 0)
    last_prev = jnp.where(c == 0, 0.0, prev_ref[7:8, :])
    prev = jnp.where(row == 0, last_prev, pltpu.roll(feat, 1, 0))
    xs = feat + (prev - feat) * mu_ref[...]

    ti = lax.broadcasted_iota(jnp.int32, (group, group), 0)
    tj = lax.broadcasted_iota(jnp.int32, (group, group), 1)
    tri = jnp.where((ti >= tj) & (ti // C == tj // C), 1.0, 0.0).astype(BF16)
    t_idx = lax.broadcasted_iota(jnp.int32, (C, W), 0)
    i_idx = lax.broadcasted_iota(jnp.int32, (C, W), 1) % HEAD_DIM
    lane_head = lax.broadcasted_iota(jnp.int32, (C, W), 1) // HEAD_DIM
    cst = (
        jnp.where(t_idx == i_idx, 1.0, 0.0),
        t_idx > i_idx,
        t_idx >= i_idx,
        (t_idx // RWKV_SUB) == (i_idx // RWKV_SUB),
        [jnp.where(lane_head == h, 1.0, 0.0).astype(BF16) for h in range(RWKV_HEADS)],
    )
    carry = {"state": state[...], "turn": 0}

    def token_group(idx):
        rows = slice(idx * group, (idx + 1) * group)
        x = xs[rows]
        r = x[:, :W]
        k = x[:, W:2 * W]
        v = x[:, 2 * W:3 * W]
        wd = x[:, 3 * W:3 * W + DECAY_LORA]
        ad = x[:, 3 * W + DECAY_LORA:]
        pre = -(w0_ref[...] + lora(jnp.tanh(wd), w2_ref))
        softplus = jnp.maximum(pre, 0.0) + jnp.log(1.0 + jnp.exp(-jnp.abs(pre)))
        logw = -jnp.exp(-softplus - 0.5)
        yield
        if has_vres:
            mix = jax.nn.sigmoid(v0_ref[...] + lora(lora(v, v1_ref), v2_ref))
            v = v + (vfirst_ref[rows, :] - v) * mix
        else:
            vout_ref[rows, :] = v
        alpha = jax.nn.sigmoid(a0_ref[...] + lora(ad, a2_ref))
        kk = k * kk_ref[...]
        kk = kk / jnp.maximum(jnp.sqrt(head_sum(kk * kk)), 1e-12)
        k = k * (1.0 + (alpha - 1.0) * ka_ref[...])
        yield
        cum = _dot_exact_lhs(tri, logw, 3)
        yield
        a_t = -kk * jnp.exp(cum - logw)
        r_t = r * jnp.exp(cum)
        inv_decay = jnp.exp(-cum)
        b_raw = kk * alpha
        b_t = b_raw * inv_decay
        k_t = k * inv_decay
        chunks = [slice(j * C, (j + 1) * C) for j in range(RWKV_GROUP_CHUNKS)]
        cum_end = [cum[cr.stop - 1:cr.stop, :] for cr in chunks]
        to_end = [jnp.exp(ce - cum[cr]) for ce, cr in zip(cum_end, chunks)]
        yield
        maps = yield from _rwkv_chunk_maps(
            [a_t[cr] for cr in chunks], [r_t[cr] for cr in chunks], [b_t[cr] for cr in chunks],
            [k_t[cr] for cr in chunks], [b_raw[cr] * te for cr, te in zip(chunks, to_end)],
            [k[cr] * te for cr, te in zip(chunks, to_end)], [v[cr] for cr in chunks],
            [jnp.exp(ce) for ce in cum_end], cst)
        while carry["turn"] != idx:
            yield
        s_l = carry["state"]
        ys = []
        for lhs, y0, n_l in maps:
            out = _dot3(lhs, _block_diag(_split(s_l), cst[4]))
            ys.append(out[:C] + y0)
            s_l = out[C:] + n_l
        carry["state"] = s_l
        carry["turn"] = idx + 1
        yield
        y = jnp.concatenate(ys, axis=0)
        inv_n = 1.0 / HEAD_DIM
        mean = head_sum(y) * inv_n
        cen = y - mean
        yield
        var = head_sum(cen * cen) * inv_n
        y = cen * lax.rsqrt(var + GN_EPS) * lnw_ref[...] + lnb_ref[...]
        y = y + head_sum(r * k * rk_ref[...]) * v
        z = z_ref[rows, :]
        y_ref[rows, :] = (y * (z * jax.nn.sigmoid(z))).astype(y_ref.dtype)

    _interleave(*[token_group(idx) for idx in range(n_tok // group)], stagger=RWKV_STAGGER)
    state[...] = carry["state"]


def _rwkv(feat, z, v_first, p):
    b, t, _ = feat.shape
    n_tok = RWKV_STEP_CHUNKS * RWKV_CHUNK
    assert t % n_tok == 0 and RWKV_CHUNK == HEAD_DIM
    W = RWKV_WIDTH
    has_vres = v_first is not None
    hsum = (np.arange(W)[:, None] // HEAD_DIM == np.arange(W)[None, :] // HEAD_DIM).astype(np.float32)

    def tok(width):
        return pl.BlockSpec((None, n_tok, width), lambda bi, c: (bi, c, 0))

    def full(arr):
        return pl.BlockSpec(arr.shape, lambda bi, c: (0,) * arr.ndim)

    args = [feat, feat, z]
    specs = [tok(SHIFT_WIDTH),
             pl.BlockSpec((None, 8, SHIFT_WIDTH), lambda bi, c: (bi, jnp.maximum(c * (n_tok // 8) - 1, 0), 0)),
             tok(W)]
    if has_vres:
        args.append(v_first)
        specs.append(tok(W))
    names = ["mu", "w0", "w2", "a0", "a2", "k_k", "k_a", "r_k", "ln_w", "ln_b"]
    consts = [p[n] for n in names] + [jnp.asarray(hsum, BF16)]
    if has_vres:
        consts += [p["v0"], p["v1"], p["v2"]]
    args += consts
    specs += [full(a) for a in consts]
    y_shape = jax.ShapeDtypeStruct((b, t, W), BF16)
    if has_vres:
        out_shape, out_specs = y_shape, tok(W)
    else:
        out_shape = (y_shape, jax.ShapeDtypeStruct((b, t, W), F32))
        out_specs = (tok(W), tok(W))
    res = pl.pallas_call(
        functools.partial(_rwkv_kernel, has_vres=has_vres),
        grid=(b, t // n_tok),
        in_specs=specs,
        out_specs=out_specs,
        out_shape=out_shape,
        scratch_shapes=[pltpu.VMEM((HEAD_DIM, W), F32)],
        compiler_params=_params(("parallel", "arbitrary")),
    )(*args)
    if has_vres:
        return res, v_first
    return res[0], res[1]


def _merge_kernel(x_ref, ya_ref, yb_ref, yc_ref, g_ref, wm_ref, bm_ref, pa_ref, pb_ref, pc_ref, wo_ref, fg_ref,
                  o_ref, *, final_norm):
    d = D_MODEL
    x = x_ref[...]
    ms = jnp.mean(x * x, axis=-1, keepdims=True)
    xn = (x * lax.rsqrt(ms + NORM_EPS) * g_ref[...]).astype(BF16)
    mixed = None
    for j, (y_ref, p_ref) in enumerate(((ya_ref, pa_ref), (yb_ref, pb_ref), (yc_ref, pc_ref))):
        cols = slice(j * d, (j + 1) * d)
        logits = jnp.dot(xn, wm_ref[:, cols], preferred_element_type=F32) + bm_ref[:, cols]
        term = jax.nn.sigmoid(logits) * jnp.dot(y_ref[...], p_ref[...], preferred_element_type=F32)
        mixed = term if mixed is None else mixed + term
    x = x + jnp.dot(mixed.astype(BF16), wo_ref[...], preferred_element_type=F32)
    if final_norm:
        ms = jnp.mean(x * x, axis=-1, keepdims=True)
        x = x * lax.rsqrt(ms + NORM_EPS) * fg_ref[...]
    o_ref[...] = x


def _merge(x2d, ya, yb, yc, g, wm, bm, pa, pb, pc, wo, fg, final_norm):
    m = x2d.shape[0]
    tm = ROW_TILE

    def rows(arr):
        return pl.BlockSpec((tm, arr.shape[-1]), lambda i: (i, 0))

    def full(arr):
        return pl.BlockSpec(arr.shape, lambda i: (0, 0))

    consts = [g, wm, bm, pa, pb, pc, wo, fg]
    return pl.pallas_call(
        functools.partial(_merge_kernel, final_norm=final_norm),
        grid=(m // tm,),
        in_specs=[rows(a) for a in (x2d, ya, yb, yc)] + [full(a) for a in consts],
        out_specs=rows(x2d),
        out_shape=jax.ShapeDtypeStruct((m, D_MODEL), F32),
        compiler_params=_params(("parallel",)),
    )(x2d, ya, yb, yc, *consts)


def kernel(x, norm_g, w_in, b_merge, cmp_pe_k, cmp_w1_k, cmp_w2_k, cmp_pe_v, cmp_w1_v, cmp_w2_v, swa_sinks,
           rwkv_mu, rwkv_w0, rwkv_w2, rwkv_a0, rwkv_a2, rwkv_k_k, rwkv_k_a, rwkv_r_k, rwkv_ln_w, rwkv_ln_b,
           rwkv_v0, rwkv_v1, rwkv_v2, proj_a, proj_b, proj_c, w_out, final_g):
    b, t, d = x.shape
    depth = w_in.shape[0]
    m = b * t
    x2d = x.reshape(m, d)
    o_merge = _ref_offsets()["merge"]
    v_first = None
    for l in range(depth):
        w = jnp.concatenate([jnp.zeros((d, n), BF16) if start == N_IN else w_in[l][:, start:start + n].astype(BF16)
                             for start, n in _COL_RUNS], axis=1)
        g_l = norm_g[l].reshape(1, d)
        outs = iter(_in_proj(x2d, g_l, w))
        seg = {}
        for name, _, _, parts in _SEGS:
            arrs = [next(outs) for _ in range(parts)]
            arrs = [a.reshape(b, -1, a.shape[-1]) for a in arrs]
            seg[name] = arrs[0] if parts == 1 else arrs

        pe = jnp.stack([cmp_pe_k[l], cmp_pe_v[l]]).reshape(2, 1, CMP_BLOCK * HEAD_DIM)
        kcvc = _compress(seg["a_kv_cmp"], pe, jnp.stack([cmp_w1_k[l], cmp_w1_v[l]]),
                         jnp.stack([cmp_w2_k[l], cmp_w2_v[l]]))
        y_a, y_b = _attention(seg["a_q"], seg["a_z"], seg["a_gate"], seg["a_kv_slc"], seg["a_kv_win"], kcvc,
                              seg["b_q"], seg["b_z"], seg["b_kv"], swa_sinks[l])

        row = lambda a: a.reshape(1, -1)
        p = {"mu": row(rwkv_mu[l]), "w0": row(rwkv_w0[l]), "w2": rwkv_w2[l], "a0": row(rwkv_a0[l]),
             "a2": rwkv_a2[l], "k_k": row(rwkv_k_k[l]), "k_a": row(rwkv_k_a[l]), "r_k": row(rwkv_r_k[l]),
             "ln_w": row(rwkv_ln_w[l]), "ln_b": row(rwkv_ln_b[l])}
        if l > 0:
            p.update(v0=row(rwkv_v0[l - 1]), v1=rwkv_v1[l - 1], v2=rwkv_v2[l - 1])
        y_c, v_first = _rwkv(seg["c_shift"], seg["c_z"], v_first, p)

        w_merge = w_in[l][:, o_merge:o_merge + 3 * d].astype(BF16)
        x2d = _merge(x2d, y_a.reshape(m, -1), y_b.reshape(m, -1), y_c.reshape(m, -1), g_l, w_merge,
                     b_merge[l].reshape(1, -1), proj_a[l].astype(BF16), proj_b[l].astype(BF16),
                     proj_c[l].astype(BF16), w_out[l].astype(BF16), final_g.reshape(1, d),
                     final_norm=(l == depth - 1))
    return x2d.reshape(b, t, d)
```

```python
import functools

import numpy as np
import jax
import jax.numpy as jnp
from jax import lax
from jax.experimental import pallas as pl
from jax.experimental.pallas import tpu as pltpu

F32 = jnp.float32
BF16 = jnp.bfloat16

D_MODEL = 1024
HEAD_DIM = 64
NSA_HEADS = 8
NSA_GROUPS = 2
NSA_HG = NSA_HEADS // NSA_GROUPS
CMP_STRIDE = 16
CMP_BLOCK = 32
CMP_HIDDEN = 128
SLC_BLOCK = 64
SLC_TOPN = 16
NSA_WINDOW = 512
SWA_HEADS = 4
SWA_GROUPS = 2
SWA_HG = SWA_HEADS // SWA_GROUPS
SWA_WINDOW = 128
RWKV_HEADS = 4
RWKV_WIDTH = RWKV_HEADS * HEAD_DIM
DECAY_LORA = 64
ICLR_LORA = 64
VRES_LORA = 32
SHIFT_WIDTH = 3 * RWKV_WIDTH + DECAY_LORA + ICLR_LORA
NORM_EPS = 1e-6
GN_EPS = 64e-5
NEG_INF = -1e30
FORCE = 1e9

_REF_SEGMENTS = (
    ("a_q", 512), ("a_kv_cmp", 256), ("a_kv_slc", 256), ("a_kv_win", 256), ("a_gate", 24),
    ("a_z", 512), ("b_q", 256), ("b_kv", 256), ("b_z", 256), ("c_shift", SHIFT_WIDTH),
    ("c_z", 256), ("merge", 3 * D_MODEL),
)
N_IN = sum(w for _, w in _REF_SEGMENTS)

VMEM_LIMIT = 56 * 1024 * 1024
ROW_TILE = 512
ATT_TILE = 256
ATT_ROWS = 128
RWKV_CHUNK = 64
RWKV_STEP_CHUNKS = 8
RWKV_GROUP_CHUNKS = 1
RWKV_STAGGER = 1
RWKV_SUB = 16


def _ref_offsets():
    out, off = {}, 0
    for name, width in _REF_SEGMENTS:
        out[name] = off
        off += width
    return out


def _kv_interleave(base):
    idx = []
    for g in range(2):
        idx += list(range(base + 128 + g * 64, base + 128 + (g + 1) * 64))
        idx += list(range(base + g * 64, base + (g + 1) * 64))
    return idx


def _projection_layout():
    o = _ref_offsets()
    segs, idx = [], []

    def add(name, cols, dtype, parts=1):
        segs.append((name, len(cols), dtype, parts))
        idx.extend(cols)

    add("a_q", list(range(o["a_q"], o["a_q"] + 512)), BF16)
    add("a_z", list(range(o["a_z"], o["a_z"] + 512)), F32)
    add("a_kv_cmp", list(range(o["a_kv_cmp"], o["a_kv_cmp"] + 256)), BF16, parts=4)
    add("a_kv_slc", _kv_interleave(o["a_kv_slc"]), BF16)
    add("a_kv_win", _kv_interleave(o["a_kv_win"]), BF16)
    gate = []
    for g in range(NSA_GROUPS):
        cols = [N_IN] * 128
        for br in range(3):
            for h in range(NSA_HG):
                cols[br * NSA_HG + h] = o["a_gate"] + br * NSA_HEADS + g * NSA_HG + h
        gate += cols
    add("a_gate", gate, F32)
    add("b_q", list(range(o["b_q"], o["b_q"] + 256)), BF16)
    add("b_kv", _kv_interleave(o["b_kv"]), BF16)
    add("b_z", list(range(o["b_z"], o["b_z"] + 256)), F32)
    add("c_shift", list(range(o["c_shift"], o["c_shift"] + SHIFT_WIDTH)), F32)
    add("c_z", list(range(o["c_z"], o["c_z"] + 256)), F32)
    return tuple(segs), np.asarray(idx, np.int32)


_SEGS, _COL_IDX = _projection_layout()
_N_PROJ = int(_COL_IDX.shape[0])


def _column_runs(idx):
    runs, i = [], 0
    while i < len(idx):
        j = i + 1
        while j < len(idx) and (idx[j] == idx[j - 1] + 1 if idx[i] != N_IN else idx[j] == N_IN):
            j += 1
        runs.append((int(idx[i]), j - i))
        i = j
    return runs


_COL_RUNS = _column_runs(_COL_IDX)


def _params(sem):
    return pltpu.CompilerParams(dimension_semantics=sem, vmem_limit_bytes=VMEM_LIMIT)


def _in_proj_kernel(x_ref, g_ref, w_ref, *refs):
    out_refs, chunk_sc = refs[:-1], refs[-1]
    x = x_ref[...]
    ms = jnp.mean(x * x, axis=-1, keepdims=True)
    xn = (x * lax.rsqrt(ms + NORM_EPS) * g_ref[...]).astype(BF16)
    off = 0
    refs = iter(out_refs)
    for _, width, _, parts in _SEGS:
        outs = [next(refs) for _ in range(parts)]
        for c0 in range(0, width, 512):
            cw = min(512, width - c0)
            val = jnp.dot(xn, w_ref[:, off + c0:off + c0 + cw], preferred_element_type=F32)
            if parts == 1:
                outs[0][:, c0:c0 + cw] = val.astype(outs[0].dtype)
            else:
                pw = width // parts
                per_slab = 128 // pw
                for h in range(width // 128):
                    chunk_sc[h] = val[:, h * 128:(h + 1) * 128]
                n_chunks = val.shape[0] // CMP_STRIDE
                for l in range(CMP_STRIDE):
                    for h in range(width // 128):
                        rows = chunk_sc[h, pl.ds(l, n_chunks, stride=CMP_STRIDE), :]
                        for jj in range(per_slab):
                            o_ref = outs[h * per_slab + jj]
                            o_ref[:, l * pw:(l + 1) * pw] = rows[:, jj * pw:(jj + 1) * pw].astype(o_ref.dtype)
        off += width


def _in_proj(x2d, g, w):
    m = x2d.shape[0]
    tm = ROW_TILE
    out_shape, out_specs, chunk_width = [], [], 0
    for _, width, dt, parts in _SEGS:
        if parts == 1:
            out_shape.append(jax.ShapeDtypeStruct((m, width), dt))
            out_specs.append(pl.BlockSpec((tm, width), lambda i: (i, 0)))
        else:
            assert width <= 512 and chunk_width in (0, width)
            chunk_width = width
            cols = width // parts * CMP_STRIDE
            out_shape += [jax.ShapeDtypeStruct((m // CMP_STRIDE, cols), dt)] * parts
            out_specs += [pl.BlockSpec((tm // CMP_STRIDE, cols), lambda i: (i, 0))] * parts
    return pl.pallas_call(
        _in_proj_kernel,
        grid=(m // tm,),
        in_specs=[
            pl.BlockSpec((tm, D_MODEL), lambda i: (i, 0)),
            pl.BlockSpec((1, D_MODEL), lambda i: (0, 0)),
            pl.BlockSpec((D_MODEL, _N_PROJ), lambda i: (0, 0), pipeline_mode=pl.Buffered(1)),
        ],
        out_specs=out_specs,
        out_shape=out_shape,
        scratch_shapes=[pltpu.VMEM((chunk_width // 128, tm, 128), F32)],
        compiler_params=_params(("parallel",)),
    )(x2d, g, w)


def _compress_kernel(*refs):
    z_refs, (pe_ref, w1_ref, w2_ref, o_ref) = refs[:-4], refs[-4:]
    half = CMP_STRIDE * HEAD_DIM
    for j, z_ref in enumerate(z_refs):
        kv = j // NSA_GROUPS
        z = z_ref[...]
        w1 = w1_ref[kv]
        first = _dot_exact_lhs(z, w1[:half], 3)
        second = _dot_exact_lhs(z, w1[half:], 3)
        n = z.shape[0]
        hid = first + pltpu.roll(second, n - 1, 0)
        hid = hid + _dot3(_split(pe_ref[kv]), _split(w1))
        act = hid * jax.nn.sigmoid(hid)
        o_ref[j] = _dot3(_split(act), _split(w2_ref[kv]))


def _compress(kv_cmp, pe, w1, w2):
    zs = list(kv_cmp)
    b, nch, _ = zs[0].shape

    def full(arr):
        return pl.BlockSpec(arr.shape, lambda i: (0,) * arr.ndim)

    return pl.pallas_call(
        _compress_kernel,
        grid=(b,),
        in_specs=[pl.BlockSpec((None, nch, CMP_STRIDE * HEAD_DIM), lambda i: (i, 0, 0)) for _ in zs]
        + [full(pe), full(w1), full(w2)],
        out_specs=pl.BlockSpec((None, len(zs), nch, HEAD_DIM), lambda i: (i, 0, 0, 0)),
        out_shape=jax.ShapeDtypeStruct((b, len(zs), nch, HEAD_DIM), F32),
        compiler_params=_params(("parallel",)),
    )(*zs, pe, w1, w2)


def _widen_queries(q, n_heads):
    zeros = jnp.zeros((q.shape[0], HEAD_DIM), q.dtype)
    return [jnp.concatenate([zeros, q[:, h * HEAD_DIM:(h + 1) * HEAD_DIM]], axis=1) for h in range(n_heads)]


def _pair_values(kv):
    swapped = jnp.concatenate([kv[:, HEAD_DIM:], kv[:, :HEAD_DIM]], axis=1)
    lane = lax.broadcasted_iota(jnp.int32, kv.shape, 1)
    low = jnp.where(lane < HEAD_DIM, 1.0, 0.0).astype(kv.dtype)
    high = jnp.where(lane < HEAD_DIM, 0.0, 1.0).astype(kv.dtype)
    even = jnp.concatenate([kv * low, low], axis=1)
    odd = jnp.concatenate([swapped * high, high], axis=1)
    return even, odd


def _pair_lanes(even, odd):
    low = lax.broadcasted_iota(jnp.int32, even.shape, 1) < HEAD_DIM
    return jnp.where(low, even, odd)


def _nt(a, b):
    return lax.dot_general(a, b, (((1,), (1,)), ((), ())), preferred_element_type=F32)


def _rounds(streams, stagger=0):
    results = [None] * len(streams)
    waiting = list(enumerate(streams))
    live = {}
    rounds = 0
    while waiting or live:
        while waiting and waiting[0][0] * stagger <= rounds:
            idx, stream = waiting.pop(0)
            live[idx] = stream
        for idx in list(live):
            try:
                next(live[idx])
            except StopIteration as stop:
                results[idx] = stop.value
                del live[idx]
        rounds += 1
        yield
    return results


def _interleave(*streams, stagger=0):
    rounds = _rounds(streams, stagger)
    while True:
        try:
            next(rounds)
        except StopIteration as stop:
            return stop.value


def _band_block(qa_even, qa_odd, keys, val_even, val_odd, bias, sinks=None):
    s = [_nt(qa, keys) + bias for qa in (qa_even, qa_odd)]
    yield
    m = [jnp.max(x, axis=-1, keepdims=True) for x in s]
    if sinks is not None:
        m = [jnp.maximum(x, sinks[e]) for e, x in enumerate(m)]
    p = [jnp.exp(x - mx).astype(BF16) for x, mx in zip(s, m)]
    yield
    acc = (jnp.dot(p[0], val_even, preferred_element_type=F32)
           + jnp.dot(p[1], val_odd, preferred_element_type=F32))
    denom = acc[:, 2 * HEAD_DIM:]
    if sinks is not None:
        shape = (acc.shape[0], 2 * HEAD_DIM)
        denom = denom + _pair_lanes(jnp.broadcast_to(jnp.exp(sinks[0] - m[0]), shape),
                                    jnp.broadcast_to(jnp.exp(sinks[1] - m[1]), shape))
    return acc[:, :2 * HEAD_DIM] / denom


def _band_blocks(blocks, sinks=None):
    return (yield from _rounds([_band_block(*blk, sinks) for blk in blocks], stagger=1))


def _band_bias_table(deltas, rows, width, window):
    r = np.arange(rows)[:, None]
    c = np.arange(width)[None, :]
    out = [np.where((c <= r + d) & (c > r + d - window), 0.0, NEG_INF) for d in deltas]
    return np.stack(out).astype(np.float32)


def _attention_kernel(sink_ref, q_ref, z_ref, gate_ref, kvs_ref, kvw_ref, kcvc_ref, bq_ref, bz_ref,
                      bkv_ref, negexp_ref, ovl_ref, wbias_ref, cbias_ref, gexp_ref, bbias_ref,
                      o_ref, ob_ref, srhs_sc, spe_sc, spo_sc, wpe_sc, wpo_sc, bpe_sc, bpo_sc, m_sc, acc_sc):
    i = pl.program_id(1)
    tq = ATT_TILE
    rb = ATT_ROWS
    n_rb = tq // rb
    n_cmp_pad = kcvc_ref.shape[1]
    n_slc = ovl_ref.shape[0]
    n_pairs = NSA_HG // 2
    groups = range(NSA_GROUPS)
    wd = NSA_HG * HEAD_DIM
    wb = SWA_HG * HEAD_DIM
    kv_lanes = [slice(g * 2 * HEAD_DIM, (g + 1) * 2 * HEAD_DIM) for g in groups]

    @pl.when(i == 0)
    def _():
        for g in groups:
            kvs = kvs_ref[:, kv_lanes[g]]
            srhs_sc[g] = jnp.concatenate([kvs, negexp_ref[...]], axis=1)
            spe_sc[g], spo_sc[g] = _pair_values(kvs)
            wpe_sc[g], wpo_sc[g] = _pair_values(kvw_ref[:, kv_lanes[g]])
            bpe_sc[g], bpo_sc[g] = _pair_values(bkv_ref[:, kv_lanes[g]])

    scale = jnp.asarray(HEAD_DIM ** -0.5, BF16)
    qas = [_widen_queries(q_ref[:, g * wd:(g + 1) * wd] * scale, NSA_HG) for g in groups]

    def band_inputs(g, kv_ref, pe_sc, po_sc, bias_ref, window, r0):
        q0 = i * tq + r0
        k0 = pl.multiple_of(jnp.maximum(q0 - window, 0), rb)
        width = window + rb
        variant = jnp.minimum(q0 // rb, bias_ref.shape[0] - 1)
        return (kv_ref[pl.ds(k0, width), kv_lanes[g]], pe_sc[g, pl.ds(k0, width), :],
                po_sc[g, pl.ds(k0, width), :], bias_ref[variant])

    def window_stream(g):
        bands = [band_inputs(g, kvw_ref, wpe_sc, wpo_sc, wbias_ref, NSA_WINDOW, r0) for r0 in range(0, tq, rb)]
        blocks = [(qas[g][2 * pair][r0:r0 + rb], qas[g][2 * pair + 1][r0:r0 + rb]) + bands[r0 // rb]
                  for pair in range(n_pairs) for r0 in range(0, tq, rb)]
        outs = yield from _band_blocks(blocks)
        return jnp.concatenate([jnp.concatenate(outs[pair * n_rb:(pair + 1) * n_rb], axis=0)
                                for pair in range(n_pairs)], axis=1)

    def sink_window_stream(g):
        lanes = slice(g * wb, (g + 1) * wb)
        qbs = _widen_queries(bq_ref[:, lanes] * scale, SWA_HG)
        sinks = (sink_ref[g * SWA_HG], sink_ref[g * SWA_HG + 1])
        blocks = [(qbs[0][r0:r0 + rb], qbs[1][r0:r0 + rb])
                  + band_inputs(g, bkv_ref, bpe_sc, bpo_sc, bbias_ref, SWA_WINDOW, r0) for r0 in range(0, tq, rb)]
        outs = yield from _band_blocks(blocks, sinks)
        y = jnp.concatenate(outs, axis=0)
        zb = bz_ref[:, lanes]
        ob_ref[:, lanes] = (y * (zb * jax.nn.sigmoid(zb))).astype(ob_ref.dtype)

    def compressed_stream(g):
        zeros_k = jnp.zeros((n_cmp_pad, HEAD_DIM), BF16)
        kc_both = jnp.concatenate([jnp.concatenate([zeros_k, t], axis=1) for t in _split(kcvc_ref[g])], axis=0)
        vc = kcvc_ref[NSA_GROUPS + g].astype(BF16)
        vc_even = jnp.concatenate([vc, zeros_k], axis=1)
        vc_odd = jnp.concatenate([zeros_k, vc], axis=1)
        tpos = i * tq + lax.broadcasted_iota(jnp.int32, (tq, n_cmp_pad), 0)
        cmp_end = lax.broadcasted_iota(jnp.int32, (tq, n_cmp_pad), 1) * CMP_STRIDE + (CMP_BLOCK - 1)
        valid_c = cmp_end <= tpos
        s = [_nt(qa, kc_both) for qa in qas[g]]
        s = [x[:, :n_cmp_pad] + x[:, n_cmp_pad:] for x in s]
        yield
        s = [jnp.where(valid_c, x, NEG_INF) for x in s]
        m = [jnp.max(x, axis=-1, keepdims=True) for x in s]
        yield
        p = [jnp.where(valid_c, jnp.exp(x - mx), 0.0) for x, mx in zip(s, m)]
        denom = [jnp.sum(x, axis=-1, keepdims=True) for x in p]
        yield
        p = [x * (1.0 / jnp.where(d > 0, d, 1.0)) for x, d in zip(p, denom)]
        p_sum = (p[0] + p[1]) + (p[2] + p[3])
        pb = [x.astype(BF16) for x in p]
        o_cmp = jnp.concatenate(
            [jnp.dot(pb[2 * pair], vc_even, preferred_element_type=F32)
             + jnp.dot(pb[2 * pair + 1], vc_odd, preferred_element_type=F32) for pair in range(n_pairs)], axis=1)
        yield
        imp = None
        rest = p_sum
        for _ in range(3):
            term = rest.astype(BF16)
            rest = rest - term.astype(F32)
            part = _nt(ovl_ref[...], term)
            imp = part if imp is None else imp + part
        yield
        blk = lax.broadcasted_iota(jnp.int32, (n_slc, tq), 0)
        cur = (i * tq + lax.broadcasted_iota(jnp.int32, (n_slc, tq), 1)) // SLC_BLOCK
        forced = (blk == 0) | (blk == cur) | (blk == cur - 1)
        score = jnp.where(forced, FORCE, jnp.where(blk <= cur, imp, -FORCE))
        ranks = [jnp.zeros((n_slc, tq), F32) for _ in range(4)]
        for r in range(n_slc):
            row = score[r:r + 1, :]
            ahead = (row > score) | ((row == score) & (r < blk))
            ranks[r % 4] = ranks[r % 4] + jnp.where(ahead, 1.0, 0.0)
            if r % 4 == 3:
                yield
        rank = (ranks[0] + ranks[1]) + (ranks[2] + ranks[3])
        n_sel = min(SLC_TOPN, n_slc)
        unsel_t = jnp.where((rank < n_sel) & (blk <= cur), 0.0, 1.0)
        unsel_t = jnp.concatenate([unsel_t, jnp.zeros((128 - n_slc, tq), F32)], axis=0)
        return o_cmp, jnp.transpose(unsel_t).astype(BF16)

    results = _interleave(*[stream(g) for g in groups
                            for stream in (window_stream, sink_window_stream, compressed_stream)])
    o_win = [results[3 * g] for g in groups]
    o_cmp = [results[3 * g + 2][0] for g in groups]
    unsel = [results[3 * g + 2][1] for g in groups]

    lhs = {(g, h): jnp.concatenate([qas[g][h], unsel[g]], axis=1) for g in groups for h in range(NSA_HG)}
    m_sc[...] = jnp.full(m_sc.shape, NEG_INF, F32)
    acc_sc[...] = jnp.zeros(acc_sc.shape, F32)
    subs = [(g, h, r0) for g in groups for h in range(NSA_HG) for r0 in range(0, tq, rb)]

    def slc_keys(start, width, bias_lanes):
        rhs = [srhs_sc[g, pl.ds(start, width), :] for g in groups]
        val_even = [spe_sc[g, pl.ds(start, width), :] for g in groups]
        val_odd = [spo_sc[g, pl.ds(start, width), :] for g in groups]
        s = {}
        for g, h, r0 in subs:
            s[g, h, r0] = _nt(lhs[g, h][r0:r0 + rb], rhs[g])
            if bias_lanes is not None:
                s[g, h, r0] = s[g, h, r0] + cbias_ref[r0:r0 + rb, bias_lanes:bias_lanes + width]
        m_prev = {(g, h, r0): m_sc[g * NSA_HG + h, r0:r0 + rb] for g, h, r0 in subs}
        m_next = {key: jnp.maximum(m_prev[key], jnp.max(s[key], axis=-1, keepdims=True)) for key in subs}
        p = {key: jnp.exp(s[key] - jnp.concatenate([m_next[key]] * (width // 128), axis=1)).astype(BF16)
             for key in subs}
        alpha = {key: jnp.exp(m_prev[key] - m_next[key]) for key in subs}
        for g, h, r0 in subs:
            m_sc[g * NSA_HG + h, r0:r0 + rb] = m_next[g, h, r0]
        for g in groups:
            for pair in range(n_pairs):
                for r0 in range(0, tq, rb):
                    rows = slice(r0, r0 + rb)
                    decay = _pair_lanes(alpha[g, 2 * pair, r0], alpha[g, 2 * pair + 1, r0])
                    decay = jnp.concatenate([decay, decay], axis=1)
                    acc_sc[g * n_pairs + pair, rows] = (
                        decay * acc_sc[g * n_pairs + pair, rows]
                        + jnp.dot(p[g, 2 * pair, r0], val_even[g], preferred_element_type=F32)
                        + jnp.dot(p[g, 2 * pair + 1, r0], val_odd[g], preferred_element_type=F32))

    def wide(j, carry):
        slc_keys(pl.multiple_of(j * (2 * tq), 2 * tq), 2 * tq, None)
        return carry

    lax.fori_loop(0, i // 2, wide, 0)

    @pl.when(i % 2 == 0)
    def _():
        slc_keys(pl.multiple_of(i * tq, tq), tq, tq)

    @pl.when(i % 2 == 1)
    def _():
        slc_keys(pl.multiple_of((i - 1) * tq, 2 * tq), 2 * tq, 0)

    for g in groups:
        o_slc = []
        for pair in range(n_pairs):
            acc = acc_sc[g * n_pairs + pair]
            o_slc.append(acc[:, :2 * HEAD_DIM] / acc[:, 2 * HEAD_DIM:])
        o_slc = jnp.concatenate(o_slc, axis=1)
        sg = _split(jax.nn.sigmoid(gate_ref[:, g * 128:(g + 1) * 128]))
        gates = (jnp.dot(sg[0], gexp_ref[...], preferred_element_type=F32)
                 + jnp.dot(sg[1], gexp_ref[...], preferred_element_type=F32))
        y = gates[:, :wd] * o_cmp[g] + gates[:, wd:2 * wd] * o_slc + gates[:, 2 * wd:] * o_win[g]
        z = z_ref[:, g * wd:(g + 1) * wd]
        o_ref[:, g * wd:(g + 1) * wd] = (y * (z * jax.nn.sigmoid(z))).astype(o_ref.dtype)


def _attention(q, z, gate, kv_slc, kv_win, kcvc, bq, bz, bkv, sinks):
    b, t, _ = q.shape
    tq = ATT_TILE
    assert t % (2 * tq) == 0 and t >= NSA_WINDOW + tq and NSA_WINDOW % tq == 0 and SWA_HG == 2
    assert NSA_GROUPS == SWA_GROUPS
    n_slc = t // SLC_BLOCK
    nch = t // CMP_STRIDE
    wd = NSA_HG * HEAD_DIM
    negexp = np.where(np.arange(t)[:, None] // SLC_BLOCK == np.arange(128)[None, :], -2.0 ** 100, 0.0)
    ci = np.arange(nch)[None, :] * CMP_STRIDE
    sj = np.arange(n_slc)[:, None] * SLC_BLOCK
    overlap_t = ((ci < sj + SLC_BLOCK) & (ci + CMP_BLOCK > sj)).astype(np.float32)
    rb = ATT_ROWS
    assert NSA_WINDOW % rb == 0 and SWA_WINDOW % rb == 0 and tq % rb == 0
    wbias = _band_bias_table(range(0, NSA_WINDOW + 1, rb), rb, NSA_WINDOW + rb, NSA_WINDOW)
    cbias = np.concatenate([np.zeros((tq, tq), np.float32), _band_bias_table([0], tq, tq, t)[0]], axis=1)
    bbias = _band_bias_table(range(0, SWA_WINDOW + 1, rb), rb, SWA_WINDOW + rb, SWA_WINDOW)
    gexp = np.zeros((128, 3 * wd), np.float32)
    for br in range(3):
        for h in range(NSA_HG):
            gexp[br * NSA_HG + h, br * wd + h * HEAD_DIM:br * wd + (h + 1) * HEAD_DIM] = 1.0

    def const(arr):
        return pl.BlockSpec(arr.shape, lambda bi, i: (0,) * arr.ndim)

    def tile(arr):
        return pl.BlockSpec((None, tq, arr.shape[-1]), lambda bi, i: (bi, i, 0))

    def whole(arr):
        return pl.BlockSpec((None,) + arr.shape[1:], lambda bi, i: (bi,) + (0,) * (arr.ndim - 1))

    consts = [jnp.asarray(negexp, BF16), jnp.asarray(overlap_t, BF16), jnp.asarray(wbias),
              jnp.asarray(cbias), jnp.asarray(gexp, BF16), jnp.asarray(bbias)]
    key_scratch = pltpu.VMEM((NSA_GROUPS, t, 256), BF16)
    return pl.pallas_call(
        _attention_kernel,
        grid=(b, t // tq),
        in_specs=[pl.BlockSpec(memory_space=pltpu.SMEM), tile(q), tile(z), tile(gate), whole(kv_slc),
                  whole(kv_win), whole(kcvc), tile(bq), tile(bz), whole(bkv)] + [const(a) for a in consts],
        out_specs=(pl.BlockSpec((None, tq, 512), lambda bi, i: (bi, i, 0)),
                   pl.BlockSpec((None, tq, 256), lambda bi, i: (bi, i, 0))),
        out_shape=(jax.ShapeDtypeStruct((b, t, 512), BF16), jax.ShapeDtypeStruct((b, t, 256), BF16)),
        scratch_shapes=[key_scratch] * 7 + [
            pltpu.VMEM((NSA_GROUPS * NSA_HG, tq, 128), F32),
            pltpu.VMEM((NSA_GROUPS * NSA_HG // 2, tq, 256), F32),
        ],
        compiler_params=_params(("parallel", "arbitrary")),
    )(sinks, q, z, gate, kv_slc, kv_win, kcvc, bq, bz, bkv, *consts)


_NN = (((1,), (0,)), ((), ()))
_NT = (((1,), (1,)), ((), ()))


def _split(a):
    hi = a.astype(BF16)
    lo = (a - hi.astype(F32)).astype(BF16)
    return hi, lo


def _dot(a, b, dims=_NN):
    return lax.dot_general(a, b, dims, preferred_element_type=F32)


def _dot3(a, b, dims=_NN):
    rows = a[0].shape[0]
    both = _dot(jnp.concatenate([a[0], a[1]], axis=0), b[0], dims)
    return (both[:rows] + both[rows:]) + _dot(a[0], b[1], dims)


def _dot_exact_lhs(a, b, n_terms):
    out = None
    for _ in range(n_terms):
        term = b.astype(BF16)
        b = b - term.astype(F32)
        part = _dot(a, term)
        out = part if out is None else out + part
    return out


def _dot_exact_rhs(x, ones):
    hi, lo = _split(x)
    return _dot(hi, ones) + _dot(lo, ones)


def _block_diag(pair, head_masks):
    return tuple(jnp.concatenate([x * m for m in head_masks], axis=0) for x in pair)


def _each(fn, *lists):
    return [fn(*items) for items in zip(*lists)]


def _unit_lower_inverse(lows, eye, same_sub, bd):
    diag = _each(lambda low: jnp.where(same_sub, low, 0.0), lows)
    off = _each(lambda low, d: low - d, lows, diag)
    inv = _each(lambda d: eye + d, diag)
    power = diag
    span = 2
    while span < RWKV_SUB:
        ps = _each(_split, power)
        power = _each(lambda p: _dot3(p, bd(p)), ps)
        yield
        inv = _each(lambda iv, pw: _dot3(_split(iv), bd(_split(eye + pw))), inv, power)
        yield
        span *= 2
    inv_s = _each(_split, inv)
    cross = _each(lambda iv, o: _dot3(iv, bd(_split(o))), inv_s, off)
    yield
    total = _each(lambda cr: eye + cr, cross)
    power = cross
    span = 2
    while span < RWKV_CHUNK // RWKV_SUB:
        ps = _each(_split, power)
        power = _each(lambda p: _dot3(p, bd(p)), ps)
        yield
        total = _each(lambda tt, pw: _dot3(_split(tt), bd(_split(eye + pw))), total, power)
        yield
        span *= 2
    return _each(lambda tt, iv: _dot3(_split(tt), bd(iv)), total, inv_s)


def _rwkv_chunk_maps(a_t, r_t, b_t, k_t, b_e, k_e, v, p_end, cst):
    C = RWKV_CHUNK
    eye, strict, incl, same_sub, head_masks = cst
    bd = lambda pair: _block_diag(pair, head_masks)

    ar = _each(lambda a, r: _split(jnp.concatenate([a, r], axis=0)), a_t, r_t)
    g_b = _each(lambda x, b: _dot3(x, bd(_split(b)), _NT), ar, b_t)
    g_k = _each(lambda x, k: _dot3(x, bd(_split(k)), _NT), ar, k_t)
    yield
    a_ab = _each(lambda g: jnp.where(strict, g[:C], 0.0), g_b)
    a_rb = _each(lambda g: _split(jnp.where(incl, g[C:], 0.0)), g_b)
    a_ak = _each(lambda g: _split(jnp.where(strict, g[:C], 0.0)), g_k)
    a_rk = _each(lambda g: _split(jnp.where(incl, g[C:], 0.0)), g_k)
    v_bd = _each(lambda x: bd(_split(x)), v)
    akv = _each(_dot3, a_ak, v_bd)
    yield
    t_inv = yield from _unit_lower_inverse(a_ab, eye, same_sub, bd)
    t_inv = _each(_split, t_inv)
    yield
    w_m = _each(lambda t, a: _dot3(t, bd(_split(a))), t_inv, a_t)
    u0 = _each(lambda t, x: _dot3(t, bd(_split(x))), t_inv, akv)
    yield
    r_m = _each(lambda r, g, w: r + _dot3(g, bd(_split(w))), r_t, a_rb, w_m)
    y0 = _each(lambda g, u, gk, vb: _dot3(g, bd(_split(u))) + _dot3(gk, vb), a_rb, u0, a_rk, v_bd)
    yield
    bk = _each(lambda b, k: _split(jnp.transpose(jnp.concatenate([b, k], axis=0))), b_e, k_e)
    m_add = _each(lambda x, w: _dot3((x[0][:, :C], x[1][:, :C]), _split(w)), bk, w_m)
    n_all = _each(lambda x, u, vv: _dot3(x, _split(jnp.concatenate([u, vv], axis=0))), bk, u0, v)
    yield
    keep = [m.astype(F32) for m in head_masks]

    def own_blocks(full):
        parts = [full[h * HEAD_DIM:(h + 1) * HEAD_DIM, :] * keep[h] for h in range(RWKV_HEADS)]
        return (parts[0] + parts[1]) + (parts[2] + parts[3])

    m_l = _each(lambda m, p: own_blocks(m) + eye * p, m_add, p_end)
    n_l = _each(own_blocks, n_all)
    lhs = _each(lambda r, m: _split(jnp.concatenate([r, m], axis=0)), r_m, m_l)
    return list(zip(lhs, y0, n_l))


def _rwkv_kernel(*refs, has_vres):
    if has_vres:
        (feat_ref, prev_ref, z_ref, vfirst_ref, mu_ref, w0_ref, w2_ref, a0_ref, a2_ref, kk_ref, ka_ref,
         rk_ref, lnw_ref, lnb_ref, hsum_ref, v0_ref, v1_ref, v2_ref, y_ref, state) = refs
    else:
        (feat_ref, prev_ref, z_ref, mu_ref, w0_ref, w2_ref, a0_ref, a2_ref, kk_ref, ka_ref,
         rk_ref, lnw_ref, lnb_ref, hsum_ref, y_ref, vout_ref, state) = refs
    c = pl.program_id(1)
    C = RWKV_CHUNK
    W = RWKV_WIDTH
    n_tok = feat_ref.shape[0]
    group = RWKV_GROUP_CHUNKS * C

    @pl.when(c == 0)
    def _():
        state[...] = jnp.zeros(state.shape, F32)

    def lora(x, w_ref):
        return _dot3(_split(x), _split(w_ref[...]))

    hsum = hsum_ref[...]

    def head_sum(x):
        return _dot_exact_rhs(x, hsum)

    feat = feat_ref[...]
    row = lax.broadcasted_iota(jnp.int32, feat.shape, 0)
    last_prev = jnp.where(c == 0, 0.0, prev_ref[7:8, :])
    prev = jnp.where(row == 0, last_prev, pltpu.roll(feat, 1, 0))
    xs = feat + (prev - feat) * mu_ref[...]

    ti = lax.broadcasted_iota(jnp.int32, (group, group), 0)
    tj = lax.broadcasted_iota(jnp.int32, (group, group), 1)
    tri = jnp.where((ti >= tj) & (ti // C == tj // C), 1.0, 0.0).astype(BF16)
    t_idx = lax.broadcasted_iota(jnp.int32, (C, W), 0)
    i_idx = lax.broadcasted_iota(jnp.int32, (C, W), 1) % HEAD_DIM
    lane_head = lax.broadcasted_iota(jnp.int32, (C, W), 1) // HEAD_DIM
    cst = (
        jnp.where(t_idx == i_idx, 1.0, 0.0),
        t_idx > i_idx,
        t_idx >= i_idx,
        (t_idx // RWKV_SUB) == (i_idx // RWKV_SUB),
        [jnp.where(lane_head == h, 1.0, 0.0).astype(BF16) for h in range(RWKV_HEADS)],
    )
    carry = {"state": state[...], "turn": 0}

    def token_group(idx):
        rows = slice(idx * group, (idx + 1) * group)
        x = xs[rows]
        r = x[:, :W]
        k = x[:, W:2 * W]
        v = x[:, 2 * W:3 * W]
        wd = x[:, 3 * W:3 * W + DECAY_LORA]
        ad = x[:, 3 * W + DECAY_LORA:]
        pre = -(w0_ref[...] + lora(jnp.tanh(wd), w2_ref))
        softplus = jnp.maximum(pre, 0.0) + jnp.log(1.0 + jnp.exp(-jnp.abs(pre)))
        logw = -jnp.exp(-softplus - 0.5)
        yield
        if has_vres:
            mix = jax.nn.sigmoid(v0_ref[...] + lora(lora(v, v1_ref), v2_ref))
            v = v + (vfirst_ref[rows, :] - v) * mix
        else:
            vout_ref[rows, :] = v
        alpha = jax.nn.sigmoid(a0_ref[...] + lora(ad, a2_ref))
        kk = k * kk_ref[...]
        kk = kk / jnp.maximum(jnp.sqrt(head_sum(kk * kk)), 1e-12)
        k = k * (1.0 + (alpha - 1.0) * ka_ref[...])
        yield
        cum = _dot_exact_lhs(tri, logw, 3)
        yield
        a_t = -kk * jnp.exp(cum - logw)
        r_t = r * jnp.exp(cum)
        inv_decay = jnp.exp(-cum)
        b_raw = kk * alpha
        b_t = b_raw * inv_decay
        k_t = k * inv_decay
        chunks = [slice(j * C, (j + 1) * C) for j in range(RWKV_GROUP_CHUNKS)]
        cum_end = [cum[cr.stop - 1:cr.stop, :] for cr in chunks]
        to_end = [jnp.exp(ce - cum[cr]) for ce, cr in zip(cum_end, chunks)]
        yield
        maps = yield from _rwkv_chunk_maps(
            [a_t[cr] for cr in chunks], [r_t[cr] for cr in chunks], [b_t[cr] for cr in chunks],
            [k_t[cr] for cr in chunks], [b_raw[cr] * te for cr, te in zip(chunks, to_end)],
            [k[cr] * te for cr, te in zip(chunks, to_end)], [v[cr] for cr in chunks],
            [jnp.exp(ce) for ce in cum_end], cst)
        while carry["turn"] != idx:
            yield
        s_l = carry["state"]
        ys = []
        for lhs, y0, n_l in maps:
            out = _dot3(lhs, _block_diag(_split(s_l), cst[4]))
            ys.append(out[:C] + y0)
            s_l = out[C:] + n_l
        carry["state"] = s_l
        carry["turn"] = idx + 1
        yield
        y = jnp.concatenate(ys, axis=0)
        inv_n = 1.0 / HEAD_DIM
        mean = head_sum(y) * inv_n
        cen = y - mean
        yield
        var = head_sum(cen * cen) * inv_n
        y = cen * lax.rsqrt(var + GN_EPS) * lnw_ref[...] + lnb_ref[...]
        y = y + head_sum(r * k * rk_ref[...]) * v
        z = z_ref[rows, :]
        y_ref[rows, :] = (y * (z * jax.nn.sigmoid(z))).astype(y_ref.dtype)

    _interleave(*[token_group(idx) for idx in range(n_tok // group)], stagger=RWKV_STAGGER)
    state[...] = carry["state"]


def _rwkv(feat, z, v_first, p):
    b, t, _ = feat.shape
    n_tok = RWKV_STEP_CHUNKS * RWKV_CHUNK
    assert t % n_tok == 0 and RWKV_CHUNK == HEAD_DIM
    W = RWKV_WIDTH
    has_vres = v_first is not None
    hsum = (np.arange(W)[:, None] // HEAD_DIM == np.arange(W)[None, :] // HEAD_DIM).astype(np.float32)

    def tok(width):
        return pl.BlockSpec((None, n_tok, width), lambda bi, c: (bi, c, 0))

    def full(arr):
        return pl.BlockSpec(arr.shape, lambda bi, c: (0,) * arr.ndim)

    args = [feat, feat, z]
    specs = [tok(SHIFT_WIDTH),
             pl.BlockSpec((None, 8, SHIFT_WIDTH), lambda bi, c: (bi, jnp.maximum(c * (n_tok // 8) - 1, 0), 0)),
             tok(W)]
    if has_vres:
        args.append(v_first)
        specs.append(tok(W))
    names = ["mu", "w0", "w2", "a0", "a2", "k_k", "k_a", "r_k", "ln_w", "ln_b"]
    consts = [p[n] for n in names] + [jnp.asarray(hsum, BF16)]
    if has_vres:
        consts += [p["v0"], p["v1"], p["v2"]]
    args += consts
    specs += [full(a) for a in consts]
    y_shape = jax.ShapeDtypeStruct((b, t, W), BF16)
    if has_vres:
        out_shape, out_specs = y_shape, tok(W)
    else:
        out_shape = (y_shape, jax.ShapeDtypeStruct((b, t, W), F32))
        out_specs = (tok(W), tok(W))
    res = pl.pallas_call(
        functools.partial(_rwkv_kernel, has_vres=has_vres),
        grid=(b, t // n_tok),
        in_specs=specs,
        out_specs=out_specs,
        out_shape=out_shape,
        scratch_shapes=[pltpu.VMEM((HEAD_DIM, W), F32)],
        compiler_params=_params(("parallel", "arbitrary")),
    )(*args)
    if has_vres:
        return res, v_first
    return res[0], res[1]


def _merge_kernel(x_ref, ya_ref, yb_ref, yc_ref, g_ref, wm_ref, bm_ref, pa_ref, pb_ref, pc_ref, wo_ref, fg_ref,
                  o_ref, *, final_norm):
    d = D_MODEL
    x = x_ref[...]
    ms = jnp.mean(x * x, axis=-1, keepdims=True)
    xn = (x * lax.rsqrt(ms + NORM_EPS) * g_ref[...]).astype(BF16)
    mixed = None
    for j, (y_ref, p_ref) in enumerate(((ya_ref, pa_ref), (yb_ref, pb_ref), (yc_ref, pc_ref))):
        cols = slice(j * d, (j + 1) * d)
        logits = jnp.dot(xn, wm_ref[:, cols], preferred_element_type=F32) + bm_ref[:, cols]
        term = jax.nn.sigmoid(logits) * jnp.dot(y_ref[...], p_ref[...], preferred_element_type=F32)
        mixed = term if mixed is None else mixed + term
    x = x + jnp.dot(mixed.astype(BF16), wo_ref[...], preferred_element_type=F32)
    if final_norm:
        ms = jnp.mean(x * x, axis=-1, keepdims=True)
        x = x * lax.rsqrt(ms + NORM_EPS) * fg_ref[...]
    o_ref[...] = x


def _merge(x2d, ya, yb, yc, g, wm, bm, pa, pb, pc, wo, fg, final_norm):
    m = x2d.shape[0]
    tm = ROW_TILE

    def rows(arr):
        return pl.BlockSpec((tm, arr.shape[-1]), lambda i: (i, 0))

    def full(arr):
        return pl.BlockSpec(arr.shape, lambda i: (0, 0))

    consts = [g, wm, bm, pa, pb, pc, wo, fg]
    return pl.pallas_call(
        functools.partial(_merge_kernel, final_norm=final_norm),
        grid=(m // tm,),
        in_specs=[rows(a) for a in (x2d, ya, yb, yc)] + [full(a) for a in consts],
        out_specs=rows(x2d),
        out_shape=jax.ShapeDtypeStruct((m, D_MODEL), F32),
        compiler_params=_params(("parallel",)),
    )(x2d, ya, yb, yc, *consts)


def kernel(x, norm_g, w_in, b_merge, cmp_pe_k, cmp_w1_k, cmp_w2_k, cmp_pe_v, cmp_w1_v, cmp_w2_v, swa_sinks,
           rwkv_mu, rwkv_w0, rwkv_w2, rwkv_a0, rwkv_a2, rwkv_k_k, rwkv_k_a, rwkv_r_k, rwkv_ln_w, rwkv_ln_b,
           rwkv_v0, rwkv_v1, rwkv_v2, proj_a, proj_b, proj_c, w_out, final_g):
    b, t, d = x.shape
    depth = w_in.shape[0]
    m = b * t
    x2d = x.reshape(m, d)
    o_merge = _ref_offsets()["merge"]
    v_first = None
    for l in range(depth):
        w = jnp.concatenate([jnp.zeros((d, n), BF16) if start == N_IN else w_in[l][:, start:start + n].astype(BF16)
                             for start, n in _COL_RUNS], axis=1)
        g_l = norm_g[l].reshape(1, d)
        outs = iter(_in_proj(x2d, g_l, w))
        seg = {}
        for name, _, _, parts in _SEGS:
            arrs = [next(outs) for _ in range(parts)]
            arrs = [a.reshape(b, -1, a.shape[-1]) for a in arrs]
            seg[name] = arrs[0] if parts == 1 else arrs

        pe = jnp.stack([cmp_pe_k[l], cmp_pe_v[l]]).reshape(2, 1, CMP_BLOCK * HEAD_DIM)
        kcvc = _compress(seg["a_kv_cmp"], pe, jnp.stack([cmp_w1_k[l], cmp_w1_v[l]]),
                         jnp.stack([cmp_w2_k[l], cmp_w2_v[l]]))
        y_a, y_b = _attention(seg["a_q"], seg["a_z"], seg["a_gate"], seg["a_kv_slc"], seg["a_kv_win"], kcvc,
                              seg["b_q"], seg["b_z"], seg["b_kv"], swa_sinks[l])

        row = lambda a: a.reshape(1, -1)
        p = {"mu": row(rwkv_mu[l]), "w0": row(rwkv_w0[l]), "w2": rwkv_w2[l], "a0": row(rwkv_a0[l]),
             "a2": rwkv_a2[l], "k_k": row(rwkv_k_k[l]), "k_a": row(rwkv_k_a[l]), "r_k": row(rwkv_r_k[l]),
             "ln_w": row(rwkv_ln_w[l]), "ln_b": row(rwkv_ln_b[l])}
        if l > 0:
            p.update(v0=row(rwkv_v0[l - 1]), v1=rwkv_v1[l - 1], v2=rwkv_v2[l - 1])
        y_c, v_first = _rwkv(seg["c_shift"], seg["c_z"], v_first, p)

        w_merge = w_in[l][:, o_merge:o_merge + 3 * d].astype(BF16)
        x2d = _merge(x2d, y_a.reshape(m, -1), y_b.reshape(m, -1), y_c.reshape(m, -1), g_l, w_merge,
                     b_merge[l].reshape(1, -1), proj_a[l].astype(BF16), proj_b[l].astype(BF16),
                     proj_c[l].astype(BF16), w_out[l].astype(BF16), final_g.reshape(1, d),
                     final_norm=(l == depth - 1))
    return x2d.reshape(b, t, d)
```

```python
import functools

import numpy as np
import jax
import jax.numpy as jnp
from jax import lax
from jax.experimental import pallas as pl
from jax.experimental.pallas import tpu as pltpu

F32 = jnp.float32
BF16 = jnp.bfloat16

D_MODEL = 1024
HEAD_DIM = 64
NSA_HEADS = 8
NSA_GROUPS = 2
NSA_HG = NSA_HEADS // NSA_GROUPS
CMP_STRIDE = 16
CMP_BLOCK = 32
CMP_HIDDEN = 128
SLC_BLOCK = 64
SLC_TOPN = 16
NSA_WINDOW = 512
SWA_HEADS = 4
SWA_GROUPS = 2
SWA_HG = SWA_HEADS // SWA_GROUPS
SWA_WINDOW = 128
RWKV_HEADS = 4
RWKV_WIDTH = RWKV_HEADS * HEAD_DIM
DECAY_LORA = 64
ICLR_LORA = 64
VRES_LORA = 32
SHIFT_WIDTH = 3 * RWKV_WIDTH + DECAY_LORA + ICLR_LORA
NORM_EPS = 1e-6
GN_EPS = 64e-5
NEG_INF = -1e30
FORCE = 1e9

_REF_SEGMENTS = (
    ("a_q", 512), ("a_kv_cmp", 256), ("a_kv_slc", 256), ("a_kv_win", 256), ("a_gate", 24),
    ("a_z", 512), ("b_q", 256), ("b_kv", 256), ("b_z", 256), ("c_shift", SHIFT_WIDTH),
    ("c_z", 256), ("merge", 3 * D_MODEL),
)
N_IN = sum(w for _, w in _REF_SEGMENTS)

VMEM_LIMIT = 56 * 1024 * 1024
ROW_TILE = 1024
ATT_TILE = 256
ATT_ROWS = 128
RWKV_CHUNK = 64
RWKV_STEP_CHUNKS = 8
RWKV_GROUP_CHUNKS = 1
RWKV_STAGGER = 1
RWKV_SUB = 16


def _ref_offsets():
    out, off = {}, 0
    for name, width in _REF_SEGMENTS:
        out[name] = off
        off += width
    return out


def _kv_interleave(base):
    idx = []
    for g in range(2):
        idx += list(range(base + 128 + g * 64, base + 128 + (g + 1) * 64))
        idx += list(range(base + g * 64, base + (g + 1) * 64))
    return idx


def _projection_layout():
    o = _ref_offsets()
    segs, idx = [], []

    def add(name, cols, dtype, parts=1):
        segs.append((name, len(cols), dtype, parts))
        idx.extend(cols)

    add("a_q", list(range(o["a_q"], o["a_q"] + 512)), BF16)
    add("a_z", list(range(o["a_z"], o["a_z"] + 512)), F32)
    add("a_kv_cmp", list(range(o["a_kv_cmp"], o["a_kv_cmp"] + 256)), BF16, parts=4)
    add("a_kv_slc", _kv_interleave(o["a_kv_slc"]), BF16)
    add("a_kv_win", _kv_interleave(o["a_kv_win"]), BF16)
    gate = []
    for g in range(NSA_GROUPS):
        cols = [N_IN] * 128
        for br in range(3):
            for h in range(NSA_HG):
                cols[br * NSA_HG + h] = o["a_gate"] + br * NSA_HEADS + g * NSA_HG + h
        gate += cols
    add("a_gate", gate, F32)
    add("b_q", list(range(o["b_q"], o["b_q"] + 256)), BF16)
    add("b_kv", _kv_interleave(o["b_kv"]), BF16)
    add("b_z", list(range(o["b_z"], o["b_z"] + 256)), F32)
    add("c_shift", list(range(o["c_shift"], o["c_shift"] + SHIFT_WIDTH)), F32)
    add("c_z", list(range(o["c_z"], o["c_z"] + 256)), F32)
    return tuple(segs), np.asarray(idx, np.int32)


_SEGS, _COL_IDX = _projection_layout()
_N_PROJ = int(_COL_IDX.shape[0])


def _column_runs(idx):
    runs, i = [], 0
    while i < len(idx):
        j = i + 1
        while j < len(idx) and (idx[j] == idx[j - 1] + 1 if idx[i] != N_IN else idx[j] == N_IN):
            j += 1
        runs.append((int(idx[i]), j - i))
        i = j
    return runs


_COL_RUNS = _column_runs(_COL_IDX)


def _params(sem):
    return pltpu.CompilerParams(dimension_semantics=sem, vmem_limit_bytes=VMEM_LIMIT)


def _in_proj_kernel(x_ref, g_ref, w_ref, *refs):
    out_refs, chunk_sc = refs[:-1], refs[-1]
    x = x_ref[...]
    ms = jnp.mean(x * x, axis=-1, keepdims=True)
    xn = (x * lax.rsqrt(ms + NORM_EPS) * g_ref[...]).astype(BF16)
    off = 0
    refs = iter(out_refs)
    for _, width, _, parts in _SEGS:
        outs = [next(refs) for _ in range(parts)]
        for c0 in range(0, width, 512):
            cw = min(512, width - c0)
            val = jnp.dot(xn, w_ref[:, off + c0:off + c0 + cw], preferred_element_type=F32)
            if parts == 1:
                outs[0][:, c0:c0 + cw] = val.astype(outs[0].dtype)
            else:
                pw = width // parts
                per_slab = 128 // pw
                for h in range(width // 128):
                    chunk_sc[h] = val[:, h * 128:(h + 1) * 128]
                n_chunks = val.shape[0] // CMP_STRIDE
                for l in range(CMP_STRIDE):
                    for h in range(width // 128):
                        rows = chunk_sc[h, pl.ds(l, n_chunks, stride=CMP_STRIDE), :]
                        for jj in range(per_slab):
                            o_ref = outs[h * per_slab + jj]
                            o_ref[:, l * pw:(l + 1) * pw] = rows[:, jj * pw:(jj + 1) * pw].astype(o_ref.dtype)
        off += width


def _in_proj(x2d, g, w):
    m = x2d.shape[0]
    tm = ROW_TILE
    out_shape, out_specs, chunk_width = [], [], 0
    for _, width, dt, parts in _SEGS:
        if parts == 1:
            out_shape.append(jax.ShapeDtypeStruct((m, width), dt))
            out_specs.append(pl.BlockSpec((tm, width), lambda i: (i, 0)))
        else:
            assert width <= 512 and chunk_width in (0, width)
            chunk_width = width
            cols = width // parts * CMP_STRIDE
            out_shape += [jax.ShapeDtypeStruct((m // CMP_STRIDE, cols), dt)] * parts
            out_specs += [pl.BlockSpec((tm // CMP_STRIDE, cols), lambda i: (i, 0))] * parts
    return pl.pallas_call(
        _in_proj_kernel,
        grid=(m // tm,),
        in_specs=[
            pl.BlockSpec((tm, D_MODEL), lambda i: (i, 0)),
            pl.BlockSpec((1, D_MODEL), lambda i: (0, 0)),
            pl.BlockSpec((D_MODEL, _N_PROJ), lambda i: (0, 0), pipeline_mode=pl.Buffered(1)),
        ],
        out_specs=out_specs,
        out_shape=out_shape,
        scratch_shapes=[pltpu.VMEM((chunk_width // 128, tm, 128), F32)],
        compiler_params=_params(("parallel",)),
    )(x2d, g, w)


def _compress_kernel(*refs):
    z_refs, (pe_ref, w1_ref, w2_ref, o_ref) = refs[:-4], refs[-4:]
    half = CMP_STRIDE * HEAD_DIM
    for j, z_ref in enumerate(z_refs):
        kv = j // NSA_GROUPS
        z = z_ref[...]
        w1 = w1_ref[kv]
        first = _dot_exact_lhs(z, w1[:half], 3)
        second = _dot_exact_lhs(z, w1[half:], 3)
        n = z.shape[0]
        hid = first + pltpu.roll(second, n - 1, 0)
        hid = hid + _dot3(_split(pe_ref[kv]), _split(w1))
        act = hid * jax.nn.sigmoid(hid)
        o_ref[j] = _dot3(_split(act), _split(w2_ref[kv]))


def _compress(kv_cmp, pe, w1, w2):
    zs = list(kv_cmp)
    b, nch, _ = zs[0].shape

    def full(arr):
        return pl.BlockSpec(arr.shape, lambda i: (0,) * arr.ndim)

    return pl.pallas_call(
        _compress_kernel,
        grid=(b,),
        in_specs=[pl.BlockSpec((None, nch, CMP_STRIDE * HEAD_DIM), lambda i: (i, 0, 0)) for _ in zs]
        + [full(pe), full(w1), full(w2)],
        out_specs=pl.BlockSpec((None, len(zs), nch, HEAD_DIM), lambda i: (i, 0, 0, 0)),
        out_shape=jax.ShapeDtypeStruct((b, len(zs), nch, HEAD_DIM), F32),
        compiler_params=_params(("parallel",)),
    )(*zs, pe, w1, w2)


def _widen_queries(q, n_heads):
    zeros = jnp.zeros((q.shape[0], HEAD_DIM), q.dtype)
    return [jnp.concatenate([zeros, q[:, h * HEAD_DIM:(h + 1) * HEAD_DIM]], axis=1) for h in range(n_heads)]


def _pair_values(kv):
    swapped = jnp.concatenate([kv[:, HEAD_DIM:], kv[:, :HEAD_DIM]], axis=1)
    lane = lax.broadcasted_iota(jnp.int32, kv.shape, 1)
    low = jnp.where(lane < HEAD_DIM, 1.0, 0.0).astype(kv.dtype)
    high = jnp.where(lane < HEAD_DIM, 0.0, 1.0).astype(kv.dtype)
    even = jnp.concatenate([kv * low, low], axis=1)
    odd = jnp.concatenate([swapped * high, high], axis=1)
    return even, odd


def _pair_lanes(even, odd):
    low = lax.broadcasted_iota(jnp.int32, even.shape, 1) < HEAD_DIM
    return jnp.where(low, even, odd)


def _nt(a, b):
    return lax.dot_general(a, b, (((1,), (1,)), ((), ())), preferred_element_type=F32)


def _rounds(streams, stagger=0):
    results = [None] * len(streams)
    waiting = list(enumerate(streams))
    live = {}
    rounds = 0
    while waiting or live:
        while waiting and waiting[0][0] * stagger <= rounds:
            idx, stream = waiting.pop(0)
            live[idx] = stream
        for idx in list(live):
            try:
                next(live[idx])
            except StopIteration as stop:
                results[idx] = stop.value
                del live[idx]
        rounds += 1
        yield
    return results


def _interleave(*streams, stagger=0):
    rounds = _rounds(streams, stagger)
    while True:
        try:
            next(rounds)
        except StopIteration as stop:
            return stop.value


def _band_block(qa_even, qa_odd, keys, val_even, val_odd, bias, sinks=None):
    s = [_nt(qa, keys) + bias for qa in (qa_even, qa_odd)]
    yield
    m = [jnp.max(x, axis=-1, keepdims=True) for x in s]
    if sinks is not None:
        m = [jnp.maximum(x, sinks[e]) for e, x in enumerate(m)]
    p = [jnp.exp(x - mx).astype(BF16) for x, mx in zip(s, m)]
    yield
    acc = (jnp.dot(p[0], val_even, preferred_element_type=F32)
           + jnp.dot(p[1], val_odd, preferred_element_type=F32))
    denom = acc[:, 2 * HEAD_DIM:]
    if sinks is not None:
        shape = (acc.shape[0], 2 * HEAD_DIM)
        denom = denom + _pair_lanes(jnp.broadcast_to(jnp.exp(sinks[0] - m[0]), shape),
                                    jnp.broadcast_to(jnp.exp(sinks[1] - m[1]), shape))
    return acc[:, :2 * HEAD_DIM] / denom


def _band_blocks(blocks, sinks=None):
    return (yield from _rounds([_band_block(*blk, sinks) for blk in blocks], stagger=1))


def _band_bias_table(deltas, rows, width, window):
    r = np.arange(rows)[:, None]
    c = np.arange(width)[None, :]
    out = [np.where((c <= r + d) & (c > r + d - window), 0.0, NEG_INF) for d in deltas]
    return np.stack(out).astype(np.float32)


def _attention_kernel(sink_ref, q_ref, z_ref, gate_ref, kvs_ref, kvw_ref, kcvc_ref, bq_ref, bz_ref,
                      bkv_ref, negexp_ref, ovl_ref, wbias_ref, cbias_ref, gexp_ref, bbias_ref,
                      o_ref, ob_ref, srhs_sc, spe_sc, spo_sc, wpe_sc, wpo_sc, bpe_sc, bpo_sc, m_sc, acc_sc):
    i = pl.program_id(1)
    tq = ATT_TILE
    rb = ATT_ROWS
    n_rb = tq // rb
    n_cmp_pad = kcvc_ref.shape[1]
    n_slc = ovl_ref.shape[0]
    n_pairs = NSA_HG // 2
    groups = range(NSA_GROUPS)
    wd = NSA_HG * HEAD_DIM
    wb = SWA_HG * HEAD_DIM
    kv_lanes = [slice(g * 2 * HEAD_DIM, (g + 1) * 2 * HEAD_DIM) for g in groups]

    @pl.when(i == 0)
    def _():
        for g in groups:
            kvs = kvs_ref[:, kv_lanes[g]]
            srhs_sc[g] = jnp.concatenate([kvs, negexp_ref[...]], axis=1)
            spe_sc[g], spo_sc[g] = _pair_values(kvs)
            wpe_sc[g], wpo_sc[g] = _pair_values(kvw_ref[:, kv_lanes[g]])
            bpe_sc[g], bpo_sc[g] = _pair_values(bkv_ref[:, kv_lanes[g]])

    scale = jnp.asarray(HEAD_DIM ** -0.5, BF16)
    qas = [_widen_queries(q_ref[:, g * wd:(g + 1) * wd] * scale, NSA_HG) for g in groups]

    def band_inputs(g, kv_ref, pe_sc, po_sc, bias_ref, window, r0):
        q0 = i * tq + r0
        k0 = pl.multiple_of(jnp.maximum(q0 - window, 0), rb)
        width = window + rb
        variant = jnp.minimum(q0 // rb, bias_ref.shape[0] - 1)
        return (kv_ref[pl.ds(k0, width), kv_lanes[g]], pe_sc[g, pl.ds(k0, width), :],
                po_sc[g, pl.ds(k0, width), :], bias_ref[variant])

    def window_stream(g):
        bands = [band_inputs(g, kvw_ref, wpe_sc, wpo_sc, wbias_ref, NSA_WINDOW, r0) for r0 in range(0, tq, rb)]
        blocks = [(qas[g][2 * pair][r0:r0 + rb], qas[g][2 * pair + 1][r0:r0 + rb]) + bands[r0 // rb]
                  for pair in range(n_pairs) for r0 in range(0, tq, rb)]
        outs = yield from _band_blocks(blocks)
        return jnp.concatenate([jnp.concatenate(outs[pair * n_rb:(pair + 1) * n_rb], axis=0)
                                for pair in range(n_pairs)], axis=1)

    def sink_window_stream(g):
        lanes = slice(g * wb, (g + 1) * wb)
        qbs = _widen_queries(bq_ref[:, lanes] * scale, SWA_HG)
        sinks = (sink_ref[g * SWA_HG], sink_ref[g * SWA_HG + 1])
        blocks = [(qbs[0][r0:r0 + rb], qbs[1][r0:r0 + rb])
                  + band_inputs(g, bkv_ref, bpe_sc, bpo_sc, bbias_ref, SWA_WINDOW, r0) for r0 in range(0, tq, rb)]
        outs = yield from _band_blocks(blocks, sinks)
        y = jnp.concatenate(outs, axis=0)
        zb = bz_ref[:, lanes]
        ob_ref[:, lanes] = (y * (zb * jax.nn.sigmoid(zb))).astype(ob_ref.dtype)

    def compressed_stream(g):
        zeros_k = jnp.zeros((n_cmp_pad, HEAD_DIM), BF16)
        kc_both = jnp.concatenate([jnp.concatenate([zeros_k, t], axis=1) for t in _split(kcvc_ref[g])], axis=0)
        vc = kcvc_ref[NSA_GROUPS + g].astype(BF16)
        vc_even = jnp.concatenate([vc, zeros_k], axis=1)
        vc_odd = jnp.concatenate([zeros_k, vc], axis=1)
        tpos = i * tq + lax.broadcasted_iota(jnp.int32, (tq, n_cmp_pad), 0)
        cmp_end = lax.broadcasted_iota(jnp.int32, (tq, n_cmp_pad), 1) * CMP_STRIDE + (CMP_BLOCK - 1)
        valid_c = cmp_end <= tpos
        s = [_nt(qa, kc_both) for qa in qas[g]]
        s = [x[:, :n_cmp_pad] + x[:, n_cmp_pad:] for x in s]
        yield
        s = [jnp.where(valid_c, x, NEG_INF) for x in s]
        m = [jnp.max(x, axis=-1, keepdims=True) for x in s]
        yield
        p = [jnp.where(valid_c, jnp.exp(x - mx), 0.0) for x, mx in zip(s, m)]
        denom = [jnp.sum(x, axis=-1, keepdims=True) for x in p]
        yield
        p = [x * (1.0 / jnp.where(d > 0, d, 1.0)) for x, d in zip(p, denom)]
        p_sum = (p[0] + p[1]) + (p[2] + p[3])
        pb = [x.astype(BF16) for x in p]
        o_cmp = jnp.concatenate(
            [jnp.dot(pb[2 * pair], vc_even, preferred_element_type=F32)
             + jnp.dot(pb[2 * pair + 1], vc_odd, preferred_element_type=F32) for pair in range(n_pairs)], axis=1)
        yield
        imp = None
        rest = p_sum
        for _ in range(3):
            term = rest.astype(BF16)
            rest = rest - term.astype(F32)
            part = _nt(ovl_ref[...], term)
            imp = part if imp is None else imp + part
        yield
        blk = lax.broadcasted_iota(jnp.int32, (n_slc, tq), 0)
        cur = (i * tq + lax.broadcasted_iota(jnp.int32, (n_slc, tq), 1)) // SLC_BLOCK
        forced = (blk == 0) | (blk == cur) | (blk == cur - 1)
        score = jnp.where(forced, FORCE, jnp.where(blk <= cur, imp, -FORCE))
        ranks = [jnp.zeros((n_slc, tq), F32) for _ in range(4)]
        for r in range(n_slc):
            row = score[r:r + 1, :]
            ahead = (row > score) | ((row == score) & (r < blk))
            ranks[r % 4] = ranks[r % 4] + jnp.where(ahead, 1.0, 0.0)
            if r % 4 == 3:
                yield
        rank = (ranks[0] + ranks[1]) + (ranks[2] + ranks[3])
        n_sel = min(SLC_TOPN, n_slc)
        unsel_t = jnp.where((rank < n_sel) & (blk <= cur), 0.0, 1.0)
        unsel_t = jnp.concatenate([unsel_t, jnp.zeros((128 - n_slc, tq), F32)], axis=0)
        return o_cmp, jnp.transpose(unsel_t).astype(BF16)

    results = _interleave(*[stream(g) for g in groups
                            for stream in (window_stream, sink_window_stream, compressed_stream)])
    o_win = [results[3 * g] for g in groups]
    o_cmp = [results[3 * g + 2][0] for g in groups]
    unsel = [results[3 * g + 2][1] for g in groups]

    lhs = {(g, h): jnp.concatenate([qas[g][h], unsel[g]], axis=1) for g in groups for h in range(NSA_HG)}
    m_sc[...] = jnp.full(m_sc.shape, NEG_INF, F32)
    acc_sc[...] = jnp.zeros(acc_sc.shape, F32)
    subs = [(g, h, r0) for g in groups for h in range(NSA_HG) for r0 in range(0, tq, rb)]

    def slc_keys(start, width, bias_lanes):
        rhs = [srhs_sc[g, pl.ds(start, width), :] for g in groups]
        val_even = [spe_sc[g, pl.ds(start, width), :] for g in groups]
        val_odd = [spo_sc[g, pl.ds(start, width), :] for g in groups]
        s = {}
        for g, h, r0 in subs:
            s[g, h, r0] = _nt(lhs[g, h][r0:r0 + rb], rhs[g])
            if bias_lanes is not None:
                s[g, h, r0] = s[g, h, r0] + cbias_ref[r0:r0 + rb, bias_lanes:bias_lanes + width]
        m_prev = {(g, h, r0): m_sc[g * NSA_HG + h, r0:r0 + rb] for g, h, r0 in subs}
        m_next = {key: jnp.maximum(m_prev[key], jnp.max(s[key], axis=-1, keepdims=True)) for key in subs}
        p = {key: jnp.exp(s[key] - jnp.concatenate([m_next[key]] * (width // 128), axis=1)).astype(BF16)
             for key in subs}
        alpha = {key: jnp.exp(m_prev[key] - m_next[key]) for key in subs}
        for g, h, r0 in subs:
            m_sc[g * NSA_HG + h, r0:r0 + rb] = m_next[g, h, r0]
        for g in groups:
            for pair in range(n_pairs):
                for r0 in range(0, tq, rb):
                    rows = slice(r0, r0 + rb)
                    decay = _pair_lanes(alpha[g, 2 * pair, r0], alpha[g, 2 * pair + 1, r0])
                    decay = jnp.concatenate([decay, decay], axis=1)
                    acc_sc[g * n_pairs + pair, rows] = (
                        decay * acc_sc[g * n_pairs + pair, rows]
                        + jnp.dot(p[g, 2 * pair, r0], val_even[g], preferred_element_type=F32)
                        + jnp.dot(p[g, 2 * pair + 1, r0], val_odd[g], preferred_element_type=F32))

    def wide(j, carry):
        slc_keys(pl.multiple_of(j * (2 * tq), 2 * tq), 2 * tq, None)
        return carry

    lax.fori_loop(0, i // 2, wide, 0)

    @pl.when(i % 2 == 0)
    def _():
        slc_keys(pl.multiple_of(i * tq, tq), tq, tq)

    @pl.when(i % 2 == 1)
    def _():
        slc_keys(pl.multiple_of((i - 1) * tq, 2 * tq), 2 * tq, 0)

    for g in groups:
        o_slc = []
        for pair in range(n_pairs):
            acc = acc_sc[g * n_pairs + pair]
            o_slc.append(acc[:, :2 * HEAD_DIM] / acc[:, 2 * HEAD_DIM:])
        o_slc = jnp.concatenate(o_slc, axis=1)
        sg = _split(jax.nn.sigmoid(gate_ref[:, g * 128:(g + 1) * 128]))
        gates = (jnp.dot(sg[0], gexp_ref[...], preferred_element_type=F32)
                 + jnp.dot(sg[1], gexp_ref[...], preferred_element_type=F32))
        y = gates[:, :wd] * o_cmp[g] + gates[:, wd:2 * wd] * o_slc + gates[:, 2 * wd:] * o_win[g]
        z = z_ref[:, g * wd:(g + 1) * wd]
        o_ref[:, g * wd:(g + 1) * wd] = (y * (z * jax.nn.sigmoid(z))).astype(o_ref.dtype)


def _attention(q, z, gate, kv_slc, kv_win, kcvc, bq, bz, bkv, sinks):
    b, t, _ = q.shape
    tq = ATT_TILE
    assert t % (2 * tq) == 0 and t >= NSA_WINDOW + tq and NSA_WINDOW % tq == 0 and SWA_HG == 2
    assert NSA_GROUPS == SWA_GROUPS
    n_slc = t // SLC_BLOCK
    nch = t // CMP_STRIDE
    wd = NSA_HG * HEAD_DIM
    negexp = np.where(np.arange(t)[:, None] // SLC_BLOCK == np.arange(128)[None, :], -2.0 ** 100, 0.0)
    ci = np.arange(nch)[None, :] * CMP_STRIDE
    sj = np.arange(n_slc)[:, None] * SLC_BLOCK
    overlap_t = ((ci < sj + SLC_BLOCK) & (ci + CMP_BLOCK > sj)).astype(np.float32)
    rb = ATT_ROWS
    assert NSA_WINDOW % rb == 0 and SWA_WINDOW % rb == 0 and tq % rb == 0
    wbias = _band_bias_table(range(0, NSA_WINDOW + 1, rb), rb, NSA_WINDOW + rb, NSA_WINDOW)
    cbias = np.concatenate([np.zeros((tq, tq), np.float32), _band_bias_table([0], tq, tq, t)[0]], axis=1)
    bbias = _band_bias_table(range(0, SWA_WINDOW + 1, rb), rb, SWA_WINDOW + rb, SWA_WINDOW)
    gexp = np.zeros((128, 3 * wd), np.float32)
    for br in range(3):
        for h in range(NSA_HG):
            gexp[br * NSA_HG + h, br * wd + h * HEAD_DIM:br * wd + (h + 1) * HEAD_DIM] = 1.0

    def const(arr):
        return pl.BlockSpec(arr.shape, lambda bi, i: (0,) * arr.ndim)

    def tile(arr):
        return pl.BlockSpec((None, tq, arr.shape[-1]), lambda bi, i: (bi, i, 0))

    def whole(arr):
        return pl.BlockSpec((None,) + arr.shape[1:], lambda bi, i: (bi,) + (0,) * (arr.ndim - 1))

    consts = [jnp.asarray(negexp, BF16), jnp.asarray(overlap_t, BF16), jnp.asarray(wbias),
              jnp.asarray(cbias), jnp.asarray(gexp, BF16), jnp.asarray(bbias)]
    key_scratch = pltpu.VMEM((NSA_GROUPS, t, 256), BF16)
    return pl.pallas_call(
        _attention_kernel,
        grid=(b, t // tq),
        in_specs=[pl.BlockSpec(memory_space=pltpu.SMEM), tile(q), tile(z), tile(gate), whole(kv_slc),
                  whole(kv_win), whole(kcvc), tile(bq), tile(bz), whole(bkv)] + [const(a) for a in consts],
        out_specs=(pl.BlockSpec((None, tq, 512), lambda bi, i: (bi, i, 0)),
                   pl.BlockSpec((None, tq, 256), lambda bi, i: (bi, i, 0))),
        out_shape=(jax.ShapeDtypeStruct((b, t, 512), BF16), jax.ShapeDtypeStruct((b, t, 256), BF16)),
        scratch_shapes=[key_scratch] * 7 + [
            pltpu.VMEM((NSA_GROUPS * NSA_HG, tq, 128), F32),
            pltpu.VMEM((NSA_GROUPS * NSA_HG // 2, tq, 256), F32),
        ],
        compiler_params=_params(("parallel", "arbitrary")),
    )(sinks, q, z, gate, kv_slc, kv_win, kcvc, bq, bz, bkv, *consts)


_NN = (((1,), (0,)), ((), ()))
_NT = (((1,), (1,)), ((), ()))


def _split(a):
    hi = a.astype(BF16)
    lo = (a - hi.astype(F32)).astype(BF16)
    return hi, lo


def _dot(a, b, dims=_NN):
    return lax.dot_general(a, b, dims, preferred_element_type=F32)


def _dot3(a, b, dims=_NN):
    rows = a[0].shape[0]
    both = _dot(jnp.concatenate([a[0], a[1]], axis=0), b[0], dims)
    return (both[:rows] + both[rows:]) + _dot(a[0], b[1], dims)


def _dot_exact_lhs(a, b, n_terms):
    out = None
    for _ in range(n_terms):
        term = b.astype(BF16)
        b = b - term.astype(F32)
        part = _dot(a, term)
        out = part if out is None else out + part
    return out


def _dot_exact_rhs(x, ones):
    hi, lo = _split(x)
    return _dot(hi, ones) + _dot(lo, ones)


def _block_diag(pair, head_masks):
    return tuple(jnp.concatenate([x * m for m in head_masks], axis=0) for x in pair)


def _each(fn, *lists):
    return [fn(*items) for items in zip(*lists)]


def _unit_lower_inverse(lows, eye, same_sub, bd):
    diag = _each(lambda low: jnp.where(same_sub, low, 0.0), lows)
    off = _each(lambda low, d: low - d, lows, diag)
    inv = _each(lambda d: eye + d, diag)
    power = diag
    span = 2
    while span < RWKV_SUB:
        ps = _each(_split, power)
        power = _each(lambda p: _dot3(p, bd(p)), ps)
        yield
        inv = _each(lambda iv, pw: _dot3(_split(iv), bd(_split(eye + pw))), inv, power)
        yield
        span *= 2
    inv_s = _each(_split, inv)
    cross = _each(lambda iv, o: _dot3(iv, bd(_split(o))), inv_s, off)
    yield
    total = _each(lambda cr: eye + cr, cross)
    power = cross
    span = 2
    while span < RWKV_CHUNK // RWKV_SUB:
        ps = _each(_split, power)
        power = _each(lambda p: _dot3(p, bd(p)), ps)
        yield
        total = _each(lambda tt, pw: _dot3(_split(tt), bd(_split(eye + pw))), total, power)
        yield
        span *= 2
    return _each(lambda tt, iv: _dot3(_split(tt), bd(iv)), total, inv_s)


def _rwkv_chunk_maps(a_t, r_t, b_t, k_t, b_e, k_e, v, p_end, cst):
    C = RWKV_CHUNK
    eye, strict, incl, same_sub, head_masks = cst
    bd = lambda pair: _block_diag(pair, head_masks)

    ar = _each(lambda a, r: _split(jnp.concatenate([a, r], axis=0)), a_t, r_t)
    g_b = _each(lambda x, b: _dot3(x, bd(_split(b)), _NT), ar, b_t)
    g_k = _each(lambda x, k: _dot3(x, bd(_split(k)), _NT), ar, k_t)
    yield
    a_ab = _each(lambda g: jnp.where(strict, g[:C], 0.0), g_b)
    a_rb = _each(lambda g: _split(jnp.where(incl, g[C:], 0.0)), g_b)
    a_ak = _each(lambda g: _split(jnp.where(strict, g[:C], 0.0)), g_k)
    a_rk = _each(lambda g: _split(jnp.where(incl, g[C:], 0.0)), g_k)
    v_bd = _each(lambda x: bd(_split(x)), v)
    akv = _each(_dot3, a_ak, v_bd)
    yield
    t_inv = yield from _unit_lower_inverse(a_ab, eye, same_sub, bd)
    t_inv = _each(_split, t_inv)
    yield
    w_m = _each(lambda t, a: _dot3(t, bd(_split(a))), t_inv, a_t)
    u0 = _each(lambda t, x: _dot3(t, bd(_split(x))), t_inv, akv)
    yield
    r_m = _each(lambda r, g, w: r + _dot3(g, bd(_split(w))), r_t, a_rb, w_m)
    y0 = _each(lambda g, u, gk, vb: _dot3(g, bd(_split(u))) + _dot3(gk, vb), a_rb, u0, a_rk, v_bd)
    yield
    bk = _each(lambda b, k: _split(jnp.transpose(jnp.concatenate([b, k], axis=0))), b_e, k_e)
    m_add = _each(lambda x, w: _dot3((x[0][:, :C], x[1][:, :C]), _split(w)), bk, w_m)
    n_all = _each(lambda x, u, vv: _dot3(x, _split(jnp.concatenate([u, vv], axis=0))), bk, u0, v)
    yield
    keep = [m.astype(F32) for m in head_masks]

    def own_blocks(full):
        parts = [full[h * HEAD_DIM:(h + 1) * HEAD_DIM, :] * keep[h] for h in range(RWKV_HEADS)]
        return (parts[0] + parts[1]) + (parts[2] + parts[3])

    m_l = _each(lambda m, p: own_blocks(m) + eye * p, m_add, p_end)
    n_l = _each(own_blocks, n_all)
    lhs = _each(lambda r, m: _split(jnp.concatenate([r, m], axis=0)), r_m, m_l)
    return list(zip(lhs, y0, n_l))


def _rwkv_kernel(*refs, has_vres):
    if has_vres:
        (feat_ref, prev_ref, z_ref, vfirst_ref, mu_ref, w0_ref, w2_ref, a0_ref, a2_ref, kk_ref, ka_ref,
         rk_ref, lnw_ref, lnb_ref, hsum_ref, v0_ref, v1_ref, v2_ref, y_ref, state) = refs
    else:
        (feat_ref, prev_ref, z_ref, mu_ref, w0_ref, w2_ref, a0_ref, a2_ref, kk_ref, ka_ref,
         rk_ref, lnw_ref, lnb_ref, hsum_ref, y_ref, vout_ref, state) = refs
    c = pl.program_id(1)
    C = RWKV_CHUNK
    W = RWKV_WIDTH
    n_tok = feat_ref.shape[0]
    group = RWKV_GROUP_CHUNKS * C

    @pl.when(c == 0)
    def _():
        state[...] = jnp.zeros(state.shape, F32)

    def lora(x, w_ref):
        return _dot3(_split(x), _split(w_ref[...]))

    hsum = hsum_ref[...]

    def head_sum(x):
        return _dot_exact_rhs(x, hsum)

    feat = feat_ref[...]
    row = lax.broadcasted_iota(jnp.int32, feat.shape, 0)
    last_prev = jnp.where(c == 0, 0.0, prev_ref[7:8, :])
    prev = jnp.where(row == 0, last_prev, pltpu.roll(feat, 1, 0))
    xs = feat + (prev - feat) * mu_ref[...]

    ti = lax.broadcasted_iota(jnp.int32, (group, group), 0)
    tj = lax.broadcasted_iota(jnp.int32, (group, group), 1)
    tri = jnp.where((ti >= tj) & (ti // C == tj // C), 1.0, 0.0).astype(BF16)
    t_idx = lax.broadcasted_iota(jnp.int32, (C, W), 0)
    i_idx = lax.broadcasted_iota(jnp.int32, (C, W), 1) % HEAD_DIM
    lane_head = lax.broadcasted_iota(jnp.int32, (C, W), 1) // HEAD_DIM
    cst = (
        jnp.where(t_idx == i_idx, 1.0, 0.0),
        t_idx > i_idx,
        t_idx >= i_idx,
        (t_idx // RWKV_SUB) == (i_idx // RWKV_SUB),
        [jnp.where(lane_head == h, 1.0, 0.0).astype(BF16) for h in range(RWKV_HEADS)],
    )
    carry = {"state": state[...], "turn": 0}

    def token_group(idx):
        rows = slice(idx * group, (idx + 1) * group)
        x = xs[rows]
        r = x[:, :W]
        k = x[:, W:2 * W]
        v = x[:, 2 * W:3 * W]
        wd = x[:, 3 * W:3 * W + DECAY_LORA]
        ad = x[:, 3 * W + DECAY_LORA:]
        pre = -(w0_ref[...] + lora(jnp.tanh(wd), w2_ref))
        softplus = jnp.maximum(pre, 0.0) + jnp.log(1.0 + jnp.exp(-jnp.abs(pre)))
        logw = -jnp.exp(-softplus - 0.5)
        yield
        if has_vres:
            mix = jax.nn.sigmoid(v0_ref[...] + lora(lora(v, v1_ref), v2_ref))
            v = v + (vfirst_ref[rows, :] - v) * mix
        else:
            vout_ref[rows, :] = v
        alpha = jax.nn.sigmoid(a0_ref[...] + lora(ad, a2_ref))
        kk = k * kk_ref[...]
        kk = kk / jnp.maximum(jnp.sqrt(head_sum(kk * kk)), 1e-12)
        k = k * (1.0 + (alpha - 1.0) * ka_ref[...])
        yield
        cum = _dot_exact_lhs(tri, logw, 3)
        yield
        a_t = -kk * jnp.exp(cum - logw)
        r_t = r * jnp.exp(cum)
        inv_decay = jnp.exp(-cum)
        b_raw = kk * alpha
        b_t = b_raw * inv_decay
        k_t = k * inv_decay
        chunks = [slice(j * C, (j + 1) * C) for j in range(RWKV_GROUP_CHUNKS)]
        cum_end = [cum[cr.stop - 1:cr.stop, :] for cr in chunks]
        to_end = [jnp.exp(ce - cum[cr]) for ce, cr in zip(cum_end, chunks)]
        yield
        maps = yield from _rwkv_chunk_maps(
            [a_t[cr] for cr in chunks], [r_t[cr] for cr in chunks], [b_t[cr] for cr in chunks],
            [k_t[cr] for cr in chunks], [b_raw[cr] * te for cr, te in zip(chunks, to_end)],
            [k[cr] * te for cr, te in zip(chunks, to_end)], [v[cr] for cr in chunks],
            [jnp.exp(ce) for ce in cum_end], cst)
        while carry["turn"] != idx:
            yield
        s_l = carry["state"]
        ys = []
        for lhs, y0, n_l in maps:
            out = _dot3(lhs, _block_diag(_split(s_l), cst[4]))
            ys.append(out[:C] + y0)
            s_l = out[C:] + n_l
        carry["state"] = s_l
        carry["turn"] = idx + 1
        yield
        y = jnp.concatenate(ys, axis=0)
        inv_n = 1.0 / HEAD_DIM
        mean = head_sum(y) * inv_n
        cen = y - mean
        yield
        var = head_sum(cen * cen) * inv_n
        y = cen * lax.rsqrt(var + GN_EPS) * lnw_ref[...] + lnb_ref[...]
        y = y + head_sum(r * k * rk_ref[...]) * v
        z = z_ref[rows, :]
        y_ref[rows, :] = (y * (z * jax.nn.sigmoid(z))).astype(y_ref.dtype)

    _interleave(*[token_group(idx) for idx in range(n_tok // group)], stagger=RWKV_STAGGER)
    state[...] = carry["state"]


def _rwkv(feat, z, v_first, p):
    b, t, _ = feat.shape
    n_tok = RWKV_STEP_CHUNKS * RWKV_CHUNK
    assert t % n_tok == 0 and RWKV_CHUNK == HEAD_DIM
    W = RWKV_WIDTH
    has_vres = v_first is not None
    hsum = (np.arange(W)[:, None] // HEAD_DIM == np.arange(W)[None, :] // HEAD_DIM).astype(np.float32)

    def tok(width):
        return pl.BlockSpec((None, n_tok, width), lambda bi, c: (bi, c, 0))

    def full(arr):
        return pl.BlockSpec(arr.shape, lambda bi, c: (0,) * arr.ndim)

    args = [feat, feat, z]
    specs = [tok(SHIFT_WIDTH),
             pl.BlockSpec((None, 8, SHIFT_WIDTH), lambda bi, c: (bi, jnp.maximum(c * (n_tok // 8) - 1, 0), 0)),
             tok(W)]
    if has_vres:
        args.append(v_first)
        specs.append(tok(W))
    names = ["mu", "w0", "w2", "a0", "a2", "k_k", "k_a", "r_k", "ln_w", "ln_b"]
    consts = [p[n] for n in names] + [jnp.asarray(hsum, BF16)]
    if has_vres:
        consts += [p["v0"], p["v1"], p["v2"]]
    args += consts
    specs += [full(a) for a in consts]
    y_shape = jax.ShapeDtypeStruct((b, t, W), BF16)
    if has_vres:
        out_shape, out_specs = y_shape, tok(W)
    else:
        out_shape = (y_shape, jax.ShapeDtypeStruct((b, t, W), F32))
        out_specs = (tok(W), tok(W))
    res = pl.pallas_call(
        functools.partial(_rwkv_kernel, has_vres=has_vres),
        grid=(b, t // n_tok),
        in_specs=specs,
        out_specs=out_specs,
        out_shape=out_shape,
        scratch_shapes=[pltpu.VMEM((HEAD_DIM, W), F32)],
        compiler_params=_params(("parallel", "arbitrary")),
    )(*args)
    if has_vres:
        return res, v_first
    return res[0], res[1]


def _merge_kernel(x_ref, ya_ref, yb_ref, yc_ref, g_ref, wm_ref, bm_ref, pa_ref, pb_ref, pc_ref, wo_ref, fg_ref,
                  o_ref, *, final_norm):
    d = D_MODEL
    x = x_ref[...]
    ms = jnp.mean(x * x, axis=-1, keepdims=True)
    xn = (x * lax.rsqrt(ms + NORM_EPS) * g_ref[...]).astype(BF16)
    mixed = None
    for j, (y_ref, p_ref) in enumerate(((ya_ref, pa_ref), (yb_ref, pb_ref), (yc_ref, pc_ref))):
        cols = slice(j * d, (j + 1) * d)
        logits = jnp.dot(xn, wm_ref[:, cols], preferred_element_type=F32) + bm_ref[:, cols]
        term = jax.nn.sigmoid(logits) * jnp.dot(y_ref[...], p_ref[...], preferred_element_type=F32)
        mixed = term if mixed is None else mixed + term
    x = x + jnp.dot(mixed.astype(BF16), wo_ref[...], preferred_element_type=F32)
    if final_norm:
        ms = jnp.mean(x * x, axis=-1, keepdims=True)
        x = x * lax.rsqrt(ms + NORM_EPS) * fg_ref[...]
    o_ref[...] = x


def _merge(x2d, ya, yb, yc, g, wm, bm, pa, pb, pc, wo, fg, final_norm):
    m = x2d.shape[0]
    tm = ROW_TILE

    def rows(arr):
        return pl.BlockSpec((tm, arr.shape[-1]), lambda i: (i, 0))

    def full(arr):
        return pl.BlockSpec(arr.shape, lambda i: (0, 0))

    consts = [g, wm, bm, pa, pb, pc, wo, fg]
    return pl.pallas_call(
        functools.partial(_merge_kernel, final_norm=final_norm),
        grid=(m // tm,),
        in_specs=[rows(a) for a in (x2d, ya, yb, yc)] + [full(a) for a in consts],
        out_specs=rows(x2d),
        out_shape=jax.ShapeDtypeStruct((m, D_MODEL), F32),
        compiler_params=_params(("parallel",)),
    )(x2d, ya, yb, yc, *consts)


def kernel(x, norm_g, w_in, b_merge, cmp_pe_k, cmp_w1_k, cmp_w2_k, cmp_pe_v, cmp_w1_v, cmp_w2_v, swa_sinks,
           rwkv_mu, rwkv_w0, rwkv_w2, rwkv_a0, rwkv_a2, rwkv_k_k, rwkv_k_a, rwkv_r_k, rwkv_ln_w, rwkv_ln_b,
           rwkv_v0, rwkv_v1, rwkv_v2, proj_a, proj_b, proj_c, w_out, final_g):
    b, t, d = x.shape
    depth = w_in.shape[0]
    m = b * t
    x2d = x.reshape(m, d)
    o_merge = _ref_offsets()["merge"]
    v_first = None
    for l in range(depth):
        w = jnp.concatenate([jnp.zeros((d, n), BF16) if start == N_IN else w_in[l][:, start:start + n].astype(BF16)
                             for start, n in _COL_RUNS], axis=1)
        g_l = norm_g[l].reshape(1, d)
        outs = iter(_in_proj(x2d, g_l, w))
        seg = {}
        for name, _, _, parts in _SEGS:
            arrs = [next(outs) for _ in range(parts)]
            arrs = [a.reshape(b, -1, a.shape[-1]) for a in arrs]
            seg[name] = arrs[0] if parts == 1 else arrs

        pe = jnp.stack([cmp_pe_k[l], cmp_pe_v[l]]).reshape(2, 1, CMP_BLOCK * HEAD_DIM)
        kcvc = _compress(seg["a_kv_cmp"], pe, jnp.stack([cmp_w1_k[l], cmp_w1_v[l]]),
                         jnp.stack([cmp_w2_k[l], cmp_w2_v[l]]))
        y_a, y_b = _attention(seg["a_q"], seg["a_z"], seg["a_gate"], seg["a_kv_slc"], seg["a_kv_win"], kcvc,
                              seg["b_q"], seg["b_z"], seg["b_kv"], swa_sinks[l])

        row = lambda a: a.reshape(1, -1)
        p = {"mu": row(rwkv_mu[l]), "w0": row(rwkv_w0[l]), "w2": rwkv_w2[l], "a0": row(rwkv_a0[l]),
             "a2": rwkv_a2[l], "k_k": row(rwkv_k_k[l]), "k_a": row(rwkv_k_a[l]), "r_k": row(rwkv_r_k[l]),
             "ln_w": row(rwkv_ln_w[l]), "ln_b": row(rwkv_ln_b[l])}
        if l > 0:
            p.update(v0=row(rwkv_v0[l - 1]), v1=rwkv_v1[l - 1], v2=rwkv_v2[l - 1])
        y_c, v_first = _rwkv(seg["c_shift"], seg["c_z"], v_first, p)

        w_merge = w_in[l][:, o_merge:o_merge + 3 * d].astype(BF16)
        x2d = _merge(x2d, y_a.reshape(m, -1), y_b.reshape(m, -1), y_c.reshape(m, -1), g_l, w_merge,
                     b_merge[l].reshape(1, -1), proj_a[l].astype(BF16), proj_b[l].astype(BF16),
                     proj_c[l].astype(BF16), w_out[l].astype(BF16), final_g.reshape(1, d),
                     final_norm=(l == depth - 1))
    return x2d.reshape(b, t, d)
```

```python
import functools

import numpy as np
import jax
import jax.numpy as jnp
from jax import lax
from jax.experimental import pallas as pl
from jax.experimental.pallas import tpu as pltpu

F32 = jnp.float32
BF16 = jnp.bfloat16

D_MODEL = 1024
HEAD_DIM = 64
NSA_HEADS = 8
NSA_GROUPS = 2
NSA_HG = NSA_HEADS // NSA_GROUPS
CMP_STRIDE = 16
CMP_BLOCK = 32
CMP_HIDDEN = 128
SLC_BLOCK = 64
SLC_TOPN = 16
NSA_WINDOW = 512
SWA_HEADS = 4
SWA_GROUPS = 2
SWA_HG = SWA_HEADS // SWA_GROUPS
SWA_WINDOW = 128
RWKV_HEADS = 4
RWKV_WIDTH = RWKV_HEADS * HEAD_DIM
DECAY_LORA = 64
ICLR_LORA = 64
VRES_LORA = 32
SHIFT_WIDTH = 3 * RWKV_WIDTH + DECAY_LORA + ICLR_LORA
NORM_EPS = 1e-6
GN_EPS = 64e-5
NEG_INF = -1e30
FORCE = 1e9

_REF_SEGMENTS = (
    ("a_q", 512), ("a_kv_cmp", 256), ("a_kv_slc", 256), ("a_kv_win", 256), ("a_gate", 24),
    ("a_z", 512), ("b_q", 256), ("b_kv", 256), ("b_z", 256), ("c_shift", SHIFT_WIDTH),
    ("c_z", 256), ("merge", 3 * D_MODEL),
)
N_IN = sum(w for _, w in _REF_SEGMENTS)

VMEM_LIMIT = 56 * 1024 * 1024
ROW_TILE = 1024
ATT_TILE = 256
ATT_ROWS = 128
RWKV_CHUNK = 64
RWKV_STEP_CHUNKS = 8
RWKV_GROUP_CHUNKS = 1
RWKV_STAGGER = 1
RWKV_SUB = 16


def _ref_offsets():
    out, off = {}, 0
    for name, width in _REF_SEGMENTS:
        out[name] = off
        off += width
    return out


def _kv_interleave(base):
    idx = []
    for g in range(2):
        idx += list(range(base + 128 + g * 64, base + 128 + (g + 1) * 64))
        idx += list(range(base + g * 64, base + (g + 1) * 64))
    return idx


def _projection_layout():
    o = _ref_offsets()
    segs, idx = [], []

    def add(name, cols, dtype, parts=1):
        segs.append((name, len(cols), dtype, parts))
        idx.extend(cols)

    add("a_q", list(range(o["a_q"], o["a_q"] + 512)), BF16)
    add("a_z", list(range(o["a_z"], o["a_z"] + 512)), F32)
    add("a_kv_cmp", list(range(o["a_kv_cmp"], o["a_kv_cmp"] + 256)), BF16, parts=4)
    add("a_kv_slc", _kv_interleave(o["a_kv_slc"]), BF16)
    add("a_kv_win", _kv_interleave(o["a_kv_win"]), BF16)
    gate = []
    for g in range(NSA_GROUPS):
        cols = [N_IN] * 128
        for br in range(3):
            for h in range(NSA_HG):
                cols[br * NSA_HG + h] = o["a_gate"] + br * NSA_HEADS + g * NSA_HG + h
        gate += cols
    add("a_gate", gate, F32)
    add("b_q", list(range(o["b_q"], o["b_q"] + 256)), BF16)
    add("b_kv", _kv_interleave(o["b_kv"]), BF16)
    add("b_z", list(range(o["b_z"], o["b_z"] + 256)), F32)
    add("c_shift", list(range(o["c_shift"], o["c_shift"] + SHIFT_WIDTH)), F32)
    add("c_z", list(range(o["c_z"], o["c_z"] + 256)), F32)
    return tuple(segs), np.asarray(idx, np.int32)


_SEGS, _COL_IDX = _projection_layout()
_N_PROJ = int(_COL_IDX.shape[0])


def _column_runs(idx):
    runs, i = [], 0
    while i < len(idx):
        j = i + 1
        while j < len(idx) and (idx[j] == idx[j - 1] + 1 if idx[i] != N_IN else idx[j] == N_IN):
            j += 1
        runs.append((int(idx[i]), j - i))
        i = j
    return runs


_COL_RUNS = _column_runs(_COL_IDX)


def _params(sem):
    return pltpu.CompilerParams(dimension_semantics=sem, vmem_limit_bytes=VMEM_LIMIT)


def _in_proj_kernel(x_ref, g_ref, w_ref, *refs):
    out_refs, chunk_sc = refs[:-1], refs[-1]
    x = x_ref[...]
    ms = jnp.mean(x * x, axis=-1, keepdims=True)
    xn = (x * lax.rsqrt(ms + NORM_EPS) * g_ref[...]).astype(BF16)
    off = 0
    refs = iter(out_refs)
    for _, width, _, parts in _SEGS:
        outs = [next(refs) for _ in range(parts)]
        for c0 in range(0, width, 512):
            cw = min(512, width - c0)
            val = jnp.dot(xn, w_ref[:, off + c0:off + c0 + cw], preferred_element_type=F32)
            if parts == 1:
                outs[0][:, c0:c0 + cw] = val.astype(outs[0].dtype)
            else:
                pw = width // parts
                per_slab = 128 // pw
                for h in range(width // 128):
                    chunk_sc[h] = val[:, h * 128:(h + 1) * 128]
                n_chunks = val.shape[0] // CMP_STRIDE
                for l in range(CMP_STRIDE):
                    for h in range(width // 128):
                        rows = chunk_sc[h, pl.ds(l, n_chunks, stride=CMP_STRIDE), :]
                        for jj in range(per_slab):
                            o_ref = outs[h * per_slab + jj]
                            o_ref[:, l * pw:(l + 1) * pw] = rows[:, jj * pw:(jj + 1) * pw].astype(o_ref.dtype)
        off += width


def _in_proj(x2d, g, w):
    m = x2d.shape[0]
    tm = ROW_TILE
    out_shape, out_specs, chunk_width = [], [], 0
    for _, width, dt, parts in _SEGS:
        if parts == 1:
            out_shape.append(jax.ShapeDtypeStruct((m, width), dt))
            out_specs.append(pl.BlockSpec((tm, width), lambda i: (i, 0)))
        else:
            assert width <= 512 and chunk_width in (0, width)
            chunk_width = width
            cols = width // parts * CMP_STRIDE
            out_shape += [jax.ShapeDtypeStruct((m // CMP_STRIDE, cols), dt)] * parts
            out_specs += [pl.BlockSpec((tm // CMP_STRIDE, cols), lambda i: (i, 0))] * parts
    return pl.pallas_call(
        _in_proj_kernel,
        grid=(m // tm,),
        in_specs=[
            pl.BlockSpec((tm, D_MODEL), lambda i: (i, 0)),
            pl.BlockSpec((1, D_MODEL), lambda i: (0, 0)),
            pl.BlockSpec((D_MODEL, _N_PROJ), lambda i: (0, 0), pipeline_mode=pl.Buffered(1)),
        ],
        out_specs=out_specs,
        out_shape=out_shape,
        scratch_shapes=[pltpu.VMEM((chunk_width // 128, tm, 128), F32)],
        compiler_params=_params(("parallel",)),
    )(x2d, g, w)


def _compress_kernel(*refs):
    z_refs, (pe_ref, w1_ref, w2_ref, o_ref) = refs[:-4], refs[-4:]
    half = CMP_STRIDE * HEAD_DIM
    for j, z_ref in enumerate(z_refs):
        kv = j // NSA_GROUPS
        z = z_ref[...]
        w1 = w1_ref[kv]
        first = _dot_exact_lhs(z, w1[:half], 3)
        second = _dot_exact_lhs(z, w1[half:], 3)
        n = z.shape[0]
        hid = first + pltpu.roll(second, n - 1, 0)
        hid = hid + _dot3(_split(pe_ref[kv]), _split(w1))
        act = hid * jax.nn.sigmoid(hid)
        o_ref[j] = _dot3(_split(act), _split(w2_ref[kv]))


def _compress(kv_cmp, pe, w1, w2):
    zs = list(kv_cmp)
    b, nch, _ = zs[0].shape

    def full(arr):
        return pl.BlockSpec(arr.shape, lambda i: (0,) * arr.ndim)

    return pl.pallas_call(
        _compress_kernel,
        grid=(b,),
        in_specs=[pl.BlockSpec((None, nch, CMP_STRIDE * HEAD_DIM), lambda i: (i, 0, 0)) for _ in zs]
        + [full(pe), full(w1), full(w2)],
        out_specs=pl.BlockSpec((None, len(zs), nch, HEAD_DIM), lambda i: (i, 0, 0, 0)),
        out_shape=jax.ShapeDtypeStruct((b, len(zs), nch, HEAD_DIM), F32),
        compiler_params=_params(("parallel",)),
    )(*zs, pe, w1, w2)


def _widen_queries(q, n_heads):
    zeros = jnp.zeros((q.shape[0], HEAD_DIM), q.dtype)
    return [jnp.concatenate([zeros, q[:, h * HEAD_DIM:(h + 1) * HEAD_DIM]], axis=1) for h in range(n_heads)]


def _pair_values(kv):
    swapped = jnp.concatenate([kv[:, HEAD_DIM:], kv[:, :HEAD_DIM]], axis=1)
    lane = lax.broadcasted_iota(jnp.int32, kv.shape, 1)
    low = jnp.where(lane < HEAD_DIM, 1.0, 0.0).astype(kv.dtype)
    high = jnp.where(lane < HEAD_DIM, 0.0, 1.0).astype(kv.dtype)
    even = jnp.concatenate([kv * low, low], axis=1)
    odd = jnp.concatenate([swapped * high, high], axis=1)
    return even, odd


def _pair_lanes(even, odd):
    low = lax.broadcasted_iota(jnp.int32, even.shape, 1) < HEAD_DIM
    return jnp.where(low, even, odd)


def _nt(a, b):
    return lax.dot_general(a, b, (((1,), (1,)), ((), ())), preferred_element_type=F32)


def _rounds(streams, stagger=0):
    results = [None] * len(streams)
    waiting = list(enumerate(streams))
    live = {}
    rounds = 0
    while waiting or live:
        while waiting and waiting[0][0] * stagger <= rounds:
            idx, stream = waiting.pop(0)
            live[idx] = stream
        for idx in list(live):
            try:
                next(live[idx])
            except StopIteration as stop:
                results[idx] = stop.value
                del live[idx]
        rounds += 1
        yield
    return results


def _interleave(*streams, stagger=0):
    rounds = _rounds(streams, stagger)
    while True:
        try:
            next(rounds)
        except StopIteration as stop:
            return stop.value


def _band_block(qa_even, qa_odd, keys, val_even, val_odd, bias, sinks=None):
    s = [_nt(qa, keys) + bias for qa in (qa_even, qa_odd)]
    yield
    m = [jnp.max(x, axis=-1, keepdims=True) for x in s]
    if sinks is not None:
        m = [jnp.maximum(x, sinks[e]) for e, x in enumerate(m)]
    p = [jnp.exp(x - mx).astype(BF16) for x, mx in zip(s, m)]
    yield
    acc = (jnp.dot(p[0], val_even, preferred_element_type=F32)
           + jnp.dot(p[1], val_odd, preferred_element_type=F32))
    denom = acc[:, 2 * HEAD_DIM:]
    if sinks is not None:
        shape = (acc.shape[0], 2 * HEAD_DIM)
        denom = denom + _pair_lanes(jnp.broadcast_to(jnp.exp(sinks[0] - m[0]), shape),
                                    jnp.broadcast_to(jnp.exp(sinks[1] - m[1]), shape))
    return acc[:, :2 * HEAD_DIM] / denom


def _band_blocks(blocks, sinks=None):
    return (yield from _rounds([_band_block(*blk, sinks) for blk in blocks], stagger=1))


def _band_bias_table(deltas, rows, width, window):
    r = np.arange(rows)[:, None]
    c = np.arange(width)[None, :]
    out = [np.where((c <= r + d) & (c > r + d - window), 0.0, NEG_INF) for d in deltas]
    return np.stack(out).astype(np.float32)


def _attention_kernel(sink_ref, q_ref, z_ref, gate_ref, kvs_ref, kvw_ref, kcvc_ref, bq_ref, bz_ref,
                      bkv_ref, negexp_ref, ovl_ref, wbias_ref, cbias_ref, gexp_ref, bbias_ref,
                      o_ref, ob_ref, srhs_sc, spe_sc, spo_sc, wpe_sc, wpo_sc, bpe_sc, bpo_sc, m_sc, acc_sc):
    i = pl.program_id(1)
    tq = ATT_TILE
    rb = ATT_ROWS
    n_rb = tq // rb
    n_cmp_pad = kcvc_ref.shape[1]
    n_slc = ovl_ref.shape[0]
    n_pairs = NSA_HG // 2
    groups = range(NSA_GROUPS)
    wd = NSA_HG * HEAD_DIM
    wb = SWA_HG * HEAD_DIM
    kv_lanes = [slice(g * 2 * HEAD_DIM, (g + 1) * 2 * HEAD_DIM) for g in groups]

    @pl.when(i == 0)
    def _():
        for g in groups:
            kvs = kvs_ref[:, kv_lanes[g]]
            srhs_sc[g] = jnp.concatenate([kvs, negexp_ref[...]], axis=1)
            spe_sc[g], spo_sc[g] = _pair_values(kvs)
            wpe_sc[g], wpo_sc[g] = _pair_values(kvw_ref[:, kv_lanes[g]])
            bpe_sc[g], bpo_sc[g] = _pair_values(bkv_ref[:, kv_lanes[g]])

    scale = jnp.asarray(HEAD_DIM ** -0.5, BF16)
    qas = [_widen_queries(q_ref[:, g * wd:(g + 1) * wd] * scale, NSA_HG) for g in groups]

    def band_inputs(g, kv_ref, pe_sc, po_sc, bias_ref, window, r0):
        q0 = i * tq + r0
        k0 = pl.multiple_of(jnp.maximum(q0 - window, 0), rb)
        width = window + rb
        variant = jnp.minimum(q0 // rb, bias_ref.shape[0] - 1)
        return (kv_ref[pl.ds(k0, width), kv_lanes[g]], pe_sc[g, pl.ds(k0, width), :],
                po_sc[g, pl.ds(k0, width), :], bias_ref[variant])

    def window_stream(g):
        bands = [band_inputs(g, kvw_ref, wpe_sc, wpo_sc, wbias_ref, NSA_WINDOW, r0) for r0 in range(0, tq, rb)]
        blocks = [(qas[g][2 * pair][r0:r0 + rb], qas[g][2 * pair + 1][r0:r0 + rb]) + bands[r0 // rb]
                  for pair in range(n_pairs) for r0 in range(0, tq, rb)]
        outs = yield from _band_blocks(blocks)
        return jnp.concatenate([jnp.concatenate(outs[pair * n_rb:(pair + 1) * n_rb], axis=0)
                                for pair in range(n_pairs)], axis=1)

    def sink_window_stream(g):
        lanes = slice(g * wb, (g + 1) * wb)
        qbs = _widen_queries(bq_ref[:, lanes] * scale, SWA_HG)
        sinks = (sink_ref[g * SWA_HG], sink_ref[g * SWA_HG + 1])
        blocks = [(qbs[0][r0:r0 + rb], qbs[1][r0:r0 + rb])
                  + band_inputs(g, bkv_ref, bpe_sc, bpo_sc, bbias_ref, SWA_WINDOW, r0) for r0 in range(0, tq, rb)]
        outs = yield from _band_blocks(blocks, sinks)
        y = jnp.concatenate(outs, axis=0)
        zb = bz_ref[:, lanes]
        ob_ref[:, lanes] = (y * (zb * jax.nn.sigmoid(zb))).astype(ob_ref.dtype)

    def compressed_stream(g):
        zeros_k = jnp.zeros((n_cmp_pad, HEAD_DIM), BF16)
        kc_both = jnp.concatenate([jnp.concatenate([zeros_k, t], axis=1) for t in _split(kcvc_ref[g])], axis=0)
        vc = kcvc_ref[NSA_GROUPS + g].astype(BF16)
        vc_even = jnp.concatenate([vc, zeros_k], axis=1)
        vc_odd = jnp.concatenate([zeros_k, vc], axis=1)
        tpos = i * tq + lax.broadcasted_iota(jnp.int32, (tq, n_cmp_pad), 0)
        cmp_end = lax.broadcasted_iota(jnp.int32, (tq, n_cmp_pad), 1) * CMP_STRIDE + (CMP_BLOCK - 1)
        valid_c = cmp_end <= tpos
        s = [_nt(qa, kc_both) for qa in qas[g]]
        s = [x[:, :n_cmp_pad] + x[:, n_cmp_pad:] for x in s]
        yield
        s = [jnp.where(valid_c, x, NEG_INF) for x in s]
        m = [jnp.max(x, axis=-1, keepdims=True) for x in s]
        yield
        p = [jnp.where(valid_c, jnp.exp(x - mx), 0.0) for x, mx in zip(s, m)]
        denom = [jnp.sum(x, axis=-1, keepdims=True) for x in p]
        yield
        p = [x * (1.0 / jnp.where(d > 0, d, 1.0)) for x, d in zip(p, denom)]
        p_sum = (p[0] + p[1]) + (p[2] + p[3])
        pb = [x.astype(BF16) for x in p]
        o_cmp = jnp.concatenate(
            [jnp.dot(pb[2 * pair], vc_even, preferred_element_type=F32)
             + jnp.dot(pb[2 * pair + 1], vc_odd, preferred_element_type=F32) for pair in range(n_pairs)], axis=1)
        yield
        imp = None
        rest = p_sum
        for _ in range(3):
            term = rest.astype(BF16)
            rest = rest - term.astype(F32)
            part = _nt(ovl_ref[...], term)
            imp = part if imp is None else imp + part
        yield
        blk = lax.broadcasted_iota(jnp.int32, (n_slc, tq), 0)
        cur = (i * tq + lax.broadcasted_iota(jnp.int32, (n_slc, tq), 1)) // SLC_BLOCK
        forced = (blk == 0) | (blk == cur) | (blk == cur - 1)
        score = jnp.where(forced, FORCE, jnp.where(blk <= cur, imp, -FORCE))
        ranks = [jnp.zeros((n_slc, tq), F32) for _ in range(4)]
        for r in range(n_slc):
            row = score[r:r + 1, :]
            ahead = (row > score) | ((row == score) & (r < blk))
            ranks[r % 4] = ranks[r % 4] + jnp.where(ahead, 1.0, 0.0)
            if r % 4 == 3:
                yield
        rank = (ranks[0] + ranks[1]) + (ranks[2] + ranks[3])
        n_sel = min(SLC_TOPN, n_slc)
        unsel_t = jnp.where((rank < n_sel) & (blk <= cur), 0.0, 1.0)
        unsel_t = jnp.concatenate([unsel_t, jnp.zeros((128 - n_slc, tq), F32)], axis=0)
        return o_cmp, jnp.transpose(unsel_t).astype(BF16)

    results = _interleave(*[stream(g) for g in groups
                            for stream in (window_stream, sink_window_stream, compressed_stream)])
    o_win = [results[3 * g] for g in groups]
    o_cmp = [results[3 * g + 2][0] for g in groups]
    unsel = [results[3 * g + 2][1] for g in groups]

    lhs = {(g, h): jnp.concatenate([qas[g][h], unsel[g]], axis=1) for g in groups for h in range(NSA_HG)}
    m_sc[...] = jnp.full(m_sc.shape, NEG_INF, F32)
    acc_sc[...] = jnp.zeros(acc_sc.shape, F32)
    subs = [(g, h, r0) for g in groups for h in range(NSA_HG) for r0 in range(0, tq, rb)]

    def slc_keys(start, width, bias_lanes):
        rhs = [srhs_sc[g, pl.ds(start, width), :] for g in groups]
        val_even = [spe_sc[g, pl.ds(start, width), :] for g in groups]
        val_odd = [spo_sc[g, pl.ds(start, width), :] for g in groups]
        s = {}
        for g, h, r0 in subs:
            s[g, h, r0] = _nt(lhs[g, h][r0:r0 + rb], rhs[g])
            if bias_lanes is not None:
                s[g, h, r0] = s[g, h, r0] + cbias_ref[r0:r0 + rb, bias_lanes:bias_lanes + width]
        m_prev = {(g, h, r0): m_sc[g * NSA_HG + h, r0:r0 + rb] for g, h, r0 in subs}
        m_next = {key: jnp.maximum(m_prev[key], jnp.max(s[key], axis=-1, keepdims=True)) for key in subs}
        p = {key: jnp.exp(s[key] - jnp.concatenate([m_next[key]] * (width // 128), axis=1)).astype(BF16)
             for key in subs}
        alpha = {key: jnp.exp(m_prev[key] - m_next[key]) for key in subs}
        for g, h, r0 in subs:
            m_sc[g * NSA_HG + h, r0:r0 + rb] = m_next[g, h, r0]
        for g in groups:
            for pair in range(n_pairs):
                for r0 in range(0, tq, rb):
                    rows = slice(r0, r0 + rb)
                    decay = _pair_lanes(alpha[g, 2 * pair, r0], alpha[g, 2 * pair + 1, r0])
                    decay = jnp.concatenate([decay, decay], axis=1)
                    acc_sc[g * n_pairs + pair, rows] = (
                        decay * acc_sc[g * n_pairs + pair, rows]
                        + jnp.dot(p[g, 2 * pair, r0], val_even[g], preferred_element_type=F32)
                        + jnp.dot(p[g, 2 * pair + 1, r0], val_odd[g], preferred_element_type=F32))

    def wide(j, carry):
        slc_keys(pl.multiple_of(j * (2 * tq), 2 * tq), 2 * tq, None)
        return carry

    lax.fori_loop(0, i // 2, wide, 0)

    @pl.when(i % 2 == 0)
    def _():
        slc_keys(pl.multiple_of(i * tq, tq), tq, tq)

    @pl.when(i % 2 == 1)
    def _():
        slc_keys(pl.multiple_of((i - 1) * tq, 2 * tq), 2 * tq, 0)

    for g in groups:
        o_slc = []
        for pair in range(n_pairs):
            acc = acc_sc[g * n_pairs + pair]
            o_slc.append(acc[:, :2 * HEAD_DIM] / acc[:, 2 * HEAD_DIM:])
        o_slc = jnp.concatenate(o_slc, axis=1)
        sg = _split(jax.nn.sigmoid(gate_ref[:, g * 128:(g + 1) * 128]))
        gates = (jnp.dot(sg[0], gexp_ref[...], preferred_element_type=F32)
                 + jnp.dot(sg[1], gexp_ref[...], preferred_element_type=F32))
        y = gates[:, :wd] * o_cmp[g] + gates[:, wd:2 * wd] * o_slc + gates[:, 2 * wd:] * o_win[g]
        z = z_ref[:, g * wd:(g + 1) * wd]
        o_ref[:, g * wd:(g + 1) * wd] = (y * (z * jax.nn.sigmoid(z))).astype(o_ref.dtype)


def _attention(q, z, gate, kv_slc, kv_win, kcvc, bq, bz, bkv, sinks):
    b, t, _ = q.shape
    tq = ATT_TILE
    assert t % (2 * tq) == 0 and t >= NSA_WINDOW + tq and NSA_WINDOW % tq == 0 and SWA_HG == 2
    assert NSA_GROUPS == SWA_GROUPS
    n_slc = t // SLC_BLOCK
    nch = t // CMP_STRIDE
    wd = NSA_HG * HEAD_DIM
    negexp = np.where(np.arange(t)[:, None] // SLC_BLOCK == np.arange(128)[None, :], -2.0 ** 100, 0.0)
    ci = np.arange(nch)[None, :] * CMP_STRIDE
    sj = np.arange(n_slc)[:, None] * SLC_BLOCK
    overlap_t = ((ci < sj + SLC_BLOCK) & (ci + CMP_BLOCK > sj)).astype(np.float32)
    rb = ATT_ROWS
    assert NSA_WINDOW % rb == 0 and SWA_WINDOW % rb == 0 and tq % rb == 0
    wbias = _band_bias_table(range(0, NSA_WINDOW + 1, rb), rb, NSA_WINDOW + rb, NSA_WINDOW)
    cbias = np.concatenate([np.zeros((tq, tq), np.float32), _band_bias_table([0], tq, tq, t)[0]], axis=1)
    bbias = _band_bias_table(range(0, SWA_WINDOW + 1, rb), rb, SWA_WINDOW + rb, SWA_WINDOW)
    gexp = np.zeros((128, 3 * wd), np.float32)
    for br in range(3):
        for h in range(NSA_HG):
            gexp[br * NSA_HG + h, br * wd + h * HEAD_DIM:br * wd + (h + 1) * HEAD_DIM] = 1.0

    def const(arr):
        return pl.BlockSpec(arr.shape, lambda bi, i: (0,) * arr.ndim)

    def tile(arr):
        return pl.BlockSpec((None, tq, arr.shape[-1]), lambda bi, i: (bi, i, 0))

    def whole(arr):
        return pl.BlockSpec((None,) + arr.shape[1:], lambda bi, i: (bi,) + (0,) * (arr.ndim - 1))

    consts = [jnp.asarray(negexp, BF16), jnp.asarray(overlap_t, BF16), jnp.asarray(wbias),
              jnp.asarray(cbias), jnp.asarray(gexp, BF16), jnp.asarray(bbias)]
    key_scratch = pltpu.VMEM((NSA_GROUPS, t, 256), BF16)
    return pl.pallas_call(
        _attention_kernel,
        grid=(b, t // tq),
        in_specs=[pl.BlockSpec(memory_space=pltpu.SMEM), tile(q), tile(z), tile(gate), whole(kv_slc),
                  whole(kv_win), whole(kcvc), tile(bq), tile(bz), whole(bkv)] + [const(a) for a in consts],
        out_specs=(pl.BlockSpec((None, tq, 512), lambda bi, i: (bi, i, 0)),
                   pl.BlockSpec((None, tq, 256), lambda bi, i: (bi, i, 0))),
        out_shape=(jax.ShapeDtypeStruct((b, t, 512), BF16), jax.ShapeDtypeStruct((b, t, 256), BF16)),
        scratch_shapes=[key_scratch] * 7 + [
            pltpu.VMEM((NSA_GROUPS * NSA_HG, tq, 128), F32),
            pltpu.VMEM((NSA_GROUPS * NSA_HG // 2, tq, 256), F32),
        ],
        compiler_params=_params(("parallel", "arbitrary")),
    )(sinks, q, z, gate, kv_slc, kv_win, kcvc, bq, bz, bkv, *consts)


_NN = (((1,), (0,)), ((), ()))
_NT = (((1,), (1,)), ((), ()))


def _split(a):
    hi = a.astype(BF16)
    lo = (a - hi.astype(F32)).astype(BF16)
    return hi, lo


def _dot(a, b, dims=_NN):
    return lax.dot_general(a, b, dims, preferred_element_type=F32)


def _dot3(a, b, dims=_NN):
    rows = a[0].shape[0]
    both = _dot(jnp.concatenate([a[0], a[1]], axis=0), b[0], dims)
    return (both[:rows] + both[rows:]) + _dot(a[0], b[1], dims)


def _dot_exact_lhs(a, b, n_terms):
    out = None
    for _ in range(n_terms):
        term = b.astype(BF16)
        b = b - term.astype(F32)
        part = _dot(a, term)
        out = part if out is None else out + part
    return out


def _dot_exact_rhs(x, ones):
    hi, lo = _split(x)
    return _dot(hi, ones) + _dot(lo, ones)


def _block_diag(pair, head_masks):
    return tuple(jnp.concatenate([x * m for m in head_masks], axis=0) for x in pair)


def _each(fn, *lists):
    return [fn(*items) for items in zip(*lists)]


def _unit_lower_inverse(lows, eye, same_sub, bd):
    diag = _each(lambda low: jnp.where(same_sub, low, 0.0), lows)
    off = _each(lambda low, d: low - d, lows, diag)
    inv = _each(lambda d: eye + d, diag)
    power = diag
    span = 2
    while span < RWKV_SUB:
        ps = _each(_split, power)
        power = _each(lambda p: _dot3(p, bd(p)), ps)
        yield
        inv = _each(lambda iv, pw: _dot3(_split(iv), bd(_split(eye + pw))), inv, power)
        yield
        span *= 2
    inv_s = _each(_split, inv)
    cross = _each(lambda iv, o: _dot3(iv, bd(_split(o))), inv_s, off)
    yield
    total = _each(lambda cr: eye + cr, cross)
    power = cross
    span = 2
    while span < RWKV_CHUNK // RWKV_SUB:
        ps = _each(_split, power)
        power = _each(lambda p: _dot3(p, bd(p)), ps)
        yield
        total = _each(lambda tt, pw: _dot3(_split(tt), bd(_split(eye + pw))), total, power)
        yield
        span *= 2
    return _each(lambda tt, iv: _dot3(_split(tt), bd(iv)), total, inv_s)


def _rwkv_chunk_maps(a_t, r_t, b_t, k_t, b_e, k_e, v, p_end, cst):
    C = RWKV_CHUNK
    eye, strict, incl, same_sub, head_masks = cst
    bd = lambda pair: _block_diag(pair, head_masks)

    ar = _each(lambda a, r: _split(jnp.concatenate([a, r], axis=0)), a_t, r_t)
    g_b = _each(lambda x, b: _dot3(x, bd(_split(b)), _NT), ar, b_t)
    g_k = _each(lambda x, k: _dot3(x, bd(_split(k)), _NT), ar, k_t)
    yield
    a_ab = _each(lambda g: jnp.where(strict, g[:C], 0.0), g_b)
    a_rb = _each(lambda g: _split(jnp.where(incl, g[C:], 0.0)), g_b)
    a_ak = _each(lambda g: _split(jnp.where(strict, g[:C], 0.0)), g_k)
    a_rk = _each(lambda g: _split(jnp.where(incl, g[C:], 0.0)), g_k)
    v_bd = _each(lambda x: bd(_split(x)), v)
    akv = _each(_dot3, a_ak, v_bd)
    yield
    t_inv = yield from _unit_lower_inverse(a_ab, eye, same_sub, bd)
    t_inv = _each(_split, t_inv)
    yield
    w_m = _each(lambda t, a: _dot3(t, bd(_split(a))), t_inv, a_t)
    u0 = _each(lambda t, x: _dot3(t, bd(_split(x))), t_inv, akv)
    yield
    r_m = _each(lambda r, g, w: r + _dot3(g, bd(_split(w))), r_t, a_rb, w_m)
    y0 = _each(lambda g, u, gk, vb: _dot3(g, bd(_split(u))) + _dot3(gk, vb), a_rb, u0, a_rk, v_bd)
    yield
    bk = _each(lambda b, k: _split(jnp.transpose(jnp.concatenate([b, k], axis=0))), b_e, k_e)
    m_add = _each(lambda x, w: _dot3((x[0][:, :C], x[1][:, :C]), _split(w)), bk, w_m)
    n_all = _each(lambda x, u, vv: _dot3(x, _split(jnp.concatenate([u, vv], axis=0))), bk, u0, v)
    yield
    keep = [m.astype(F32) for m in head_masks]

    def own_blocks(full):
        parts = [full[h * HEAD_DIM:(h + 1) * HEAD_DIM, :] * keep[h] for h in range(RWKV_HEADS)]
        return (parts[0] + parts[1]) + (parts[2] + parts[3])

    m_l = _each(lambda m, p: own_blocks(m) + eye * p, m_add, p_end)
    n_l = _each(own_blocks, n_all)
    lhs = _each(lambda r, m: _split(jnp.concatenate([r, m], axis=0)), r_m, m_l)
    return list(zip(lhs, y0, n_l))


def _rwkv_kernel(*refs, has_vres):
    if has_vres:
        (feat_ref, prev_ref, z_ref, vfirst_ref, mu_ref, w0_ref, w2_ref, a0_ref, a2_ref, kk_ref, ka_ref,
         rk_ref, lnw_ref, lnb_ref, hsum_ref, v0_ref, v1_ref, v2_ref, y_ref, state) = refs
    else:
        (feat_ref, prev_ref, z_ref, mu_ref, w0_ref, w2_ref, a0_ref, a2_ref, kk_ref, ka_ref,
         rk_ref, lnw_ref, lnb_ref, hsum_ref, y_ref, vout_ref, state) = refs
    c = pl.program_id(1)
    C = RWKV_CHUNK
    W = RWKV_WIDTH
    n_tok = feat_ref.shape[0]
    group = RWKV_GROUP_CHUNKS * C

    @pl.when(c == 0)
    def _():
        state[...] = jnp.zeros(state.shape, F32)

    lora_weights = {}

    def lora(x, w_ref):
        if id(w_ref) not in lora_weights:
            lora_weights[id(w_ref)] = _split(w_ref[...])
        return _dot3(_split(x), lora_weights[id(w_ref)])

    hsum = hsum_ref[...]

    def head_sum(x):
        return _dot_exact_rhs(x, hsum)

    feat = feat_ref[...]
    row = lax.broadcasted_iota(jnp.int32, feat.shape, 0)
    last_prev = jnp.where(c == 0, 0.0, prev_ref[7:8, :])
    prev = jnp.where(row == 0, last_prev, pltpu.roll(feat, 1, 0))
    xs = feat + (prev - feat) * mu_ref[...]

    ti = lax.broadcasted_iota(jnp.int32, (group, group), 0)
    tj = lax.broadcasted_iota(jnp.int32, (group, group), 1)
    tri = jnp.where((ti >= tj) & (ti // C == tj // C), 1.0, 0.0).astype(BF16)
    t_idx = lax.broadcasted_iota(jnp.int32, (C, W), 0)
    i_idx = lax.broadcasted_iota(jnp.int32, (C, W), 1) % HEAD_DIM
    lane_head = lax.broadcasted_iota(jnp.int32, (C, W), 1) // HEAD_DIM
    cst = (
        jnp.where(t_idx == i_idx, 1.0, 0.0),
        t_idx > i_idx,
        t_idx >= i_idx,
        (t_idx // RWKV_SUB) == (i_idx // RWKV_SUB),
        [jnp.where(lane_head == h, 1.0, 0.0).astype(BF16) for h in range(RWKV_HEADS)],
    )
    carry = {"state": state[...], "turn": 0}

    def token_group(idx):
        rows = slice(idx * group, (idx + 1) * group)
        x = xs[rows]
        r = x[:, :W]
        k = x[:, W:2 * W]
        v = x[:, 2 * W:3 * W]
        wd = x[:, 3 * W:3 * W + DECAY_LORA]
        ad = x[:, 3 * W + DECAY_LORA:]
        pre = -(w0_ref[...] + lora(jnp.tanh(wd), w2_ref))
        softplus = jnp.maximum(pre, 0.0) + jnp.log(1.0 + jnp.exp(-jnp.abs(pre)))
        logw = -jnp.exp(-softplus - 0.5)
        yield
        if has_vres:
            mix = jax.nn.sigmoid(v0_ref[...] + lora(lora(v, v1_ref), v2_ref))
            v = v + (vfirst_ref[rows, :] - v) * mix
        else:
            vout_ref[rows, :] = v
        alpha = jax.nn.sigmoid(a0_ref[...] + lora(ad, a2_ref))
        kk = k * kk_ref[...]
        kk = kk / jnp.maximum(jnp.sqrt(head_sum(kk * kk)), 1e-12)
        k = k * (1.0 + (alpha - 1.0) * ka_ref[...])
        yield
        cum = _dot_exact_lhs(tri, logw, 3)
        yield
        a_t = -kk * jnp.exp(cum - logw)
        r_t = r * jnp.exp(cum)
        inv_decay = jnp.exp(-cum)
        b_raw = kk * alpha
        b_t = b_raw * inv_decay
        k_t = k * inv_decay
        chunks = [slice(j * C, (j + 1) * C) for j in range(RWKV_GROUP_CHUNKS)]
        cum_end = [cum[cr.stop - 1:cr.stop, :] for cr in chunks]
        to_end = [jnp.exp(ce - cum[cr]) for ce, cr in zip(cum_end, chunks)]
        yield
        maps = yield from _rwkv_chunk_maps(
            [a_t[cr] for cr in chunks], [r_t[cr] for cr in chunks], [b_t[cr] for cr in chunks],
            [k_t[cr] for cr in chunks], [b_raw[cr] * te for cr, te in zip(chunks, to_end)],
            [k[cr] * te for cr, te in zip(chunks, to_end)], [v[cr] for cr in chunks],
            [jnp.exp(ce) for ce in cum_end], cst)
        while carry["turn"] != idx:
            yield
        s_l = carry["state"]
        ys = []
        for lhs, y0, n_l in maps:
            out = _dot3(lhs, _block_diag(_split(s_l), cst[4]))
            ys.append(out[:C] + y0)
            s_l = out[C:] + n_l
        carry["state"] = s_l
        carry["turn"] = idx + 1
        yield
        y = jnp.concatenate(ys, axis=0)
        inv_n = 1.0 / HEAD_DIM
        mean = head_sum(y) * inv_n
        cen = y - mean
        yield
        var = head_sum(cen * cen) * inv_n
        y = cen * lax.rsqrt(var + GN_EPS) * lnw_ref[...] + lnb_ref[...]
        y = y + head_sum(r * k * rk_ref[...]) * v
        z = z_ref[rows, :]
        y_ref[rows, :] = (y * (z * jax.nn.sigmoid(z))).astype(y_ref.dtype)

    _interleave(*[token_group(idx) for idx in range(n_tok // group)], stagger=RWKV_STAGGER)
    state[...] = carry["state"]


def _rwkv(feat, z, v_first, p):
    b, t, _ = feat.shape
    n_tok = RWKV_STEP_CHUNKS * RWKV_CHUNK
    assert t % n_tok == 0 and RWKV_CHUNK == HEAD_DIM
    W = RWKV_WIDTH
    has_vres = v_first is not None
    hsum = (np.arange(W)[:, None] // HEAD_DIM == np.arange(W)[None, :] // HEAD_DIM).astype(np.float32)

    def tok(width):
        return pl.BlockSpec((None, n_tok, width), lambda bi, c: (bi, c, 0))

    def full(arr):
        return pl.BlockSpec(arr.shape, lambda bi, c: (0,) * arr.ndim)

    args = [feat, feat, z]
    specs = [tok(SHIFT_WIDTH),
             pl.BlockSpec((None, 8, SHIFT_WIDTH), lambda bi, c: (bi, jnp.maximum(c * (n_tok // 8) - 1, 0), 0)),
             tok(W)]
    if has_vres:
        args.append(v_first)
        specs.append(tok(W))
    names = ["mu", "w0", "w2", "a0", "a2", "k_k", "k_a", "r_k", "ln_w", "ln_b"]
    consts = [p[n] for n in names] + [jnp.asarray(hsum, BF16)]
    if has_vres:
        consts += [p["v0"], p["v1"], p["v2"]]
    args += consts
    specs += [full(a) for a in consts]
    y_shape = jax.ShapeDtypeStruct((b, t, W), BF16)
    if has_vres:
        out_shape, out_specs = y_shape, tok(W)
    else:
        out_shape = (y_shape, jax.ShapeDtypeStruct((b, t, W), F32))
        out_specs = (tok(W), tok(W))
    res = pl.pallas_call(
        functools.partial(_rwkv_kernel, has_vres=has_vres),
        grid=(b, t // n_tok),
        in_specs=specs,
        out_specs=out_specs,
        out_shape=out_shape,
        scratch_shapes=[pltpu.VMEM((HEAD_DIM, W), F32)],
        compiler_params=_params(("parallel", "arbitrary")),
    )(*args)
    if has_vres:
        return res, v_first
    return res[0], res[1]


def _merge_kernel(x_ref, ya_ref, yb_ref, yc_ref, g_ref, wm_ref, bm_ref, pa_ref, pb_ref, pc_ref, wo_ref, fg_ref,
                  o_ref, *, final_norm):
    d = D_MODEL
    x = x_ref[...]
    ms = jnp.mean(x * x, axis=-1, keepdims=True)
    xn = (x * lax.rsqrt(ms + NORM_EPS) * g_ref[...]).astype(BF16)
    mixed = None
    for j, (y_ref, p_ref) in enumerate(((ya_ref, pa_ref), (yb_ref, pb_ref), (yc_ref, pc_ref))):
        cols = slice(j * d, (j + 1) * d)
        logits = jnp.dot(xn, wm_ref[:, cols], preferred_element_type=F32) + bm_ref[:, cols]
        term = jax.nn.sigmoid(logits) * jnp.dot(y_ref[...], p_ref[...], preferred_element_type=F32)
        mixed = term if mixed is None else mixed + term
    x = x + jnp.dot(mixed.astype(BF16), wo_ref[...], preferred_element_type=F32)
    if final_norm:
        ms = jnp.mean(x * x, axis=-1, keepdims=True)
        x = x * lax.rsqrt(ms + NORM_EPS) * fg_ref[...]
    o_ref[...] = x


def _merge(x2d, ya, yb, yc, g, wm, bm, pa, pb, pc, wo, fg, final_norm):
    m = x2d.shape[0]
    tm = ROW_TILE

    def rows(arr):
        return pl.BlockSpec((tm, arr.shape[-1]), lambda i: (i, 0))

    def full(arr):
        return pl.BlockSpec(arr.shape, lambda i: (0, 0))

    consts = [g, wm, bm, pa, pb, pc, wo, fg]
    return pl.pallas_call(
        functools.partial(_merge_kernel, final_norm=final_norm),
        grid=(m // tm,),
        in_specs=[rows(a) for a in (x2d, ya, yb, yc)] + [full(a) for a in consts],
        out_specs=rows(x2d),
        out_shape=jax.ShapeDtypeStruct((m, D_MODEL), F32),
        compiler_params=_params(("parallel",)),
    )(x2d, ya, yb, yc, *consts)


def kernel(x, norm_g, w_in, b_merge, cmp_pe_k, cmp_w1_k, cmp_w2_k, cmp_pe_v, cmp_w1_v, cmp_w2_v, swa_sinks,
           rwkv_mu, rwkv_w0, rwkv_w2, rwkv_a0, rwkv_a2, rwkv_k_k, rwkv_k_a, rwkv_r_k, rwkv_ln_w, rwkv_ln_b,
           rwkv_v0, rwkv_v1, rwkv_v2, proj_a, proj_b, proj_c, w_out, final_g):
    b, t, d = x.shape
    depth = w_in.shape[0]
    m = b * t
    x2d = x.reshape(m, d)
    o_merge = _ref_offsets()["merge"]
    v_first = None
    for l in range(depth):
        w = jnp.concatenate([jnp.zeros((d, n), BF16) if start == N_IN else w_in[l][:, start:start + n].astype(BF16)
                             for start, n in _COL_RUNS], axis=1)
        g_l = norm_g[l].reshape(1, d)
        outs = iter(_in_proj(x2d, g_l, w))
        seg = {}
        for name, _, _, parts in _SEGS:
            arrs = [next(outs) for _ in range(parts)]
            arrs = [a.reshape(b, -1, a.shape[-1]) for a in arrs]
            seg[name] = arrs[0] if parts == 1 else arrs

        pe = jnp.stack([cmp_pe_k[l], cmp_pe_v[l]]).reshape(2, 1, CMP_BLOCK * HEAD_DIM)
        kcvc = _compress(seg["a_kv_cmp"], pe, jnp.stack([cmp_w1_k[l], cmp_w1_v[l]]),
                         jnp.stack([cmp_w2_k[l], cmp_w2_v[l]]))
        y_a, y_b = _attention(seg["a_q"], seg["a_z"], seg["a_gate"], seg["a_kv_slc"], seg["a_kv_win"], kcvc,
                              seg["b_q"], seg["b_z"], seg["b_kv"], swa_sinks[l])

        row = lambda a: a.reshape(1, -1)
        p = {"mu": row(rwkv_mu[l]), "w0": row(rwkv_w0[l]), "w2": rwkv_w2[l], "a0": row(rwkv_a0[l]),
             "a2": rwkv_a2[l], "k_k": row(rwkv_k_k[l]), "k_a": row(rwkv_k_a[l]), "r_k": row(rwkv_r_k[l]),
             "ln_w": row(rwkv_ln_w[l]), "ln_b": row(rwkv_ln_b[l])}
        if l > 0:
            p.update(v0=row(rwkv_v0[l - 1]), v1=rwkv_v1[l - 1], v2=rwkv_v2[l - 1])
        y_c, v_first = _rwkv(seg["c_shift"], seg["c_z"], v_first, p)

        w_merge = w_in[l][:, o_merge:o_merge + 3 * d].astype(BF16)
        x2d = _merge(x2d, y_a.reshape(m, -1), y_b.reshape(m, -1), y_c.reshape(m, -1), g_l, w_merge,
                     b_merge[l].reshape(1, -1), proj_a[l].astype(BF16), proj_b[l].astype(BF16),
                     proj_c[l].astype(BF16), w_out[l].astype(BF16), final_g.reshape(1, d),
                     final_norm=(l == depth - 1))
    return x2d.reshape(b, t, d)
```
